```python
import jax, jax.numpy as jnp
from jax import lax
import numpy as np

D_MODEL = 1024
BATCH = 4
SEQ = 8192
DEPTH = 2
DEC_BATCH = 32
DEC_SEQ = 4
PAST_LEN = 16384
PAGE_SIZE = 128

D_MIX = D_MODEL
D_CONV = D_MIX // 4
CONV_WIDTH = 31
D_LRU = D_MIX // 4
LRU_BLOCKS = 4
LRU_BW = D_LRU // LRU_BLOCKS
LRU_CONV_WIDTH = 4
LRU_C = 8.0
D_ATTN = D_MIX // 2
N_HEADS = 8
HEAD_DIM = D_ATTN // N_HEADS
N_KV_HEADS = 2
KV_DIM = N_KV_HEADS * HEAD_DIM
ROPE_DIM = HEAD_DIM // 4
ROPE_THETA = 500000.0
CMP_BLOCK = 32
CMP_STRIDE = 16
CMP_HIDDEN = 256
SEL_BLOCK = 64
N_SEL = 16
WINDOW = 512
Q_BLOCK = 128
D_FF = 2816
ALPHA = (2 * DEPTH) ** 0.25
BETA = (8 * DEPTH) ** -0.25
LN_EPS = 1e-5
PROJ_SIZES = (2 * D_CONV, D_LRU, D_LRU, D_ATTN, 2 * KV_DIM, 2 * KV_DIM, 2 * KV_DIM, 3 * N_HEADS)
D_IN = sum(PROJ_SIZES)

kernel_name = "hymba_conformer_griffin_nsa_step"


def _layernorm(x, g, b):
    xf = x.astype(jnp.float32)
    mu = xf.mean(-1, keepdims=True)
    var = jnp.square(xf - mu).mean(-1, keepdims=True)
    return ((xf - mu) * lax.rsqrt(var + LN_EPS) * g + b).astype(x.dtype)


def _swiglu(x, w_in, w_out):
    g, u = jnp.split(x @ w_in, 2, axis=-1)
    return (jax.nn.silu(g) * u) @ w_out


def _masked_softmax(s, mask):
    s = jnp.where(mask, s.astype(jnp.float32), -jnp.inf)
    m = jnp.max(s, axis=-1, keepdims=True)
    m = jnp.where(jnp.isfinite(m), m, 0.0)
    e = jnp.exp(s - m)
    d = e.sum(-1, keepdims=True)
    return e / jnp.where(d > 0, d, 1.0)


def _partial_rope(x, pos):
    half = ROPE_DIM // 2
    inv = ROPE_THETA ** (-jnp.arange(half, dtype=jnp.float32) / half)
    ang = pos.astype(jnp.float32)[:, None] * inv[None, :]
    cos = jnp.cos(ang)[None, :, None, :]
    sin = jnp.sin(ang)[None, :, None, :]
    xr = x[..., :ROPE_DIM].astype(jnp.float32)
    x1, x2 = xr[..., :half], xr[..., half:]
    rot = jnp.concatenate([x1 * cos - x2 * sin, x2 * cos + x1 * sin], -1).astype(x.dtype)
    return jnp.concatenate([rot, x[..., ROPE_DIM:]], -1)


def _causal_dwconv(buf, u, w, b):
    full = jnp.concatenate([buf.astype(u.dtype), u], axis=1)
    y = lax.conv_general_dilated(full, w[:, None, :].astype(u.dtype), window_strides=(1,), padding='VALID',
                                 dimension_numbers=('NWC', 'WIO', 'NWC'), feature_group_count=u.shape[-1])
    return y + b.astype(u.dtype), full[:, full.shape[1] - (w.shape[0] - 1):]


def _rg_lru(xc, h0, w_gate, b_gate, lam):
    B, T, _ = xc.shape
    xf = xc.astype(jnp.float32)
    xb = xf.reshape(B, T, LRU_BLOCKS, LRU_BW)
    gts = jnp.einsum('btnd,gnde->gbtne', xb, w_gate.astype(jnp.float32)).reshape(2, B, T, D_LRU)
    gts = gts + b_gate.astype(jnp.float32)[:, None, None, :]
    r_gate = jax.nn.sigmoid(gts[0])
    i_gate = jax.nn.sigmoid(gts[1])
    log_a = LRU_C * r_gate * jax.nn.log_sigmoid(lam.astype(jnp.float32))
    a = jnp.exp(log_a)
    bterm = jnp.sqrt(-jnp.expm1(2.0 * log_a)) * (i_gate * xf)
    bterm = bterm.at[:, 0].add(a[:, 0] * h0.astype(jnp.float32))

    def comb(left, right):
        a1, b1 = left
        a2, b2 = right
        return a1 * a2, a2 * b1 + b2

    _, h = lax.associative_scan(comb, (a, bterm), axis=1)
    return h, h[:, -1]


def _compress(rows, pe, w1, b1, w2, b2):
    B, Tk = rows.shape[:2]
    n_chunk = Tk // CMP_STRIDE
    halves = CMP_BLOCK // CMP_STRIDE
    n_blk = n_chunk - halves + 1
    ch = rows[:, :n_chunk * CMP_STRIDE].reshape(B, n_chunk, CMP_STRIDE, 2, N_KV_HEADS, HEAD_DIM)
    w1r = w1.reshape(2, halves, CMP_STRIDE, HEAD_DIM, CMP_HIDDEN)
    per = pe.reshape(2, halves, CMP_STRIDE, HEAD_DIM)
    h = b1[:, None, :]
    for j in range(halves):
        pe_j = jnp.moveaxis(per[:, j], 0, 1)[:, :, None, :]
        part = jnp.einsum('bcrskd,srdh->bcskh', ch + pe_j, w1r[:, j])
        h = h + part[:, j:j + n_blk]
    out = jnp.einsum('bcskh,she->bcske', jax.nn.gelu(h), w2) + b2[:, None, :]
    cmp_end = jnp.arange(n_blk) * CMP_STRIDE + CMP_BLOCK - 1
    return out[:, :, 0], out[:, :, 1], cmp_end


def _nsa_block(q, qr, qpos, gates, kc, vc, cmp_end, sk, sv, wk, wv, wpos):
    B, QB = q.shape[:2]
    G = N_HEADS // N_KV_HEADS
    scale = HEAD_DIM ** -0.5
    qg = q.reshape(B, QB, N_KV_HEADS, G, HEAD_DIM)
    qrg = qr.reshape(B, QB, N_KV_HEADS, G, HEAD_DIM)
    s = jnp.einsum('bqkgd,bckd->bqkgc', qg, kc).astype(jnp.float32) * scale
    m = (cmp_end[None, :] <= qpos[:, None])[None, :, None, None, :]
    p_cmp = _masked_softmax(s, m)
    o_cmp = jnp.einsum('bqkgc,bckd->bqkgd', p_cmp, vc.astype(jnp.float32))
    p_kv = p_cmp.sum(axis=3)
    n_c = p_kv.shape[-1]
    n_sel_blk = sk.shape[2]
    ratio = SEL_BLOCK // CMP_STRIDE
    lead = CMP_BLOCK // CMP_STRIDE - 1
    length = ratio * n_sel_blk + ratio + lead
    pc = jnp.pad(p_kv, ((0, 0), (0, 0), (0, 0), (lead, length - lead - n_c)))
    slc = jnp.zeros(p_kv.shape[:3] + (n_sel_blk,), jnp.float32)
    for o in range(-lead, ratio):
        w_ov = float(min(CMP_STRIDE * o + CMP_BLOCK, SEL_BLOCK) - max(CMP_STRIDE * o, 0)) / CMP_STRIDE
        slc = slc + w_ov * pc[..., lead + o: lead + o + ratio * n_sel_blk: ratio]
    jblk = jnp.arange(n_sel_blk)[None, :]
    qblk = (qpos // SEL_BLOCK)[:, None]
    forced = (jblk == 0) | (jblk == qblk) | (jblk == qblk - 1)
    valid = jblk * SEL_BLOCK <= qpos[:, None]
    score = jnp.where(forced[None, :, None, :], jnp.inf, jnp.where(valid[None, :, None, :], slc, -jnp.inf))
    k_sel = min(N_SEL, n_sel_blk)
    _, idx = lax.top_k(score, k_sel)
    bi = jnp.arange(B)[:, None, None, None]
    hi = jnp.arange(N_KV_HEADS)[None, None, :, None]
    ks_g = sk[bi, hi, idx].reshape(B, QB, N_KV_HEADS, k_sel * SEL_BLOCK, HEAD_DIM)
    vs_g = sv[bi, hi, idx].reshape(B, QB, N_KV_HEADS, k_sel * SEL_BLOCK, HEAD_DIM)
    kpos = (idx[..., None] * SEL_BLOCK + jnp.arange(SEL_BLOCK)).reshape(B, QB, N_KV_HEADS, k_sel * SEL_BLOCK)
    s = jnp.einsum('bqkgd,bqknd->bqkgn', qrg, ks_g).astype(jnp.float32) * scale
    m = (kpos <= qpos[None, :, None, None])[:, :, :, None, :]
    o_sel = jnp.einsum('bqkgn,bqknd->bqkgd', _masked_softmax(s, m), vs_g.astype(jnp.float32))
    s = jnp.einsum('bqkgd,btkd->bqkgt', qrg, wk).astype(jnp.float32) * scale
    dpos = qpos[:, None] - wpos[None, :]
    m = ((dpos >= 0) & (dpos <= WINDOW) & (wpos[None, :] >= 0))[None, :, None, None, :]
    o_win = jnp.einsum('bqkgt,btkd->bqkgd', _masked_softmax(s, m), wv.astype(jnp.float32))
    g = gates.reshape(B, QB, N_KV_HEADS, G, 3)
    o = g[..., 0:1] * o_cmp + g[..., 1:2] * o_sel + g[..., 2:3] * o_win
    return o.reshape(B, QB, D_ATTN).astype(q.dtype)


def _mixer(x, pos0, conv_buf, lru_buf, lru_h, past_cmp, past_sel, win_buf, p):
    B, T, _ = x.shape
    dt = x.dtype
    splits = [int(v) for v in np.cumsum(PROJ_SIZES)[:-1]]
    glu, lru_x, lru_g, q, kv_c, kv_s, kv_w, g = jnp.split(x @ p['w_in'], splits, axis=-1)
    a_half, b_half = jnp.split(glu, 2, axis=-1)
    u = a_half * jax.nn.sigmoid(b_half)
    yc, new_conv = _causal_dwconv(conv_buf, u, p['conv_w'], p['conv_b'])
    yc = jax.nn.silu(_layernorm(yc, p['conv_ln_g'], p['conv_ln_b']))
    xl, new_lru_buf = _causal_dwconv(lru_buf, lru_x, p['lru_conv_w'], p['lru_conv_b'])
    hl, h_last = _rg_lru(xl, lru_h, p['lru_w_gate'], p['lru_b_gate'], p['lru_lambda'])
    yl = (hl * jax.nn.gelu(lru_g.astype(jnp.float32))).astype(dt)
    pos = pos0 + jnp.arange(T)
    q = q.reshape(B, T, N_HEADS, HEAD_DIM)
    qr = _partial_rope(q, pos)
    kv_c = kv_c.reshape(B, T, 2, N_KV_HEADS, HEAD_DIM)
    kv_s = kv_s.reshape(B, T, 2, N_KV_HEADS, HEAD_DIM)
    kv_s = jnp.stack([_partial_rope(kv_s[:, :, 0], pos), kv_s[:, :, 1]], axis=2)
    kv_w = kv_w.reshape(B, T, 2, N_KV_HEADS, HEAD_DIM)
    kv_w = jnp.stack([_partial_rope(kv_w[:, :, 0], pos), kv_w[:, :, 1]], axis=2)
    all_c = jnp.concatenate([past_cmp.astype(dt), kv_c], axis=1)
    all_s = jnp.concatenate([past_sel.astype(dt), kv_s], axis=1)
    kc, vc, cmp_end = _compress(all_c, p['cmp_pe'], p['cmp_w1'], p['cmp_b1'], p['cmp_w2'], p['cmp_b2'])
    tk = all_s.shape[1]
    n_sel_blk = -(-tk // SEL_BLOCK)
    sel = jnp.pad(all_s, ((0, 0), (0, n_sel_blk * SEL_BLOCK - tk), (0, 0), (0, 0), (0, 0)))
    sel = sel.reshape(B, n_sel_blk, SEL_BLOCK, 2, N_KV_HEADS, HEAD_DIM).transpose(3, 0, 4, 1, 2, 5)
    win_all = jnp.concatenate([win_buf.astype(dt), kv_w], axis=1)
    wb = win_buf.shape[1]
    win_pad = jnp.pad(win_all, ((0, 0), (WINDOW - wb, 0), (0, 0), (0, 0), (0, 0)))
    new_win = win_all[:, win_all.shape[1] - min(WINDOW, win_all.shape[1]):]
    gates = jax.nn.sigmoid(g.astype(jnp.float32)).reshape(B, T, N_HEADS, 3)
    qb_len = Q_BLOCK if T % Q_BLOCK == 0 else T
    n_qb = T // qb_len

    def blk(t):
        return jnp.moveaxis(t.reshape((B, n_qb, qb_len) + t.shape[2:]), 1, 0)

    def body(args):
        qb, qrb, gb, q0 = args
        qpos = pos0 + q0 + jnp.arange(qb_len)
        wkv = lax.dynamic_slice_in_dim(win_pad, q0, WINDOW + qb_len, axis=1)
        wpos = pos0 - WINDOW + q0 + jnp.arange(WINDOW + qb_len)
        return _nsa_block(qb, qrb, qpos, gb, kc, vc, cmp_end, sel[0], sel[1], wkv[:, :, 0], wkv[:, :, 1], wpos)

    o = lax.map(body, (blk(q), blk(qr), blk(gates), jnp.arange(n_qb) * qb_len))
    ya = jnp.moveaxis(o, 0, 1).reshape(B, T, D_ATTN)
    y = jnp.concatenate([yc, yl, ya], axis=-1) @ p['w_out']
    return y, (new_conv, new_lru_buf, h_last, kv_c, kv_s, new_win)


def _layer(x, pos0, conv_buf, lru_buf, lru_h, past_cmp, past_sel, win_buf, p):
    x = _layernorm(ALPHA * x + 0.5 * _swiglu(x, p['ffn_w_in'][0], p['ffn_w_out'][0]), p['ln_g'][0], p['ln_b'][0])
    y, st = _mixer(x, pos0, conv_buf, lru_buf, lru_h, past_cmp, past_sel, win_buf, p)
    x = _layernorm(ALPHA * x + y, p['ln_g'][1], p['ln_b'][1])
    x = _layernorm(ALPHA * x + 0.5 * _swiglu(x, p['ffn_w_in'][1], p['ffn_w_out'][1]), p['ln_g'][2], p['ln_b'][2])
    return x, st


def setup_inputs(seed: int = 0) -> dict:
    key = jax.random.key(seed)
    ks = jax.random.split(key, 32)
    f32 = jnp.float32
    n_pages = PAST_LEN // PAGE_SIZE
    n_pool = (DEC_BATCH * n_pages * 5) // 4
    win_buf = min(WINDOW, PAST_LEN)

    def nrm(k, shape, s=1.0):
        return jax.random.normal(k, shape, f32) * s

    kvs = (2, N_KV_HEADS, HEAD_DIM)
    page_table = jax.random.permutation(ks[8], n_pool)[:DEC_BATCH * n_pages].reshape(DEC_BATCH, n_pages).astype(jnp.int32)
    a_c = jax.random.uniform(ks[22], (DEPTH, D_LRU), f32, 0.9, 0.999)
    s_l = a_c ** (1.0 / LRU_C)
    return {
        'x_prompt': nrm(ks[0], (BATCH, SEQ, D_MODEL)),
        'x_sample': nrm(ks[1], (DEC_BATCH, DEC_SEQ, D_MODEL)),
        'state_conv': nrm(ks[2], (DEPTH, DEC_BATCH, CONV_WIDTH - 1, D_CONV)),
        'state_lru_conv': nrm(ks[3], (DEPTH, DEC_BATCH, LRU_CONV_WIDTH - 1, D_LRU)),
        'state_lru_h': nrm(ks[4], (DEPTH, DEC_BATCH, D_LRU), 0.5),
        'cache_cmp_kv': nrm(ks[5], (DEPTH, n_pool, PAGE_SIZE) + kvs),
        'cache_sel_kv': nrm(ks[6], (DEPTH, n_pool, PAGE_SIZE) + kvs),
        'cache_win_kv': nrm(ks[7], (DEPTH, DEC_BATCH, win_buf) + kvs),
        'page_table': page_table,
        'ln_g': 1.0 + nrm(ks[9], (DEPTH, 3, D_MODEL), 0.02),
        'ln_b': nrm(ks[10], (DEPTH, 3, D_MODEL), 0.02),
        'ffn_w_in': nrm(ks[11], (DEPTH, 2, D_MODEL, 2 * D_FF), D_MODEL ** -0.5),
        'ffn_w_out': nrm(ks[12], (DEPTH, 2, D_FF, D_MODEL), D_FF ** -0.5 * BETA),
        'w_in': nrm(ks[13], (DEPTH, D_MODEL, D_IN), D_MODEL ** -0.5),
        'conv_w': nrm(ks[14], (DEPTH, CONV_WIDTH, D_CONV), CONV_WIDTH ** -0.5),
        'conv_b': nrm(ks[15], (DEPTH, D_CONV), 0.02),
        'conv_ln_g': 1.0 + nrm(ks[16], (DEPTH, D_CONV), 0.02),
        'conv_ln_b': nrm(ks[17], (DEPTH, D_CONV), 0.02),
        'lru_conv_w': nrm(ks[18], (DEPTH, LRU_CONV_WIDTH, D_LRU), LRU_CONV_WIDTH ** -0.5),
        'lru_conv_b': nrm(ks[19], (DEPTH, D_LRU), 0.02),
        'lru_w_gate': nrm(ks[20], (DEPTH, 2, LRU_BLOCKS, LRU_BW, LRU_BW), LRU_BW ** -0.5),
        'lru_b_gate': nrm(ks[21], (DEPTH, 2, D_LRU), 0.02),
        'lru_lambda': jnp.log(s_l) - jnp.log1p(-s_l),
        'cmp_pe': nrm(ks[23], (DEPTH, 2, CMP_BLOCK, HEAD_DIM), 0.5),
        'cmp_w1': nrm(ks[24], (DEPTH, 2, CMP_BLOCK * HEAD_DIM, CMP_HIDDEN), (CMP_BLOCK * HEAD_DIM) ** -0.5),
        'cmp_b1': nrm(ks[25], (DEPTH, 2, CMP_HIDDEN), 0.02),
        'cmp_w2': nrm(ks[26], (DEPTH, 2, CMP_HIDDEN, HEAD_DIM), CMP_HIDDEN ** -0.5),
        'cmp_b2': nrm(ks[27], (DEPTH, 2, HEAD_DIM), 0.02),
        'w_out': nrm(ks[28], (DEPTH, D_MIX, D_MODEL), D_MIX ** -0.5 * BETA),
    }


def reference(x_prompt, x_sample, state_conv, state_lru_conv, state_lru_h, cache_cmp_kv, cache_sel_kv, cache_win_kv,
              page_table, ln_g, ln_b, ffn_w_in, ffn_w_out, w_in, conv_w, conv_b, conv_ln_g, conv_ln_b,
              lru_conv_w, lru_conv_b, lru_w_gate, lru_b_gate, lru_lambda, cmp_pe, cmp_w1, cmp_b1, cmp_w2, cmp_b2, w_out):
    dt = x_prompt.dtype
    bp = x_prompt.shape[0]
    bs = x_sample.shape[0]
    kvs = (2, N_KV_HEADS, HEAD_DIM)
    n_pages = page_table.shape[1]
    xp, xs = x_prompt, x_sample
    new_p, new_s = [], []
    for l in range(DEPTH):
        lp = {'ln_g': ln_g[l], 'ln_b': ln_b[l], 'ffn_w_in': ffn_w_in[l], 'ffn_w_out': ffn_w_out[l], 'w_in': w_in[l],
              'conv_w': conv_w[l], 'conv_b': conv_b[l], 'conv_ln_g': conv_ln_g[l], 'conv_ln_b': conv_ln_b[l],
              'lru_conv_w': lru_conv_w[l], 'lru_conv_b': lru_conv_b[l], 'lru_w_gate': lru_w_gate[l],
              'lru_b_gate': lru_b_gate[l], 'lru_lambda': lru_lambda[l], 'cmp_pe': cmp_pe[l], 'cmp_w1': cmp_w1[l],
              'cmp_b1': cmp_b1[l], 'cmp_w2': cmp_w2[l], 'cmp_b2': cmp_b2[l], 'w_out': w_out[l]}
        empty_kv = jnp.zeros((bp, 0) + kvs, dt)
        xp, sp = _layer(xp, 0, jnp.zeros((bp, CONV_WIDTH - 1, D_CONV), dt),
                        jnp.zeros((bp, LRU_CONV_WIDTH - 1, D_LRU), dt), jnp.zeros((bp, D_LRU), jnp.float32),
                        empty_kv, empty_kv, empty_kv, lp)
        past_c = cache_cmp_kv[l][page_table].reshape((bs, n_pages * PAGE_SIZE) + kvs)
        past_s = cache_sel_kv[l][page_table].reshape((bs, n_pages * PAGE_SIZE) + kvs)
        xs, ss = _layer(xs, PAST_LEN, state_conv[l], state_lru_conv[l], state_lru_h[l], past_c, past_s,
                        cache_win_kv[l], lp)
        new_p.append(sp)
        new_s.append(ss)
    conv_p = jnp.stack([s[0] for s in new_p])
    conv_s = jnp.stack([s[0] for s in new_s])
    lru_conv_p = jnp.stack([s[1] for s in new_p])
    lru_conv_s = jnp.stack([s[1] for s in new_s])
    lru_h_p = jnp.stack([s[2] for s in new_p])
    lru_h_s = jnp.stack([s[2] for s in new_s])
    cmp_kv_p = jnp.stack([s[3] for s in new_p])
    cmp_kv_s = jnp.stack([s[3] for s in new_s])
    sel_kv_p = jnp.stack([s[4] for s in new_p])
    sel_kv_s = jnp.stack([s[4] for s in new_s])
    win_kv_p = jnp.stack([s[5] for s in new_p])
    win_kv_s = jnp.stack([s[5] for s in new_s])
    return (xp, xs, conv_p, conv_s, lru_conv_p, lru_conv_s, lru_h_p, lru_h_s, cmp_kv_p, cmp_kv_s, sel_kv_p, sel_kv_s, win_kv_p, win_kv_s)
```

```python
import functools

import numpy as np
import jax
import jax.numpy as jnp
from jax import lax
from jax.experimental import pallas as pl
from jax.experimental.pallas import tpu as pltpu

F32 = jnp.float32
BF16 = jnp.bfloat16
I32 = jnp.int32

D_MODEL = 1024
DEPTH = 2
PAST_LEN = 16384
PAGE_SIZE = 128
D_CONV = 256
CONV_WIDTH = 31
D_LRU = 256
LRU_BLOCKS = 4
LRU_BW = D_LRU // LRU_BLOCKS
LRU_CONV_WIDTH = 4
LRU_C = 8.0
D_ATTN = 512
N_HEADS = 8
HEAD_DIM = 64
N_KV_HEADS = 2
GROUP = N_HEADS // N_KV_HEADS
KV_ROW = 2 * N_KV_HEADS * HEAD_DIM
ROPE_DIM = 16
ROPE_THETA = 500000.0
CMP_BLOCK = 32
CMP_STRIDE = 16
CMP_HIDDEN = 256
SEL_BLOCK = 64
N_SEL = 16
WINDOW = 512
Q_BLOCK = 128
D_FF = 2816
ALPHA = (2 * DEPTH) ** 0.25
LN_EPS = 1e-5
SCALE = HEAD_DIM ** -0.5

LANES = 128
VMEM_LIMIT = 56 * 1024 * 1024
FF_CHUNK = 256
N_FF_CHUNKS = D_FF // FF_CHUNK
PAD_HEAD = 128
Q_PAD = N_HEADS * PAD_HEAD
KV_GROUP_LANES = GROUP * PAD_HEAD
MASK_NEG = -(2.0 ** 60)
SEL_TILE = 512
WIN_KEYS = WINDOW + Q_BLOCK

C_GLU = 0
C_LRUX = 512
C_LRUG = 768
C_Q = 1024
C_KVC = C_Q + Q_PAD
C_KVS = C_KVC + KV_ROW
C_KVW = C_KVS + KV_ROW
C_GATE = C_KVW + KV_ROW
N_PROJ = C_GATE + N_KV_HEADS * LANES


def _cparams(sem):
    return pltpu.CompilerParams(dimension_semantics=sem, vmem_limit_bytes=VMEM_LIMIT)


def _const_spec(shape):
    nd = len(shape)
    return pl.BlockSpec(shape, lambda *_: (0,) * nd, pipeline_mode=pl.Buffered(1))


def _layernorm(y, g, b):
    mu = jnp.mean(y, axis=-1, keepdims=True)
    d = y - mu
    var = jnp.mean(d * d, axis=-1, keepdims=True)
    return d * lax.rsqrt(var + LN_EPS) * g + b


def _dot(a, b):
    return jnp.dot(a, b, preferred_element_type=F32)


def _dot_nt(a, b):
    return lax.dot_general(a, b, (((1,), (1,)), ((), ())), preferred_element_type=F32)


def _dot_tn(a, b):
    return lax.dot_general(a, b, (((0,), (0,)), ((), ())), preferred_element_type=F32)


def _ffn_ln_kernel(x_ref, wg_ref, wu_ref, wo_ref, g_ref, b_ref, o_ref):
    x = x_ref[...]
    xb = x.astype(BF16)
    acc = jnp.zeros(x.shape, F32)
    for c in range(N_FF_CHUNKS):
        gate = _dot(xb, wg_ref[c])
        up = _dot(xb, wu_ref[c])
        h = (gate * jax.nn.sigmoid(gate)) * up
        acc = acc + _dot(h.astype(BF16), wo_ref[c])
    y = ALPHA * x + 0.5 * acc
    o_ref[...] = _layernorm(y, g_ref[...], b_ref[...])


def _ffn_ln(x, fw, tm):
    n = x.shape[0]
    return pl.pallas_call(
        _ffn_ln_kernel,
        grid=(n // tm,),
        in_specs=[
            pl.BlockSpec((tm, D_MODEL), lambda i: (i, 0)),
            _const_spec((N_FF_CHUNKS, D_MODEL, FF_CHUNK)),
            _const_spec((N_FF_CHUNKS, D_MODEL, FF_CHUNK)),
            _const_spec((N_FF_CHUNKS, FF_CHUNK, D_MODEL)),
            _const_spec((1, D_MODEL)),
            _const_spec((1, D_MODEL)),
        ],
        out_specs=pl.BlockSpec((tm, D_MODEL), lambda i: (i, 0)),
        out_shape=jax.ShapeDtypeStruct((n, D_MODEL), F32),
        compiler_params=_cparams(("parallel",)),
        name="ffn_ln",
    )(x, fw["wg"], fw["wu"], fw["wo"], fw["g"], fw["b"])


def _rope(v, cos, s1, s2):
    return v * cos + pltpu.roll(v, 8, 1) * s1 + pltpu.roll(v, LANES - 8, 1) * s2


def _inproj_kernel(x_ref, w_ref, cos_ref, s1_ref, s2_ref,
                   u_ref, lx_ref, lg_ref, qc_ref, qr_ref,
                   kvc_ref, kvs_ref, kvsb_ref, kvw_ref, kvwb_ref, gt_ref):
    xb = x_ref[...].astype(BF16)

    def mm(lo, hi):
        return _dot(xb, w_ref[:, lo:hi])

    glu = mm(C_GLU, C_GLU + 2 * D_CONV)
    u_ref[...] = glu[:, :D_CONV] * jax.nn.sigmoid(glu[:, D_CONV:])
    lx_ref[...] = mm(C_LRUX, C_LRUX + D_LRU)
    lg_ref[...] = mm(C_LRUG, C_LRUG + D_LRU)
    cos = cos_ref[...]
    s1 = s1_ref[...]
    s2 = s2_ref[...]
    for h in range(N_HEADS):
        qh = mm(C_Q + h * PAD_HEAD, C_Q + (h + 1) * PAD_HEAD) * SCALE
        qc_ref[:, h * PAD_HEAD:(h + 1) * PAD_HEAD] = qh.astype(BF16)
        qr_ref[:, h * PAD_HEAD:(h + 1) * PAD_HEAD] = _rope(qh, cos, s1, s2).astype(BF16)
    kvc_ref[...] = mm(C_KVC, C_KVC + KV_ROW)
    for c0, f_ref, b_ref in ((C_KVS, kvs_ref, kvsb_ref), (C_KVW, kvw_ref, kvwb_ref)):
        kv = mm(c0, c0 + KV_ROW)
        k = _rope(kv[:, :LANES], cos, s1, s2)
        v = kv[:, LANES:]
        f_ref[:, 0:LANES] = k
        f_ref[:, LANES:KV_ROW] = v
        b_ref[:, 0:LANES] = k.astype(BF16)
        b_ref[:, LANES:KV_ROW] = v.astype(BF16)
    gt_ref[...] = jax.nn.sigmoid(mm(C_GATE, N_PROJ))


def _inproj(x, w_all, rope_tabs, tm, tiles_per_seq):
    n = x.shape[0]
    cos, s1, s2 = rope_tabs
    tok = lambda w: pl.BlockSpec((tm, w), lambda i: (i, 0))
    tab = pl.BlockSpec((tm, LANES), lambda i: (i % tiles_per_seq, 0))
    outs = [
        (D_CONV, F32), (D_LRU, F32), (D_LRU, F32), (Q_PAD, BF16), (Q_PAD, BF16),
        (KV_ROW, F32), (KV_ROW, F32), (KV_ROW, BF16), (KV_ROW, F32), (KV_ROW, BF16),
        (N_KV_HEADS * LANES, F32),
    ]
    return pl.pallas_call(
        _inproj_kernel,
        grid=(n // tm,),
        in_specs=[tok(D_MODEL), _const_spec((D_MODEL, N_PROJ)), tab, tab, tab],
        out_specs=[tok(w) for w, _ in outs],
        out_shape=[jax.ShapeDtypeStruct((n, w), dt) for w, dt in outs],
        compiler_params=_cparams(("parallel",)),
        name="inproj",
    )(x, w_all, cos, s1, s2)


CONV_HALO = 32
CONV_ROWS = 64


def _conv_kernel(u_ref, buf_ref, w_ref, b_ref, g_ref, bb_ref, o_ref, ext_ref):
    t = pl.program_id(1)
    tc = u_ref.shape[0]

    @pl.when(t == 0)
    def _():
        ext_ref[0:CONV_HALO, :] = buf_ref[...]

    ext_ref[CONV_HALO:CONV_HALO + tc, :] = u_ref[...]
    off = CONV_HALO - (CONV_WIDTH - 1)
    rows = min(CONV_ROWS, tc)
    for r0 in range(0, tc, rows):
        acc = jnp.zeros((rows, D_CONV), F32)
        for k in range(CONV_WIDTH):
            acc = acc + ext_ref[r0 + off + k:r0 + off + k + rows, :] * w_ref[k:k + 1, :]
        y = acc + b_ref[...]
        y = _layernorm(y, g_ref[...], bb_ref[...])
        o_ref[r0:r0 + rows, :] = (y * jax.nn.sigmoid(y)).astype(BF16)
    ext_ref[0:CONV_HALO, :] = ext_ref[tc:tc + CONV_HALO, :]


def _conv_group(u, buf, cw, tc):
    b, t, _ = u.shape
    return pl.pallas_call(
        _conv_kernel,
        grid=(b, t // tc),
        in_specs=[
            pl.BlockSpec((None, tc, D_CONV), lambda i, j: (i, j, 0)),
            pl.BlockSpec((None, CONV_HALO, D_CONV), lambda i, j: (i, 0, 0)),
            _const_spec((CONV_HALO, D_CONV)),
            _const_spec((1, D_CONV)), _const_spec((1, D_CONV)), _const_spec((1, D_CONV)),
        ],
        out_specs=pl.BlockSpec((None, tc, D_CONV), lambda i, j: (i, j, 0)),
        out_shape=jax.ShapeDtypeStruct((b, t, D_CONV), BF16),
        scratch_shapes=[pltpu.VMEM((tc + CONV_HALO, D_CONV), F32)],
        compiler_params=_cparams(("parallel", "arbitrary")),
        name="conv_group",
    )(u, buf, cw["w"], cw["b"], cw["ln_g"], cw["ln_b"])


LRU_HALO = 8


def _lru_kernel(last_row, x_ref, gate_ref, buf_ref, h0_ref, cw_ref, cb_ref, wr_ref, wi_ref,
                bg_ref, lam_ref, y_ref, hl_ref, ext_ref, hc_ref):
    t = pl.program_id(1)
    tl = x_ref.shape[0]

    @pl.when(t == 0)
    def _():
        ext_ref[0:LRU_HALO, :] = buf_ref[...]
        hc_ref[...] = h0_ref[...]

    ext_ref[LRU_HALO:LRU_HALO + tl, :] = x_ref[...]
    off = LRU_HALO - (LRU_CONV_WIDTH - 1)
    xl = jnp.zeros((tl, D_LRU), F32)
    for k in range(LRU_CONV_WIDTH):
        xl = xl + ext_ref[off + k:off + k + tl, :] * cw_ref[k:k + 1, :]
    xl = xl + cb_ref[...]
    ext_ref[0:LRU_HALO, :] = ext_ref[tl:tl + LRU_HALO, :]

    xb = xl.astype(BF16)
    r_gate = jax.nn.sigmoid(_dot(xb, wr_ref[...]) + bg_ref[0:1, :])
    i_gate = jax.nn.sigmoid(_dot(xb, wi_ref[...]) + bg_ref[1:2, :])
    log_a = LRU_C * r_gate * jax.nn.log_sigmoid(lam_ref[...])
    a = jnp.exp(log_a)
    bv = jnp.sqrt(-jnp.tanh(log_a) * (a * a + 1.0)) * (i_gate * xl)

    row = lax.broadcasted_iota(I32, (tl, D_LRU), 0)
    s = 1
    while s < tl:
        keep = row >= s
        a_sh = jnp.where(keep, pltpu.roll(a, s, 0), 1.0)
        b_sh = jnp.where(keep, pltpu.roll(bv, s, 0), 0.0)
        bv = a * b_sh + bv
        a = a * a_sh
        s *= 2
    h = a * hc_ref[0:1, :] + bv
    hc_ref[...] = jnp.broadcast_to(h[last_row:last_row + 1, :], hc_ref.shape)
    hl_ref[...] = hc_ref[...]
    y_ref[...] = (h * jax.nn.gelu(gate_ref[...])).astype(BF16)


def _lru_group(x, gate, buf, h0, lw, tl, last_row):
    b, t, _ = x.shape
    assert last_row == tl - 1 or t == tl
    seq = pl.BlockSpec((None, tl, D_LRU), lambda i, j: (i, j, 0))
    per_b = pl.BlockSpec((None, LRU_HALO, D_LRU), lambda i, j: (i, 0, 0))
    return pl.pallas_call(
        functools.partial(_lru_kernel, last_row),
        grid=(b, t // tl),
        in_specs=[seq, seq, per_b, per_b,
                  _const_spec((LRU_HALO, D_LRU)), _const_spec((1, D_LRU)),
                  _const_spec((D_LRU, D_LRU)), _const_spec((D_LRU, D_LRU)),
                  _const_spec((2, D_LRU)), _const_spec((1, D_LRU))],
        out_specs=[seq, per_b],
        out_shape=[jax.ShapeDtypeStruct((b, t, D_LRU), BF16),
                   jax.ShapeDtypeStruct((b, LRU_HALO, D_LRU), F32)],
        scratch_shapes=[pltpu.VMEM((tl + LRU_HALO, D_LRU), F32),
                        pltpu.VMEM((LRU_HALO, D_LRU), F32)],
        compiler_params=_cparams(("parallel", "arbitrary")),
        name="lru_group",
    )(x, gate, buf, h0, lw["cw"], lw["cb"], lw["wr"], lw["wi"], lw["bg"], lw["lam"])


def _compress_rows(row_refs, pe_ref, w1_ref, b1_ref, w2_ref, b2_ref, carry_ref):
    lo = lax.broadcasted_iota(I32, (1, LANES), 1) < HEAD_DIM
    xs = [[[] for _ in range(4)] for _ in range(2)]
    for halves in row_refs:
        n = halves[0].shape[0] // CMP_STRIDE
        cols = [[[] for _ in range(4)] for _ in range(2)]
        for rp in range(CMP_STRIDE // 2):
            for half, ref in enumerate(halves):
                pa = ref[pl.ds(2 * rp, n, stride=CMP_STRIDE), :]
                pb = ref[pl.ds(2 * rp + 1, n, stride=CMP_STRIDE), :]
                for j in range(2):
                    va = pa + pe_ref[j, 2 * rp:2 * rp + 1, half * LANES:(half + 1) * LANES]
                    vb = pb + pe_ref[j, 2 * rp + 1:2 * rp + 2, half * LANES:(half + 1) * LANES]
                    cols[j][2 * half].append(jnp.where(lo, va, pltpu.roll(vb, HEAD_DIM, 1)))
                    cols[j][2 * half + 1].append(jnp.where(lo, pltpu.roll(va, HEAD_DIM, 1), vb))
        for j in range(2):
            for g in range(4):
                xs[j][g].append(jnp.concatenate(cols[j][g], axis=1))
    out = None
    for g in range(4):
        sidx = g // 2
        x0 = jnp.concatenate(xs[0][g], axis=0).astype(BF16)
        x1 = jnp.concatenate(xs[1][g], axis=0).astype(BF16)
        p0 = _dot(x0, w1_ref[sidx, 0])
        p1 = _dot(x1, w1_ref[sidx, 1])
        n_tot = p0.shape[0]
        row = lax.broadcasted_iota(I32, p0.shape, 0)
        p0s = jnp.where(row == 0, carry_ref[g, 0:1, :], pltpu.roll(p0, 1, 0))
        carry_ref[g, 0:1, :] = p0[n_tot - 1:n_tot, :]
        h = (b1_ref[sidx:sidx + 1, :] + p0s) + p1
        part = _dot(jax.nn.gelu(h).astype(BF16), w2_ref[g])
        out = part if out is None else out + part
    return out + b2_ref[...]


def _cmp_weight_specs():
    return [
        _const_spec((2, CMP_STRIDE, KV_ROW)),
        _const_spec((2, 2, CMP_STRIDE * HEAD_DIM, CMP_HIDDEN)),
        _const_spec((2, CMP_HIDDEN)),
        _const_spec((4, CMP_HIDDEN, KV_ROW)),
        _const_spec((1, KV_ROW)),
    ]


def _cmp_weight_args(cw):
    return (cw["pe"], cw["w1"], cw["b1"], cw["w2"], cw["b2"])


CMP_TILE_ROWS = 1024


def _compress_prompt_kernel(k_ref, v_ref, pe_ref, w1_ref, b1_ref, w2_ref, b2_ref, o_ref, carry_ref):
    @pl.when(pl.program_id(1) == 0)
    def _():
        carry_ref[...] = jnp.zeros(carry_ref.shape, F32)

    out = _compress_rows([(k_ref, v_ref)], pe_ref, w1_ref, b1_ref, w2_ref, b2_ref, carry_ref)
    o_ref[...] = out.astype(BF16)


def _compress_prompt(kvc, cw):
    b, t, _ = kvc.shape
    n_e = CMP_TILE_ROWS // CMP_STRIDE
    return pl.pallas_call(
        _compress_prompt_kernel,
        grid=(b, t // CMP_TILE_ROWS),
        in_specs=[pl.BlockSpec((None, CMP_TILE_ROWS, LANES), lambda i, j: (i, j, 0)),
                  pl.BlockSpec((None, CMP_TILE_ROWS, LANES), lambda i, j: (i, j, 1))]
        + _cmp_weight_specs(),
        out_specs=pl.BlockSpec((None, n_e, KV_ROW), lambda i, j: (i, j, 0)),
        out_shape=jax.ShapeDtypeStruct((b, t // CMP_STRIDE, KV_ROW), BF16),
        scratch_shapes=[pltpu.VMEM((4, 8, CMP_HIDDEN), F32)],
        compiler_params=_cparams(("parallel", "arbitrary")),
        name="compress_prompt",
    )(kvc, kvc, *_cmp_weight_args(cw))


def _selection_scores(pk_ref, n_sel):
    ratio = SEL_BLOCK // CMP_STRIDE
    slc = pk_ref[pl.ds(0, n_sel, stride=ratio), :]
    for o in range(1, ratio):
        slc = slc + 2.0 * pk_ref[pl.ds(o, n_sel, stride=ratio), :]
    return slc + pk_ref[pl.ds(ratio, n_sel, stride=ratio), :]


def _topk_rounds(score, n_rows):
    j = lax.broadcasted_iota(I32, score.shape, 0).astype(F32)
    sel = jnp.zeros(score.shape, F32)
    picks = []
    for _ in range(N_SEL):
        cm = jnp.max(score, axis=0, keepdims=True)
        mi = jnp.min(jnp.where(score == cm, j, float(n_rows)), axis=0, keepdims=True)
        hit = j == mi
        sel = jnp.where(hit, 1.0, sel)
        score = jnp.where(hit, -jnp.inf, score)
        picks.append(mi)
    return sel, picks


def _masked_softmax_rows(s_t, valid):
    s_t = jnp.where(valid, s_t, -jnp.inf)
    m = jnp.max(s_t, axis=0, keepdims=True)
    m = jnp.where(m > -jnp.inf, m, 0.0)
    e = jnp.exp(s_t - m)
    d = jnp.sum(e, axis=0, keepdims=True)
    return e / jnp.where(d > 0, d, 1.0)


def _cmp_topk_prompt_kernel(qc_ref, kcv_ref, ocmp_ref, mneg_ref, pk_ref):
    i = pl.program_id(1)
    n_e = kcv_ref.shape[0]
    n_sel = n_e // (SEL_BLOCK // CMP_STRIDE)
    kc = kcv_ref[:, 0:LANES]
    vc = kcv_ref[:, LANES:KV_ROW]
    e_idx = lax.broadcasted_iota(I32, (n_e, Q_BLOCK), 0)
    qpos = i * Q_BLOCK + lax.broadcasted_iota(I32, (n_e, Q_BLOCK), 1)
    valid = (e_idx >= 1) & (CMP_STRIDE * e_idx + (CMP_STRIDE - 1) <= qpos)
    pkv = jnp.zeros((n_e, Q_BLOCK), F32)
    for h in range(GROUP):
        qh = qc_ref[:, h * PAD_HEAD:(h + 1) * PAD_HEAD]
        p = _masked_softmax_rows(_dot_nt(kc, qh), valid)
        pkv = pkv + p
        ocmp_ref[:, h * PAD_HEAD:(h + 1) * PAD_HEAD] = _dot_tn(p.astype(BF16), vc)
    pk_ref[0:n_e, :] = pkv
    pk_ref[n_e:n_e + 8, :] = jnp.zeros((8, Q_BLOCK), F32)
    slc = _selection_scores(pk_ref, n_sel)

    j = lax.broadcasted_iota(I32, (n_sel, Q_BLOCK), 0)
    qp = i * Q_BLOCK + lax.broadcasted_iota(I32, (n_sel, Q_BLOCK), 1)
    qblk = jnp.right_shift(qp, 6)
    forced = (j == 0) | (j == qblk) | (j == qblk - 1)
    score = jnp.where(forced, jnp.inf, jnp.where(j * SEL_BLOCK <= qp, slc, -jnp.inf))
    sel, _ = _topk_rounds(score, n_sel)
    mneg_t = jnp.where(sel > 0, 0.0, MASK_NEG)
    mneg_ref[...] = mneg_t.T.astype(BF16)


def _cmp_topk_prompt(qc, kcv):
    b, t, _ = qc.shape
    n_e = kcv.shape[1]
    n_sel = t // SEL_BLOCK
    return pl.pallas_call(
        _cmp_topk_prompt_kernel,
        grid=(b, t // Q_BLOCK, N_KV_HEADS),
        in_specs=[
            pl.BlockSpec((None, Q_BLOCK, KV_GROUP_LANES), lambda bi, i, k: (bi, i, k)),
            pl.BlockSpec((None, n_e, KV_ROW), lambda bi, i, k: (bi, 0, 0)),
        ],
        out_specs=[
            pl.BlockSpec((None, Q_BLOCK, KV_GROUP_LANES), lambda bi, i, k: (bi, i, k)),
            pl.BlockSpec((None, None, Q_BLOCK, n_sel), lambda bi, i, k: (bi, k, i, 0)),
        ],
        out_shape=[jax.ShapeDtypeStruct((b, t, Q_PAD), F32),
                   jax.ShapeDtypeStruct((b, N_KV_HEADS, t, n_sel), BF16)],
        scratch_shapes=[pltpu.VMEM((n_e + 8, Q_BLOCK), F32)],
        compiler_params=_cparams(("parallel", "parallel", "arbitrary")),
        name="cmp_topk_prompt",
    )(qc, kcv)


def _selwin_prompt_kernel(qr_ref, mneg_ref, kvs_ref, kvw_ref, ocmp_ref, gt_ref, o_ref, qa_ref):
    i = pl.program_id(1)
    t_len = kvs_ref.shape[0]
    n_sel = mneg_ref.shape[1]
    rows = GROUP * Q_BLOCK
    for h in range(GROUP):
        qa_ref[h * Q_BLOCK:(h + 1) * Q_BLOCK, 0:PAD_HEAD] = qr_ref[:, h * PAD_HEAD:(h + 1) * PAD_HEAD]
        qa_ref[h * Q_BLOCK:(h + 1) * Q_BLOCK, PAD_HEAD:PAD_HEAD + n_sel] = mneg_ref[...]
    qa = qa_ref[...]
    q0 = i * Q_BLOCK
    qpos = q0 + (lax.broadcasted_iota(I32, (rows, 1), 0) & (Q_BLOCK - 1))

    tile = min(SEL_TILE, t_len)
    n_tiles = (q0 + Q_BLOCK - 1) // tile + 1
    key_row = lax.broadcasted_iota(I32, (tile, n_sel), 0)
    blk_lane = lax.broadcasted_iota(I32, (tile, n_sel), 1)
    key_lane = lax.broadcasted_iota(I32, (1, tile), 1)

    def body(t, carry):
        m, l, acc = carry
        k0 = pl.multiple_of(t * tile, tile)
        kt = kvs_ref[pl.ds(k0, tile), 0:LANES]
        vt = kvs_ref[pl.ds(k0, tile), LANES:KV_ROW]
        onehot = jnp.where(jnp.right_shift(k0 + key_row, 6) == blk_lane, 1.0, 0.0).astype(BF16)
        ka = jnp.concatenate([kt, onehot], axis=1)
        s = _dot_nt(qa, ka)
        s = jnp.where(k0 + key_lane <= qpos, s, MASK_NEG)
        m_new = jnp.maximum(m, jnp.max(s, axis=-1, keepdims=True))
        alpha = jnp.exp(m - m_new)
        p = jnp.exp(s - m_new)
        l = alpha * l + jnp.sum(p, axis=-1, keepdims=True)
        acc = alpha * acc + _dot(p.astype(BF16), vt)
        return m_new, l, acc

    m0 = jnp.full((rows, 1), MASK_NEG, F32)
    l0 = jnp.zeros((rows, 1), F32)
    a0 = jnp.zeros((rows, LANES), F32)
    _, l_sel, acc_sel = lax.fori_loop(0, n_tiles, body, (m0, l0, a0))
    o_sel = acc_sel / l_sel

    wk = min(WIN_KEYS, t_len)
    start = pl.multiple_of(jnp.maximum(q0 + Q_BLOCK - wk, 0), Q_BLOCK)
    kw = kvw_ref[pl.ds(start, wk), 0:LANES]
    vw = kvw_ref[pl.ds(start, wk), LANES:KV_ROW]
    s = _dot_nt(qa[:, 0:PAD_HEAD], kw)
    dpos = qpos - (start + lax.broadcasted_iota(I32, (1, wk), 1))
    s = jnp.where((dpos >= 0) & (dpos <= WINDOW), s, -jnp.inf)
    m = jnp.max(s, axis=-1, keepdims=True)
    p = jnp.exp(s - m)
    o_win = _dot(p.astype(BF16), vw) / jnp.sum(p, axis=-1, keepdims=True)

    for h in range(GROUP):
        r = slice(h * Q_BLOCK, (h + 1) * Q_BLOCK)
        g_cmp = gt_ref[:, 3 * h:3 * h + 1]
        g_sel = gt_ref[:, 3 * h + 1:3 * h + 2]
        g_win = gt_ref[:, 3 * h + 2:3 * h + 3]
        o = g_cmp * ocmp_ref[:, h * PAD_HEAD:(h + 1) * PAD_HEAD] + g_sel * o_sel[r] + g_win * o_win[r]
        o_ref[:, h * PAD_HEAD:(h + 1) * PAD_HEAD] = o.astype(BF16)


def _selwin_prompt(qr, mneg, kvs_b, kvw_b, ocmp, gates):
    b, t, _ = qr.shape
    n_sel = mneg.shape[-1]
    qblk = pl.BlockSpec((None, Q_BLOCK, KV_GROUP_LANES), lambda bi, i, k: (bi, i, k))
    per_b = pl.BlockSpec((None, t, KV_ROW), lambda bi, i, k: (bi, 0, 0))
    return pl.pallas_call(
        _selwin_prompt_kernel,
        grid=(b, t // Q_BLOCK, N_KV_HEADS),
        in_specs=[
            qblk,
            pl.BlockSpec((None, None, Q_BLOCK, n_sel), lambda bi, i, k: (bi, k, i, 0)),
            per_b, per_b, qblk,
            pl.BlockSpec((None, Q_BLOCK, LANES), lambda bi, i, k: (bi, i, k)),
        ],
        out_specs=qblk,
        out_shape=jax.ShapeDtypeStruct((b, t, Q_PAD), BF16),
        scratch_shapes=[pltpu.VMEM((GROUP * Q_BLOCK, PAD_HEAD + n_sel), BF16)],
        compiler_params=_cparams(("parallel", "parallel", "arbitrary")),
        name="selwin_prompt",
    )(qr, mneg, kvs_b, kvw_b, ocmp, gates)


PAGES_PER_STEP = 8
S_COLS = LANES


def _cmp_topk_sample_kernel(n_t, pt_ref, *refs):
    page_refs = [(refs[2 * k], refs[2 * k + 1]) for k in range(PAGES_PER_STEP)]
    (q_ref, pe_ref, w1_ref, b1_ref, w2_ref, b2_ref,
     ocmp_ref, idx_ref, kcv_ref, pk_ref, carry_ref) = refs[2 * PAGES_PER_STEP:]
    s = pl.program_id(1)
    n_steps = pl.num_programs(1)
    n_e = kcv_ref.shape[0]
    step_e = PAGES_PER_STEP * PAGE_SIZE // CMP_STRIDE

    @pl.when(s == 0)
    def _():
        carry_ref[...] = jnp.zeros(carry_ref.shape, F32)

    out = _compress_rows(page_refs, pe_ref, w1_ref, b1_ref, w2_ref, b2_ref, carry_ref)
    kcv_ref[pl.ds(pl.multiple_of(s * step_e, step_e), step_e), :] = out.astype(BF16)

    @pl.when(s == n_steps - 1)
    def _():
        n_sel = n_e // (SEL_BLOCK // CMP_STRIDE) + 1
        n_sel_rows = pk_ref.shape[0] // (SEL_BLOCK // CMP_STRIDE) - 2
        kc = kcv_ref[:, 0:LANES]
        vc = kcv_ref[:, LANES:KV_ROW]
        e_idx = lax.broadcasted_iota(I32, (n_e, S_COLS), 0)
        p = _masked_softmax_rows(_dot_nt(kc, q_ref[...]), e_idx >= 1)
        ocmp_ref[...] = _dot_tn(p.astype(BF16), vc)
        pkv = p
        for hh in range(1, GROUP):
            pkv = pkv + pltpu.roll(p, S_COLS - n_t * hh, 1)
        pk_ref[0:n_e, :] = pkv
        pk_ref[n_e:, :] = jnp.zeros((pk_ref.shape[0] - n_e, S_COLS), F32)
        slc = _selection_scores(pk_ref, n_sel_rows)
        j = lax.broadcasted_iota(I32, (n_sel_rows, S_COLS), 0)
        qp = PAST_LEN + (lax.broadcasted_iota(I32, (n_sel_rows, S_COLS), 1) & (n_t - 1))
        qblk = jnp.right_shift(qp, 6)
        forced = (j == 0) | (j == qblk) | (j == qblk - 1)
        in_range = j < n_sel
        score = jnp.where(forced & in_range, jnp.inf,
                          jnp.where((j * SEL_BLOCK <= qp) & in_range, slc, -jnp.inf))
        _, picks = _topk_rounds(score, n_sel_rows)
        for r, mi in enumerate(picks):
            idx_ref[r:r + 1, :] = mi.astype(I32)


def _cmp_topk_sample(page_table, cache_cmp, q_cols, cw, n_t):
    assert n_t & (n_t - 1) == 0 and N_HEADS * n_t <= S_COLS
    b, n_pages = page_table.shape
    n_e = n_pages * PAGE_SIZE // CMP_STRIDE
    n_sel_rows = ((n_e // 4 + 1) + 7) // 8 * 8
    pk_rows = 4 * (n_sel_rows + 2)
    n_steps = n_pages // PAGES_PER_STEP

    def page_spec(k, half):
        return pl.BlockSpec((None, PAGE_SIZE, LANES),
                            lambda bi, s, pt: (pt[bi, s * PAGES_PER_STEP + k], 0, half))

    per_b = lambda rows, w: pl.BlockSpec((None, rows, w), lambda bi, s, pt: (bi, 0, 0))
    grid_spec = pltpu.PrefetchScalarGridSpec(
        num_scalar_prefetch=1,
        grid=(b, n_steps),
        in_specs=[page_spec(k, half) for k in range(PAGES_PER_STEP) for half in range(2)]
        + [per_b(S_COLS, LANES)] + _cmp_weight_specs(),
        out_specs=[per_b(S_COLS, LANES), per_b(N_SEL, S_COLS)],
        scratch_shapes=[pltpu.VMEM((n_e, KV_ROW), BF16),
                        pltpu.VMEM((pk_rows, S_COLS), F32),
                        pltpu.VMEM((4, 8, CMP_HIDDEN), F32)],
    )
    return pl.pallas_call(
        functools.partial(_cmp_topk_sample_kernel, n_t),
        grid_spec=grid_spec,
        out_shape=[jax.ShapeDtypeStruct((b, S_COLS, LANES), F32),
                   jax.ShapeDtypeStruct((b, N_SEL, S_COLS), I32)],
        compiler_params=_cparams(("parallel", "arbitrary")),
        name="cmp_topk_sample",
    )(page_table, *([cache_cmp] * (2 * PAGES_PER_STEP)), q_cols, *_cmp_weight_args(cw))


HROWS = 8


def _selwin_sample_kernel(idx_ref, pt_ref, *refs):
    blk_refs = refs[:N_SEL]
    (q_ref, new_s_ref, win_ref, new_w_ref, ocmp_ref, gt_ref, o_ref, k_ref, v_ref) = refs[N_SEL:]
    bi = pl.program_id(0)
    tq = pl.program_id(1)
    kh = pl.program_id(2)
    n_t = pl.num_programs(1)
    qpos = PAST_LEN + tq
    q = q_ref[...]
    base = ((bi * n_t + tq) * N_KV_HEADS + kh) * N_SEL
    new_block = PAST_LEN // SEL_BLOCK

    kpos_parts = []
    lane = lax.broadcasted_iota(I32, (1, SEL_BLOCK), 1)
    for r in range(N_SEL):
        j = idx_ref[base + r]
        is_new = j == new_block
        rows = slice(r * SEL_BLOCK, (r + 1) * SEL_BLOCK)

        @pl.when(is_new)
        def _():
            k_ref[rows, :] = jnp.zeros((SEL_BLOCK, LANES), BF16)
            v_ref[rows, :] = jnp.zeros((SEL_BLOCK, LANES), BF16)
            k_ref[r * SEL_BLOCK:r * SEL_BLOCK + 16, :] = new_s_ref[:, 0:LANES]
            v_ref[r * SEL_BLOCK:r * SEL_BLOCK + 16, :] = new_s_ref[:, LANES:KV_ROW]

        @pl.when(jnp.logical_not(is_new))
        def _():
            k_ref[rows, :] = blk_refs[r][:, 0:LANES].astype(BF16)
            v_ref[rows, :] = blk_refs[r][:, LANES:KV_ROW].astype(BF16)

        kpos_parts.append(j * SEL_BLOCK + lane)
    kpos = jnp.concatenate(kpos_parts, axis=1)
    s = _dot_nt(q, k_ref[...])
    s = jnp.where(kpos <= qpos, s, -jnp.inf)
    m = jnp.max(s, axis=-1, keepdims=True)
    p = jnp.exp(s - m)
    o_sel = _dot(p.astype(BF16), v_ref[...]) / jnp.sum(p, axis=-1, keepdims=True)

    wb = win_ref.shape[0]
    s_old = _dot_nt(q, win_ref[:, 0:LANES].astype(BF16))
    d_old = qpos - (PAST_LEN - wb + lax.broadcasted_iota(I32, (1, wb), 1))
    s_old = jnp.where((d_old >= 0) & (d_old <= WINDOW), s_old, -jnp.inf)
    s_new = _dot_nt(q, new_w_ref[:, 0:LANES])
    d_new = tq - lax.broadcasted_iota(I32, (1, new_w_ref.shape[0]), 1)
    s_new = jnp.where((d_new >= 0) & (d_new <= WINDOW), s_new, -jnp.inf)
    m = jnp.maximum(jnp.max(s_old, axis=-1, keepdims=True), jnp.max(s_new, axis=-1, keepdims=True))
    p_old = jnp.exp(s_old - m)
    p_new = jnp.exp(s_new - m)
    den = jnp.sum(p_old, axis=-1, keepdims=True) + jnp.sum(p_new, axis=-1, keepdims=True)
    o_win = (_dot(p_old.astype(BF16), win_ref[:, LANES:KV_ROW].astype(BF16))
             + _dot(p_new.astype(BF16), new_w_ref[:, LANES:KV_ROW])) / den

    g = gt_ref[...]
    o = g[:, 0:1] * ocmp_ref[...] + g[:, 1:2] * o_sel + g[:, 2:3] * o_win
    o_ref[...] = o.astype(BF16)


def _selwin_sample(idx_flat, page_table, cache_sel, q_rows, new_s, win, new_w, ocmp_rows, gate_rows):
    b, n_t = q_rows.shape[:2]
    wb = win.shape[1]
    n_t_static = n_t

    def blk_spec(r):
        def imap(bi, tq, kh, idx, pt):
            j = idx[((bi * n_t_static + tq) * N_KV_HEADS + kh) * N_SEL + r]
            jc = jnp.minimum(j, PAST_LEN // SEL_BLOCK - 1)
            return (pt[bi, jc // 2], jc % 2, 0)
        return pl.BlockSpec((None, SEL_BLOCK, KV_ROW), imap)

    row5 = lambda dt: pl.BlockSpec((None, None, None, HROWS, LANES),
                                   lambda bi, tq, kh, idx, pt: (bi, tq, kh, 0, 0))
    per_b = lambda rows: pl.BlockSpec((None, rows, KV_ROW), lambda bi, tq, kh, idx, pt: (bi, 0, 0))
    grid_spec = pltpu.PrefetchScalarGridSpec(
        num_scalar_prefetch=2,
        grid=(b, n_t, N_KV_HEADS),
        in_specs=[blk_spec(r) for r in range(N_SEL)]
        + [row5(BF16), per_b(16), per_b(wb), per_b(16), row5(F32), row5(F32)],
        out_specs=row5(BF16),
        scratch_shapes=[pltpu.VMEM((N_SEL * SEL_BLOCK, LANES), BF16),
                        pltpu.VMEM((N_SEL * SEL_BLOCK, LANES), BF16)],
    )
    return pl.pallas_call(
        _selwin_sample_kernel,
        grid_spec=grid_spec,
        out_shape=jax.ShapeDtypeStruct((b, n_t, N_KV_HEADS, HROWS, LANES), BF16),
        compiler_params=_cparams(("arbitrary", "arbitrary", "arbitrary")),
        name="selwin_sample",
    )(idx_flat, page_table, *([cache_sel] * N_SEL), q_rows, new_s, win, new_w, ocmp_rows, gate_rows)


def _outproj_ln_kernel(x_ref, yc_ref, yl_ref, ya_ref, w_ref, g_ref, b_ref, o_ref):
    y = _dot(yc_ref[...], w_ref[0:D_CONV, :])
    y = y + _dot(yl_ref[...], w_ref[D_CONV:D_CONV + D_LRU, :])
    y = y + _dot(ya_ref[...], w_ref[D_CONV + D_LRU:, :])
    o_ref[...] = _layernorm(ALPHA * x_ref[...] + y, g_ref[...], b_ref[...])


def _outproj_ln(x, yc, yl, ya, ow, tm):
    n = x.shape[0]
    tok = lambda w: pl.BlockSpec((tm, w), lambda i: (i, 0))
    return pl.pallas_call(
        _outproj_ln_kernel,
        grid=(n // tm,),
        in_specs=[tok(D_MODEL), tok(D_CONV), tok(D_LRU), tok(Q_PAD),
                  _const_spec((D_CONV + D_LRU + Q_PAD, D_MODEL)),
                  _const_spec((1, D_MODEL)), _const_spec((1, D_MODEL))],
        out_specs=tok(D_MODEL),
        out_shape=jax.ShapeDtypeStruct((n, D_MODEL), F32),
        compiler_params=_cparams(("parallel",)),
        name="outproj_ln",
    )(x, yc, yl, ya, ow["w"], ow["g"], ow["b"])


def _rope_tables(pos):
    half = ROPE_DIM // 2
    inv = ROPE_THETA ** (-jnp.arange(half, dtype=F32) / half)
    ang = pos.astype(F32)[:, None] * inv[None, :]
    cos, sin = jnp.cos(ang), jnp.sin(ang)
    n = pos.shape[0]
    rest = HEAD_DIM - ROPE_DIM
    zeros8 = jnp.zeros((n, half), F32)
    c = jnp.concatenate([cos, cos, jnp.ones((n, rest), F32)], axis=1)
    s1 = jnp.concatenate([zeros8, sin, jnp.zeros((n, rest), F32)], axis=1)
    s2 = jnp.concatenate([-sin, zeros8, jnp.zeros((n, rest), F32)], axis=1)
    rep = LANES // HEAD_DIM
    return tuple(jnp.tile(a, (1, rep)) for a in (c, s1, s2))


def _head_pad_index():
    h = np.arange(D_ATTN) // HEAD_DIM
    d = np.arange(D_ATTN) % HEAD_DIM
    return h * PAD_HEAD + (h // GROUP) * HEAD_DIM + d


def _prep_layer(l, ln_g, ln_b, ffn_w_in, ffn_w_out, w_in, conv_w, conv_b, conv_ln_g, conv_ln_b,
                lru_conv_w, lru_conv_b, lru_w_gate, lru_b_gate, lru_lambda,
                cmp_pe, cmp_w1, cmp_b1, cmp_w2, cmp_b2, w_out):
    row = lambda v: v.reshape(1, -1).astype(F32)
    ffn = []
    for f, ln_i in ((0, 0), (1, 2)):
        wi = ffn_w_in[l, f]
        wg = wi[:, :D_FF].reshape(D_MODEL, N_FF_CHUNKS, FF_CHUNK).transpose(1, 0, 2).astype(BF16)
        wu = wi[:, D_FF:].reshape(D_MODEL, N_FF_CHUNKS, FF_CHUNK).transpose(1, 0, 2).astype(BF16)
        wo = ffn_w_out[l, f].reshape(N_FF_CHUNKS, FF_CHUNK, D_MODEL).astype(BF16)
        ffn.append({"wg": wg, "wu": wu, "wo": wo, "g": row(ln_g[l, ln_i]), "b": row(ln_b[l, ln_i])})

    wl = w_in[l]
    o_q = 2 * D_CONV + 2 * D_LRU
    o_kv = o_q + D_ATTN
    o_g = o_kv + 3 * KV_ROW
    pad_idx = _head_pad_index()
    wq = jnp.zeros((D_MODEL, Q_PAD), F32).at[:, pad_idx].set(wl[:, o_q:o_kv])
    hh = np.arange(3 * N_HEADS) // 3
    gate_idx = (hh // GROUP) * LANES + (hh % GROUP) * 3 + np.arange(3 * N_HEADS) % 3
    wgt = jnp.zeros((D_MODEL, N_KV_HEADS * LANES), F32).at[:, gate_idx].set(wl[:, o_g:])
    w_all = jnp.concatenate([wl[:, :o_q], wq, wl[:, o_kv:o_g], wgt], axis=1).astype(BF16)

    conv = {"w": jnp.pad(conv_w[l], ((0, CONV_HALO - CONV_WIDTH), (0, 0))),
            "b": row(conv_b[l]), "ln_g": row(conv_ln_g[l]), "ln_b": row(conv_ln_b[l])}

    def blockdiag(w):
        out = jnp.zeros((D_LRU, D_LRU), F32)
        for n in range(LRU_BLOCKS):
            out = out.at[n * LRU_BW:(n + 1) * LRU_BW, n * LRU_BW:(n + 1) * LRU_BW].set(w[n])
        return out.astype(BF16)

    lru = {"cw": jnp.pad(lru_conv_w[l], ((0, LRU_HALO - LRU_CONV_WIDTH), (0, 0))),
           "cb": row(lru_conv_b[l]),
           "wr": blockdiag(lru_w_gate[l, 0]), "wi": blockdiag(lru_w_gate[l, 1]),
           "bg": lru_b_gate[l].astype(F32), "lam": row(lru_lambda[l])}

    pe = cmp_pe[l].reshape(2, 2, CMP_STRIDE, HEAD_DIM)
    pe_rows = jnp.stack([jnp.concatenate([pe[0, j], pe[0, j], pe[1, j], pe[1, j]], axis=-1)
                         for j in range(2)])
    w2e = jnp.zeros((4, CMP_HIDDEN, KV_ROW), F32)
    for g in range(4):
        w2e = w2e.at[g, :, g * HEAD_DIM:(g + 1) * HEAD_DIM].set(cmp_w2[l, g // 2])
    cmp = {"pe": pe_rows.astype(F32),
           "w1": cmp_w1[l].reshape(2, 2, CMP_STRIDE * HEAD_DIM, CMP_HIDDEN).astype(BF16),
           "b1": cmp_b1[l].astype(F32),
           "w2": w2e.astype(BF16),
           "b2": jnp.concatenate([cmp_b2[l, 0], cmp_b2[l, 0], cmp_b2[l, 1], cmp_b2[l, 1]]).reshape(1, -1)}

    wo = w_out[l]
    wo_attn = jnp.zeros((Q_PAD, D_MODEL), F32).at[pad_idx, :].set(wo[D_CONV + D_LRU:])
    out = {"w": jnp.concatenate([wo[:D_CONV + D_LRU], wo_attn], axis=0).astype(BF16),
           "g": row(ln_g[l, 1]), "b": row(ln_b[l, 1])}
    return {"ffn": ffn, "w_all": w_all, "conv": conv, "lru": lru, "cmp": cmp, "out": out}


def _pad_front(a, rows):
    return jnp.pad(a, ((0, 0), (rows - a.shape[1], 0), (0, 0)))


def _kv6(a, lead):
    return a.reshape(lead + (2, N_KV_HEADS, HEAD_DIM))


TM_PROMPT = 512
TC_PROMPT = 512
TL_PROMPT = 256


def _layer_prompt(x, bsz, t, lw, tabs):
    n = bsz * t
    x = _ffn_ln(x, lw["ffn"][0], TM_PROMPT)
    (u, lx, lg, qc, qr, kvc, kvs, kvs_b, kvw, kvw_b, gates) = _inproj(
        x, lw["w_all"], tabs, TM_PROMPT, t // TM_PROMPT)
    s3 = lambda a: a.reshape(bsz, t, a.shape[-1])
    u3, lx3 = s3(u), s3(lx)
    yc = _conv_group(u3, jnp.zeros((bsz, CONV_HALO, D_CONV), F32), lw["conv"], TC_PROMPT)
    yl, h_last = _lru_group(lx3, s3(lg), jnp.zeros((bsz, LRU_HALO, D_LRU), F32),
                            jnp.zeros((bsz, LRU_HALO, D_LRU), F32), lw["lru"], TL_PROMPT, TL_PROMPT - 1)
    kcv = _compress_prompt(s3(kvc), lw["cmp"])
    ocmp, mneg = _cmp_topk_prompt(s3(qc), kcv)
    ya = _selwin_prompt(s3(qr), mneg, s3(kvs_b), s3(kvw_b), ocmp, s3(gates))
    x = _outproj_ln(x, yc.reshape(n, D_CONV), yl.reshape(n, D_LRU), ya.reshape(n, Q_PAD), lw["out"], TM_PROMPT)
    x = _ffn_ln(x, lw["ffn"][1], TM_PROMPT)
    kvw3 = s3(kvw)
    state = (u3[:, t - (CONV_WIDTH - 1):], lx3[:, t - (LRU_CONV_WIDTH - 1):], h_last[:, 0],
             _kv6(kvc, (bsz, t)), _kv6(kvs, (bsz, t)), _kv6(kvw3[:, t - min(WINDOW, t):], (bsz, min(WINDOW, t))))
    return x, state


T_PAD = 8


def _layer_sample(x, bsz, t, lw, tabs, conv_buf, lru_buf, lru_h, cache_cmp, cache_sel, win_buf, page_table):
    n = bsz * t
    x = _ffn_ln(x, lw["ffn"][0], n)
    (u, lx, lg, qc, qr, kvc, kvs, kvs_b, kvw, kvw_b, gates) = _inproj(x, lw["w_all"], tabs, n, 1)
    s3 = lambda a: a.reshape(bsz, t, a.shape[-1])
    padt = lambda a: jnp.pad(s3(a), ((0, 0), (0, T_PAD - t), (0, 0)))
    yc = _conv_group(padt(u), _pad_front(conv_buf, CONV_HALO), lw["conv"], T_PAD)[:, :t]
    h0 = jnp.broadcast_to(lru_h[:, None, :], (bsz, LRU_HALO, D_LRU))
    yl, h_all = _lru_group(padt(lx), padt(lg), _pad_front(lru_buf, LRU_HALO), h0, lw["lru"], T_PAD, t - 1)
    yl = yl[:, :t]

    def head_cols(a):
        a = a.reshape(bsz, t, N_HEADS, PAD_HEAD).transpose(0, 2, 1, 3).reshape(bsz, N_HEADS * t, PAD_HEAD)
        return jnp.pad(a, ((0, 0), (0, S_COLS - N_HEADS * t), (0, 0)))

    ocmp_cols, picks = _cmp_topk_sample(page_table, cache_cmp, head_cols(qc), lw["cmp"], t)
    pk = picks[:, :, :N_HEADS * t].reshape(bsz, N_SEL, N_KV_HEADS, GROUP, t)[:, :, :, 0, :]
    idx_flat = pk.transpose(0, 3, 2, 1).reshape(-1).astype(I32)

    def head_rows(a, dt):
        a = a.reshape(bsz, N_KV_HEADS, GROUP, t, PAD_HEAD).transpose(0, 3, 1, 2, 4)
        return jnp.pad(a, ((0, 0), (0, 0), (0, 0), (0, HROWS - GROUP), (0, 0))).astype(dt)

    q_rows = head_rows(qr.reshape(bsz, t, N_HEADS, PAD_HEAD).transpose(0, 2, 1, 3).reshape(bsz, N_HEADS * t, PAD_HEAD), BF16)
    ocmp_rows = head_rows(ocmp_cols[:, :N_HEADS * t], F32)
    g3 = gates.reshape(bsz, t, N_KV_HEADS, LANES)[..., :3 * GROUP].reshape(bsz, t, N_KV_HEADS, GROUP, 3)
    gate_rows = jnp.pad(g3, ((0, 0), (0, 0), (0, 0), (0, HROWS - GROUP), (0, LANES - 3)))
    pad16 = lambda a: jnp.pad(s3(a), ((0, 0), (0, 16 - t), (0, 0)))
    ya_rows = _selwin_sample(idx_flat, page_table, cache_sel, q_rows, pad16(kvs_b), win_buf, pad16(kvw_b),
                             ocmp_rows, gate_rows)
    ya = ya_rows[:, :, :, :GROUP].reshape(bsz, t, Q_PAD)

    x = _outproj_ln(x, yc.reshape(n, D_CONV), yl.reshape(n, D_LRU), ya.reshape(n, Q_PAD), lw["out"], n)
    x = _ffn_ln(x, lw["ffn"][1], n)
    new_conv = jnp.concatenate([conv_buf, s3(u)], axis=1)[:, t:]
    new_lru = jnp.concatenate([lru_buf, s3(lx)], axis=1)[:, t:]
    win_all = jnp.concatenate([win_buf, s3(kvw)], axis=1)
    new_win = win_all[:, win_all.shape[1] - min(WINDOW, win_all.shape[1]):]
    state = (new_conv, new_lru, h_all[:, 0], _kv6(kvc, (bsz, t)), _kv6(kvs, (bsz, t)),
             _kv6(new_win, new_win.shape[:2]))
    return x, state


def kernel(x_prompt, x_sample, state_conv, state_lru_conv, state_lru_h, cache_cmp_kv, cache_sel_kv, cache_win_kv,
           page_table, ln_g, ln_b, ffn_w_in, ffn_w_out, w_in, conv_w, conv_b, conv_ln_g, conv_ln_b,
           lru_conv_w, lru_conv_b, lru_w_gate, lru_b_gate, lru_lambda, cmp_pe, cmp_w1, cmp_b1, cmp_w2, cmp_b2, w_out):
    bp, tp, _ = x_prompt.shape
    bs, ts, _ = x_sample.shape
    depth = ln_g.shape[0]
    past = page_table.shape[1] * PAGE_SIZE
    assert past == PAST_LEN and past % SEL_BLOCK == 0
    tabs_p = _rope_tables(jnp.arange(tp))
    tabs_s = _rope_tables(jnp.tile(past + jnp.arange(ts), bs))
    xp = x_prompt.reshape(bp * tp, D_MODEL)
    xs = x_sample.reshape(bs * ts, D_MODEL)
    n_pool = cache_cmp_kv.shape[1]
    st_p, st_s = [], []
    for l in range(depth):
        lw = _prep_layer(l, ln_g, ln_b, ffn_w_in, ffn_w_out, w_in, conv_w, conv_b, conv_ln_g, conv_ln_b,
                         lru_conv_w, lru_conv_b, lru_w_gate, lru_b_gate, lru_lambda,
                         cmp_pe, cmp_w1, cmp_b1, cmp_w2, cmp_b2, w_out)
        xp, sp = _layer_prompt(xp, bp, tp, lw, tabs_p)
        xs, ss = _layer_sample(
            xs, bs, ts, lw, tabs_s, state_conv[l], state_lru_conv[l], state_lru_h[l],
            cache_cmp_kv[l].reshape(n_pool, PAGE_SIZE, KV_ROW), cache_sel_kv[l].reshape(n_pool, PAGE_SIZE, KV_ROW),
            cache_win_kv[l].reshape(bs, -1, KV_ROW), page_table)
        st_p.append(sp)
        st_s.append(ss)
    outs = [xp.reshape(bp, tp, D_MODEL), xs.reshape(bs, ts, D_MODEL)]
    for k in range(6):
        outs.append(jnp.stack([s[k] for s in st_p]))
        outs.append(jnp.stack([s[k] for s in st_s]))
    return tuple(outs)
```

```python
import functools

import numpy as np
import jax
import jax.numpy as jnp
from jax import lax
from jax.experimental import pallas as pl
from jax.experimental.pallas import tpu as pltpu

F32 = jnp.float32
BF16 = jnp.bfloat16
I32 = jnp.int32

D_MODEL = 1024
DEPTH = 2
PAST_LEN = 16384
PAGE_SIZE = 128
D_CONV = 256
CONV_WIDTH = 31
D_LRU = 256
LRU_BLOCKS = 4
LRU_BW = D_LRU // LRU_BLOCKS
LRU_CONV_WIDTH = 4
LRU_C = 8.0
D_ATTN = 512
N_HEADS = 8
HEAD_DIM = 64
N_KV_HEADS = 2
GROUP = N_HEADS // N_KV_HEADS
KV_ROW = 2 * N_KV_HEADS * HEAD_DIM
ROPE_DIM = 16
ROPE_THETA = 500000.0
CMP_BLOCK = 32
CMP_STRIDE = 16
CMP_HIDDEN = 256
SEL_BLOCK = 64
N_SEL = 16
WINDOW = 512
Q_BLOCK = 128
D_FF = 2816
ALPHA = (2 * DEPTH) ** 0.25
LN_EPS = 1e-5
SCALE = HEAD_DIM ** -0.5

LANES = 128
VMEM_LIMIT = 56 * 1024 * 1024
FF_CHUNK = 256
N_FF_CHUNKS = D_FF // FF_CHUNK
PAD_HEAD = 128
Q_PAD = N_HEADS * PAD_HEAD
KV_GROUP_LANES = GROUP * PAD_HEAD
MASK_NEG = -(2.0 ** 60)
SEL_TILE = 512
WIN_KEYS = WINDOW + Q_BLOCK

C_GLU = 0
C_LRUX = 512
C_LRUG = 768
C_Q = 1024
C_KVC = C_Q + Q_PAD
C_KVS = C_KVC + KV_ROW
C_KVW = C_KVS + KV_ROW
C_GATE = C_KVW + KV_ROW
N_PROJ = C_GATE + N_KV_HEADS * LANES


def _cparams(sem):
    return pltpu.CompilerParams(dimension_semantics=sem, vmem_limit_bytes=VMEM_LIMIT)


def _const_spec(shape):
    nd = len(shape)
    return pl.BlockSpec(shape, lambda *_: (0,) * nd, pipeline_mode=pl.Buffered(1))


def _layernorm(y, g, b):
    mu = jnp.mean(y, axis=-1, keepdims=True)
    d = y - mu
    var = jnp.mean(d * d, axis=-1, keepdims=True)
    return d * lax.rsqrt(var + LN_EPS) * g + b


def _dot(a, b):
    return jnp.dot(a, b, preferred_element_type=F32)


def _dot_nt(a, b):
    return lax.dot_general(a, b, (((1,), (1,)), ((), ())), preferred_element_type=F32)


def _dot_tn(a, b):
    return lax.dot_general(a, b, (((0,), (0,)), ((), ())), preferred_element_type=F32)


def _ffn_ln_kernel(x_ref, wg_ref, wu_ref, wo_ref, g_ref, b_ref, o_ref):
    x = x_ref[...]
    xb = x.astype(BF16)
    acc = jnp.zeros(x.shape, F32)
    for c in range(N_FF_CHUNKS):
        gate = _dot(xb, wg_ref[c])
        up = _dot(xb, wu_ref[c])
        h = (gate * jax.nn.sigmoid(gate)) * up
        acc = acc + _dot(h.astype(BF16), wo_ref[c])
    y = ALPHA * x + 0.5 * acc
    o_ref[...] = _layernorm(y, g_ref[...], b_ref[...])


def _ffn_ln(x, fw, tm):
    n = x.shape[0]
    return pl.pallas_call(
        _ffn_ln_kernel,
        grid=(n // tm,),
        in_specs=[
            pl.BlockSpec((tm, D_MODEL), lambda i: (i, 0)),
            _const_spec((N_FF_CHUNKS, D_MODEL, FF_CHUNK)),
            _const_spec((N_FF_CHUNKS, D_MODEL, FF_CHUNK)),
            _const_spec((N_FF_CHUNKS, FF_CHUNK, D_MODEL)),
            _const_spec((1, D_MODEL)),
            _const_spec((1, D_MODEL)),
        ],
        out_specs=pl.BlockSpec((tm, D_MODEL), lambda i: (i, 0)),
        out_shape=jax.ShapeDtypeStruct((n, D_MODEL), F32),
        compiler_params=_cparams(("parallel",)),
        name="ffn_ln",
    )(x, fw["wg"], fw["wu"], fw["wo"], fw["g"], fw["b"])


def _rope(v, cos, s1, s2):
    return v * cos + pltpu.roll(v, 8, 1) * s1 + pltpu.roll(v, LANES - 8, 1) * s2


def _inproj_kernel(x_ref, w_ref, cos_ref, s1_ref, s2_ref,
                   u_ref, lx_ref, lg_ref, qc_ref, qr_ref,
                   kvc_ref, kvs_ref, kvsb_ref, kvw_ref, kvwb_ref, gt_ref):
    xb = x_ref[...].astype(BF16)

    def mm(lo, hi):
        return _dot(xb, w_ref[:, lo:hi])

    glu = mm(C_GLU, C_GLU + 2 * D_CONV)
    u_ref[...] = glu[:, :D_CONV] * jax.nn.sigmoid(glu[:, D_CONV:])
    lx_ref[...] = mm(C_LRUX, C_LRUX + D_LRU)
    lg_ref[...] = mm(C_LRUG, C_LRUG + D_LRU)
    cos = cos_ref[...]
    s1 = s1_ref[...]
    s2 = s2_ref[...]
    for h in range(N_HEADS):
        qh = mm(C_Q + h * PAD_HEAD, C_Q + (h + 1) * PAD_HEAD) * SCALE
        qc_ref[:, h * PAD_HEAD:(h + 1) * PAD_HEAD] = qh.astype(BF16)
        qr_ref[:, h * PAD_HEAD:(h + 1) * PAD_HEAD] = _rope(qh, cos, s1, s2).astype(BF16)
    kvc_ref[...] = mm(C_KVC, C_KVC + KV_ROW)
    for c0, f_ref, b_ref in ((C_KVS, kvs_ref, kvsb_ref), (C_KVW, kvw_ref, kvwb_ref)):
        kv = mm(c0, c0 + KV_ROW)
        k = _rope(kv[:, :LANES], cos, s1, s2)
        v = kv[:, LANES:]
        f_ref[:, 0:LANES] = k
        f_ref[:, LANES:KV_ROW] = v
        b_ref[:, 0:LANES] = k.astype(BF16)
        b_ref[:, LANES:KV_ROW] = v.astype(BF16)
    gt_ref[...] = jax.nn.sigmoid(mm(C_GATE, N_PROJ))


def _rope_t(v, cos, s1, s2):
    return v * cos + pltpu.roll(v, 8, 0) * s1 + pltpu.roll(v, PAD_HEAD - 8, 0) * s2


R_Q = 0
R_VS = Q_PAD
R_VW = R_VS + LANES
R_GATE = R_VW + LANES
N_PROJ_T = R_GATE + N_KV_HEADS * LANES


def _inproj_t_kernel(x_ref, w_ref, wt_ref, cos_ref, s1_ref, s2_ref, cos_t_ref, s1_t_ref, s2_t_ref,
                     u_ref, lx_ref, lg_ref, kvc_ref, kvs_ref, kvw_ref, ks_ref, kw_ref,
                     qc_ref, qr_ref, vs_ref, vw_ref, gt_ref):
    xb = x_ref[...].astype(BF16)
    tm = xb.shape[0]

    def mm(lo, hi):
        return _dot(xb, w_ref[:, lo:hi])

    def mm_t(lo, hi):
        return _dot_nt(wt_ref[lo:hi, :], xb)

    glu = mm(C_GLU, C_GLU + 2 * D_CONV)
    u_ref[...] = glu[:, :D_CONV] * jax.nn.sigmoid(glu[:, D_CONV:])
    lx_ref[...] = mm(C_LRUX, C_LRUX + D_LRU)
    lg_ref[...] = mm(C_LRUG, C_LRUG + D_LRU)
    kvc_ref[...] = mm(C_KVC, C_KVC + KV_ROW)
    cos = cos_ref[...]
    s1 = s1_ref[...]
    s2 = s2_ref[...]
    for c0, f_ref, b_ref in ((C_KVS, kvs_ref, ks_ref), (C_KVW, kvw_ref, kw_ref)):
        kv = mm(c0, c0 + KV_ROW)
        k = _rope(kv[:, :LANES], cos, s1, s2)
        f_ref[:, 0:LANES] = k
        f_ref[:, LANES:KV_ROW] = kv[:, LANES:]
        b_ref[...] = k.astype(BF16)
    cos_t = cos_t_ref[...]
    s1_t = s1_t_ref[...]
    s2_t = s2_t_ref[...]
    for h in range(N_HEADS):
        rows = slice(h * PAD_HEAD, (h + 1) * PAD_HEAD)
        qh = mm_t(R_Q + h * PAD_HEAD, R_Q + (h + 1) * PAD_HEAD) * SCALE
        qc_ref[rows, :] = qh.astype(BF16)
        qr_ref[rows, :] = _rope_t(qh, cos_t, s1_t, s2_t).astype(BF16)
    for r0, o_ref in ((R_VS, vs_ref), (R_VW, vw_ref)):
        vt = mm_t(r0, r0 + LANES).astype(BF16)
        for c in range(tm // LANES):
            o_ref[c] = vt[:, c * LANES:(c + 1) * LANES]
    gt_ref[...] = jax.nn.sigmoid(mm_t(R_GATE, N_PROJ_T))


def _inproj_t(x, w_all, w_t, rope_tabs, rope_tabs_t, bsz, t, tm):
    n = x.shape[0]
    tps = t // tm
    tok = lambda w: pl.BlockSpec((tm, w), lambda i: (i, 0))
    tab = pl.BlockSpec((tm, LANES), lambda i: (i % tps, 0))
    tab_t = pl.BlockSpec((PAD_HEAD, tm), lambda i: (0, i % tps))
    feat_t = lambda r: pl.BlockSpec((None, r, tm), lambda i: (i // tps, 0, i % tps))
    vt_spec = pl.BlockSpec((None, tm // LANES, LANES, LANES), lambda i: (i // tps, i % tps, 0, 0))
    row_outs = [(D_CONV, F32), (D_LRU, F32), (D_LRU, F32), (KV_ROW, F32), (KV_ROW, F32), (KV_ROW, F32),
                (LANES, BF16), (LANES, BF16)]
    vt_shape = jax.ShapeDtypeStruct((bsz, t // LANES, LANES, LANES), BF16)
    return pl.pallas_call(
        _inproj_t_kernel,
        grid=(n // tm,),
        in_specs=[tok(D_MODEL), _const_spec((D_MODEL, N_PROJ)), _const_spec((N_PROJ_T, D_MODEL)),
                  tab, tab, tab, tab_t, tab_t, tab_t],
        out_specs=[tok(w) for w, _ in row_outs]
        + [feat_t(Q_PAD), feat_t(Q_PAD), vt_spec, vt_spec, feat_t(N_KV_HEADS * LANES)],
        out_shape=[jax.ShapeDtypeStruct((n, w), dt) for w, dt in row_outs]
        + [jax.ShapeDtypeStruct((bsz, Q_PAD, t), BF16), jax.ShapeDtypeStruct((bsz, Q_PAD, t), BF16),
           vt_shape, vt_shape, jax.ShapeDtypeStruct((bsz, N_KV_HEADS * LANES, t), F32)],
        compiler_params=_cparams(("parallel",)),
        name="inproj_t",
    )(x, w_all, w_t, *rope_tabs, *rope_tabs_t)


def _inproj(x, w_all, rope_tabs, tm, tiles_per_seq):
    n = x.shape[0]
    cos, s1, s2 = rope_tabs
    tok = lambda w: pl.BlockSpec((tm, w), lambda i: (i, 0))
    tab = pl.BlockSpec((tm, LANES), lambda i: (i % tiles_per_seq, 0))
    outs = [
        (D_CONV, F32), (D_LRU, F32), (D_LRU, F32), (Q_PAD, BF16), (Q_PAD, BF16),
        (KV_ROW, F32), (KV_ROW, F32), (KV_ROW, BF16), (KV_ROW, F32), (KV_ROW, BF16),
        (N_KV_HEADS * LANES, F32),
    ]
    return pl.pallas_call(
        _inproj_kernel,
        grid=(n // tm,),
        in_specs=[tok(D_MODEL), _const_spec((D_MODEL, N_PROJ)), tab, tab, tab],
        out_specs=[tok(w) for w, _ in outs],
        out_shape=[jax.ShapeDtypeStruct((n, w), dt) for w, dt in outs],
        compiler_params=_cparams(("parallel",)),
        name="inproj",
    )(x, w_all, cos, s1, s2)


CONV_HALO = 32
CONV_ROWS = 64


def _conv_kernel(u_ref, buf_ref, w_ref, b_ref, g_ref, bb_ref, o_ref, ext_ref):
    t = pl.program_id(1)
    tc = u_ref.shape[0]

    @pl.when(t == 0)
    def _():
        ext_ref[0:CONV_HALO, :] = buf_ref[...]

    ext_ref[CONV_HALO:CONV_HALO + tc, :] = u_ref[...]
    off = CONV_HALO - (CONV_WIDTH - 1)
    rows = min(CONV_ROWS, tc)
    for r0 in range(0, tc, rows):
        acc = jnp.zeros((rows, D_CONV), F32)
        for k in range(CONV_WIDTH):
            acc = acc + ext_ref[r0 + off + k:r0 + off + k + rows, :] * w_ref[k:k + 1, :]
        y = acc + b_ref[...]
        y = _layernorm(y, g_ref[...], bb_ref[...])
        o_ref[r0:r0 + rows, :] = (y * jax.nn.sigmoid(y)).astype(BF16)
    ext_ref[0:CONV_HALO, :] = ext_ref[tc:tc + CONV_HALO, :]


def _conv_group(u, buf, cw, tc):
    b, t, _ = u.shape
    return pl.pallas_call(
        _conv_kernel,
        grid=(b, t // tc),
        in_specs=[
            pl.BlockSpec((None, tc, D_CONV), lambda i, j: (i, j, 0)),
            pl.BlockSpec((None, CONV_HALO, D_CONV), lambda i, j: (i, 0, 0)),
            _const_spec((CONV_HALO, D_CONV)),
            _const_spec((1, D_CONV)), _const_spec((1, D_CONV)), _const_spec((1, D_CONV)),
        ],
        out_specs=pl.BlockSpec((None, tc, D_CONV), lambda i, j: (i, j, 0)),
        out_shape=jax.ShapeDtypeStruct((b, t, D_CONV), BF16),
        scratch_shapes=[pltpu.VMEM((tc + CONV_HALO, D_CONV), F32)],
        compiler_params=_cparams(("parallel", "arbitrary")),
        name="conv_group",
    )(u, buf, cw["w"], cw["b"], cw["ln_g"], cw["ln_b"])


LRU_HALO = 8


def _lru_kernel(last_row, x_ref, gate_ref, buf_ref, h0_ref, cw_ref, cb_ref, wr_ref, wi_ref,
                bg_ref, lam_ref, y_ref, hl_ref, ext_ref, hc_ref):
    t = pl.program_id(1)
    tl = x_ref.shape[0]

    @pl.when(t == 0)
    def _():
        ext_ref[0:LRU_HALO, :] = buf_ref[...]
        hc_ref[...] = h0_ref[...]

    ext_ref[LRU_HALO:LRU_HALO + tl, :] = x_ref[...]
    off = LRU_HALO - (LRU_CONV_WIDTH - 1)
    xl = jnp.zeros((tl, D_LRU), F32)
    for k in range(LRU_CONV_WIDTH):
        xl = xl + ext_ref[off + k:off + k + tl, :] * cw_ref[k:k + 1, :]
    xl = xl + cb_ref[...]
    ext_ref[0:LRU_HALO, :] = ext_ref[tl:tl + LRU_HALO, :]

    xb = xl.astype(BF16)
    r_gate = jax.nn.sigmoid(_dot(xb, wr_ref[...]) + bg_ref[0:1, :])
    i_gate = jax.nn.sigmoid(_dot(xb, wi_ref[...]) + bg_ref[1:2, :])
    log_a = LRU_C * r_gate * jax.nn.log_sigmoid(lam_ref[...])
    a = jnp.exp(log_a)
    bv = jnp.sqrt(-jnp.tanh(log_a) * (a * a + 1.0)) * (i_gate * xl)

    row = lax.broadcasted_iota(I32, (tl, D_LRU), 0)
    s = 1
    while s < tl:
        keep = row >= s
        a_sh = jnp.where(keep, pltpu.roll(a, s, 0), 1.0)
        b_sh = jnp.where(keep, pltpu.roll(bv, s, 0), 0.0)
        bv = a * b_sh + bv
        a = a * a_sh
        s *= 2
    h = a * hc_ref[0:1, :] + bv
    hc_ref[...] = jnp.broadcast_to(h[last_row:last_row + 1, :], hc_ref.shape)
    hl_ref[...] = hc_ref[...]
    y_ref[...] = (h * jax.nn.gelu(gate_ref[...])).astype(BF16)


def _lru_group(x, gate, buf, h0, lw, tl, last_row):
    b, t, _ = x.shape
    assert last_row == tl - 1 or t == tl
    seq = pl.BlockSpec((None, tl, D_LRU), lambda i, j: (i, j, 0))
    per_b = pl.BlockSpec((None, LRU_HALO, D_LRU), lambda i, j: (i, 0, 0))
    return pl.pallas_call(
        functools.partial(_lru_kernel, last_row),
        grid=(b, t // tl),
        in_specs=[seq, seq, per_b, per_b,
                  _const_spec((LRU_HALO, D_LRU)), _const_spec((1, D_LRU)),
                  _const_spec((D_LRU, D_LRU)), _const_spec((D_LRU, D_LRU)),
                  _const_spec((2, D_LRU)), _const_spec((1, D_LRU))],
        out_specs=[seq, per_b],
        out_shape=[jax.ShapeDtypeStruct((b, t, D_LRU), BF16),
                   jax.ShapeDtypeStruct((b, LRU_HALO, D_LRU), F32)],
        scratch_shapes=[pltpu.VMEM((tl + LRU_HALO, D_LRU), F32),
                        pltpu.VMEM((LRU_HALO, D_LRU), F32)],
        compiler_params=_cparams(("parallel", "arbitrary")),
        name="lru_group",
    )(x, gate, buf, h0, lw["cw"], lw["cb"], lw["wr"], lw["wi"], lw["bg"], lw["lam"])


def _compress_rows(row_refs, pe_ref, w1_ref, b1_ref, w2_ref, b2_ref, carry_ref):
    lo = lax.broadcasted_iota(I32, (1, LANES), 1) < HEAD_DIM
    xs = [[[] for _ in range(4)] for _ in range(2)]
    for halves in row_refs:
        n = halves[0].shape[0] // CMP_STRIDE
        cols = [[[] for _ in range(4)] for _ in range(2)]
        for rp in range(CMP_STRIDE // 2):
            for half, ref in enumerate(halves):
                pa = ref[pl.ds(2 * rp, n, stride=CMP_STRIDE), :]
                pb = ref[pl.ds(2 * rp + 1, n, stride=CMP_STRIDE), :]
                for j in range(2):
                    va = pa + pe_ref[j, 2 * rp:2 * rp + 1, half * LANES:(half + 1) * LANES]
                    vb = pb + pe_ref[j, 2 * rp + 1:2 * rp + 2, half * LANES:(half + 1) * LANES]
                    cols[j][2 * half].append(jnp.where(lo, va, pltpu.roll(vb, HEAD_DIM, 1)))
                    cols[j][2 * half + 1].append(jnp.where(lo, pltpu.roll(va, HEAD_DIM, 1), vb))
        for j in range(2):
            for g in range(4):
                xs[j][g].append(jnp.concatenate(cols[j][g], axis=1))
    out = None
    for g in range(4):
        sidx = g // 2
        x0 = jnp.concatenate(xs[0][g], axis=0).astype(BF16)
        x1 = jnp.concatenate(xs[1][g], axis=0).astype(BF16)
        p0 = _dot(x0, w1_ref[sidx, 0])
        p1 = _dot(x1, w1_ref[sidx, 1])
        n_tot = p0.shape[0]
        row = lax.broadcasted_iota(I32, p0.shape, 0)
        p0s = jnp.where(row == 0, carry_ref[g, 0:1, :], pltpu.roll(p0, 1, 0))
        carry_ref[g, 0:1, :] = p0[n_tot - 1:n_tot, :]
        h = (b1_ref[sidx:sidx + 1, :] + p0s) + p1
        part = _dot(jax.nn.gelu(h).astype(BF16), w2_ref[g])
        out = part if out is None else out + part
    return out + b2_ref[...]


def _cmp_weight_specs():
    return [
        _const_spec((2, CMP_STRIDE, KV_ROW)),
        _const_spec((2, 2, CMP_STRIDE * HEAD_DIM, CMP_HIDDEN)),
        _const_spec((2, CMP_HIDDEN)),
        _const_spec((4, CMP_HIDDEN, KV_ROW)),
        _const_spec((1, KV_ROW)),
    ]


def _cmp_weight_args(cw):
    return (cw["pe"], cw["w1"], cw["b1"], cw["w2"], cw["b2"])


CMP_TILE_ROWS = 2048


def _compress_prompt_kernel(k_ref, v_ref, pe_ref, w1_ref, b1_ref, w2_ref, b2_ref, kc_ref, vct_ref, carry_ref):
    @pl.when(pl.program_id(1) == 0)
    def _():
        carry_ref[...] = jnp.zeros(carry_ref.shape, F32)

    out = _compress_rows([(k_ref, v_ref)], pe_ref, w1_ref, b1_ref, w2_ref, b2_ref, carry_ref)
    kc_ref[...] = out[:, 0:LANES].astype(BF16)
    vct_ref[...] = out[:, LANES:KV_ROW].T.astype(BF16)


def _compress_prompt(kvc, cw):
    b, t, _ = kvc.shape
    n_e = CMP_TILE_ROWS // CMP_STRIDE
    return pl.pallas_call(
        _compress_prompt_kernel,
        grid=(b, t // CMP_TILE_ROWS),
        in_specs=[pl.BlockSpec((None, CMP_TILE_ROWS, LANES), lambda i, j: (i, j, 0)),
                  pl.BlockSpec((None, CMP_TILE_ROWS, LANES), lambda i, j: (i, j, 1))]
        + _cmp_weight_specs(),
        out_specs=[pl.BlockSpec((None, n_e, LANES), lambda i, j: (i, j, 0)),
                   pl.BlockSpec((None, LANES, n_e), lambda i, j: (i, 0, j))],
        out_shape=[jax.ShapeDtypeStruct((b, t // CMP_STRIDE, LANES), BF16),
                   jax.ShapeDtypeStruct((b, LANES, t // CMP_STRIDE), BF16)],
        scratch_shapes=[pltpu.VMEM((4, 8, CMP_HIDDEN), F32)],
        compiler_params=_cparams(("parallel", "arbitrary")),
        name="compress_prompt",
    )(kvc, kvc, *_cmp_weight_args(cw))


def _selection_scores(pk_ref, n_sel):
    ratio = SEL_BLOCK // CMP_STRIDE
    slc = pk_ref[pl.ds(0, n_sel, stride=ratio), :]
    for o in range(1, ratio):
        slc = slc + 2.0 * pk_ref[pl.ds(o, n_sel, stride=ratio), :]
    return slc + pk_ref[pl.ds(ratio, n_sel, stride=ratio), :]


def _topk_rounds(score, n_rows):
    j = lax.broadcasted_iota(I32, score.shape, 0).astype(F32)
    sel = jnp.zeros(score.shape, F32)
    picks = []
    for _ in range(N_SEL):
        cm = jnp.max(score, axis=0, keepdims=True)
        mi = jnp.min(jnp.where(score == cm, j, float(n_rows)), axis=0, keepdims=True)
        hit = j == mi
        sel = jnp.where(hit, 1.0, sel)
        score = jnp.where(hit, -jnp.inf, score)
        picks.append(mi)
    return sel, picks


def _masked_softmax_rows(s_t, valid):
    s_t = jnp.where(valid, s_t, -jnp.inf)
    m = jnp.max(s_t, axis=0, keepdims=True)
    m = jnp.where(m > -jnp.inf, m, 0.0)
    e = jnp.exp(s_t - m)
    d = jnp.sum(e, axis=0, keepdims=True)
    return e / jnp.where(d > 0, d, 1.0)


def _cmp_topk_prompt_kernel(qc_ref, kc_ref, vct_ref, ocmp_ref, mneg_ref, pk_ref):
    i = pl.program_id(2)
    n_e = kc_ref.shape[0]
    n_sel = n_e // (SEL_BLOCK // CMP_STRIDE)
    kc = kc_ref[...]
    vct = vct_ref[...]
    e_idx = lax.broadcasted_iota(I32, (n_e, Q_BLOCK), 0)
    qpos = i * Q_BLOCK + lax.broadcasted_iota(I32, (n_e, Q_BLOCK), 1)
    valid = (e_idx >= 1) & (CMP_STRIDE * e_idx + (CMP_STRIDE - 1) <= qpos)
    pkv = jnp.zeros((n_e, Q_BLOCK), F32)
    for h in range(GROUP):
        rows = slice(h * PAD_HEAD, (h + 1) * PAD_HEAD)
        p = _masked_softmax_rows(_dot(kc, qc_ref[rows, :]), valid)
        pkv = pkv + p
        ocmp_ref[rows, :] = _dot(vct, p.astype(BF16))
    pk_ref[0:n_e, :] = pkv
    pk_ref[n_e:n_e + 8, :] = jnp.zeros((8, Q_BLOCK), F32)
    slc = _selection_scores(pk_ref, n_sel)

    j = lax.broadcasted_iota(I32, (n_sel, Q_BLOCK), 0)
    qp = i * Q_BLOCK + lax.broadcasted_iota(I32, (n_sel, Q_BLOCK), 1)
    qblk = jnp.right_shift(qp, 6)
    forced = (j == 0) | (j == qblk) | (j == qblk - 1)
    score = jnp.where(forced, jnp.inf, jnp.where(j * SEL_BLOCK <= qp, slc, -jnp.inf))
    sel, _ = _topk_rounds(score, n_sel)
    mneg_ref[...] = jnp.where(sel > 0, 0.0, MASK_NEG).astype(BF16)


def _cmp_topk_prompt(qc_t, kc, vc_t):
    b, _, t = qc_t.shape
    n_e = kc.shape[1]
    n_sel = t // SEL_BLOCK
    qblk = pl.BlockSpec((None, KV_GROUP_LANES, Q_BLOCK), lambda bi, k, i: (bi, k, i))
    return pl.pallas_call(
        _cmp_topk_prompt_kernel,
        grid=(b, N_KV_HEADS, t // Q_BLOCK),
        in_specs=[
            qblk,
            pl.BlockSpec((None, n_e, LANES), lambda bi, k, i: (bi, 0, 0)),
            pl.BlockSpec((None, LANES, n_e), lambda bi, k, i: (bi, 0, 0)),
        ],
        out_specs=[
            qblk,
            pl.BlockSpec((None, None, n_sel, Q_BLOCK), lambda bi, k, i: (bi, k, 0, i)),
        ],
        out_shape=[jax.ShapeDtypeStruct((b, Q_PAD, t), F32),
                   jax.ShapeDtypeStruct((b, N_KV_HEADS, n_sel, t), BF16)],
        scratch_shapes=[pltpu.VMEM((n_e + 8, Q_BLOCK), F32)],
        compiler_params=_cparams(("parallel", "parallel", "arbitrary")),
        name="cmp_topk_prompt",
    )(qc_t, kc, vc_t)


def _selwin_prompt_kernel(qr_ref, mneg_ref, ks_ref, vs_ref, kw_ref, vw_ref, ocmp_ref, gt_ref, o_ref,
                          qa_ref, sa_ref, sb_ref, pa_ref, pb_ref):
    i = pl.program_id(2)
    n_sel = mneg_ref.shape[0]
    cols = GROUP * Q_BLOCK
    for h in range(GROUP):
        c = slice(h * Q_BLOCK, (h + 1) * Q_BLOCK)
        qa_ref[0:PAD_HEAD, c] = qr_ref[h * PAD_HEAD:(h + 1) * PAD_HEAD, :]
        qa_ref[PAD_HEAD:PAD_HEAD + n_sel, c] = mneg_ref[...]
    qa = qa_ref[...]
    q0 = i * Q_BLOCK
    qpos = q0 + (lax.broadcasted_iota(I32, (1, cols), 1) & (Q_BLOCK - 1))

    tile = SEL_TILE
    sub = tile // LANES
    key_row = lax.broadcasted_iota(I32, (tile, n_sel), 0)
    blk_lane = lax.broadcasted_iota(I32, (tile, n_sel), 1)
    key_col = lax.broadcasted_iota(I32, (tile, 1), 0)

    def scores(t):
        k0 = pl.multiple_of(t * tile, tile)
        kt = ks_ref[pl.ds(k0, tile), :]
        onehot = jnp.where(jnp.right_shift(k0 + key_row, 6) == blk_lane, 1.0, 0.0).astype(BF16)
        return _dot(jnp.concatenate([kt, onehot], axis=1), qa)

    def weighted_values(t, p):
        v4 = vs_ref[pl.ds(t * sub, sub)]
        return _dot(jnp.concatenate([v4[c] for c in range(sub)], axis=1), p)

    def stage(t, cur, nxt, carry, masked, prefetch):
        m, l, acc, alpha_prev = carry
        if prefetch:
            s_refs[nxt][...] = scores(t + 1)
        acc = alpha_prev * acc + weighted_values(jnp.maximum(t - 1, 0), p_refs[nxt][...])
        s = s_refs[cur][...]
        if masked:
            s = jnp.where(t * tile + key_col <= qpos, s, MASK_NEG)
        m_new = jnp.maximum(m, jnp.max(s, axis=0, keepdims=True))
        alpha = jnp.exp(m - m_new)
        p = jnp.exp(s - m_new)
        l = alpha * l + jnp.sum(p, axis=0, keepdims=True)
        p_refs[cur][...] = p.astype(BF16)
        return m_new, l, acc, alpha

    s_refs = (sa_ref, sb_ref)
    p_refs = (pa_ref, pb_ref)
    sa_ref[...] = scores(0)
    pb_ref[...] = jnp.zeros(pb_ref.shape, BF16)
    init = (jnp.full((1, cols), MASK_NEG, F32), jnp.zeros((1, cols), F32),
            jnp.zeros((LANES, cols), F32), jnp.ones((1, cols), F32))

    def pair(u, carry):
        carry = stage(2 * u, 0, 1, carry, False, True)
        return stage(2 * u + 1, 1, 0, carry, False, True)

    u_diag = q0 // (2 * tile)
    carry = lax.fori_loop(0, u_diag, pair, init)
    carry = stage(2 * u_diag, 0, 1, carry, True, True)
    _, l_sel, acc_sel, alpha_last = stage(2 * u_diag + 1, 1, 0, carry, True, False)
    acc_sel = alpha_last * acc_sel + weighted_values(2 * u_diag + 1, pb_ref[...])
    o_sel = acc_sel / l_sel

    wsub = WIN_KEYS // LANES
    start = pl.multiple_of(jnp.maximum(q0 + Q_BLOCK - WIN_KEYS, 0), Q_BLOCK)
    s = _dot(kw_ref[pl.ds(start, WIN_KEYS), :], qa[0:PAD_HEAD, :])
    dpos = qpos - (start + lax.broadcasted_iota(I32, (WIN_KEYS, 1), 0))
    s = jnp.where((dpos >= 0) & (dpos <= WINDOW), s, -jnp.inf)
    m = jnp.max(s, axis=0, keepdims=True)
    p = jnp.exp(s - m)
    v5 = vw_ref[pl.ds(start // LANES, wsub)]
    vt = jnp.concatenate([v5[c] for c in range(wsub)], axis=1)
    o_win = _dot(vt, p.astype(BF16)) / jnp.sum(p, axis=0, keepdims=True)

    for h in range(GROUP):
        c = slice(h * Q_BLOCK, (h + 1) * Q_BLOCK)
        rows = slice(h * PAD_HEAD, (h + 1) * PAD_HEAD)
        g_cmp = gt_ref[3 * h:3 * h + 1, :]
        g_sel = gt_ref[3 * h + 1:3 * h + 2, :]
        g_win = gt_ref[3 * h + 2:3 * h + 3, :]
        o = g_cmp * ocmp_ref[rows, :] + g_sel * o_sel[:, c] + g_win * o_win[:, c]
        o_ref[rows, :] = o.astype(BF16)


def _selwin_prompt(qr_t, mneg, ks, vs_t, kw, vw_t, ocmp_t, gates_t):
    b, _, t = qr_t.shape
    n_sel = mneg.shape[2]
    assert t % SEL_TILE == 0 and t >= WIN_KEYS
    qblk = pl.BlockSpec((None, KV_GROUP_LANES, Q_BLOCK), lambda bi, k, i: (bi, k, i))
    keys = pl.BlockSpec((None, t, LANES), lambda bi, k, i: (bi, 0, 0))
    vals = pl.BlockSpec((None, t // LANES, LANES, LANES), lambda bi, k, i: (bi, 0, 0, 0))
    return pl.pallas_call(
        _selwin_prompt_kernel,
        grid=(b, N_KV_HEADS, t // Q_BLOCK),
        in_specs=[
            qblk,
            pl.BlockSpec((None, None, n_sel, Q_BLOCK), lambda bi, k, i: (bi, k, 0, i)),
            keys, vals, keys, vals, qblk,
            pl.BlockSpec((None, LANES, Q_BLOCK), lambda bi, k, i: (bi, k, i)),
        ],
        out_specs=qblk,
        out_shape=jax.ShapeDtypeStruct((b, Q_PAD, t), BF16),
        scratch_shapes=[pltpu.VMEM((PAD_HEAD + n_sel, GROUP * Q_BLOCK), BF16),
                        pltpu.VMEM((SEL_TILE, GROUP * Q_BLOCK), F32),
                        pltpu.VMEM((SEL_TILE, GROUP * Q_BLOCK), F32),
                        pltpu.VMEM((SEL_TILE, GROUP * Q_BLOCK), BF16),
                        pltpu.VMEM((SEL_TILE, GROUP * Q_BLOCK), BF16)],
        compiler_params=_cparams(("parallel", "parallel", "arbitrary")),
        name="selwin_prompt",
    )(qr_t, mneg, ks, vs_t, kw, vw_t, ocmp_t, gates_t)


PAGES_PER_STEP = 8
S_COLS = LANES


def _cmp_topk_sample_kernel(n_t, pt_ref, *refs):
    page_refs = refs[:PAGES_PER_STEP]
    (q_ref, pe_ref, w1_ref, b1_ref, w2_ref, b2_ref,
     ocmp_ref, idx_ref, kcv_ref, pk_ref, carry_ref, xk_ref, xv_ref) = refs[PAGES_PER_STEP:]
    s = pl.program_id(1)
    n_steps = pl.num_programs(1)
    n_e = kcv_ref.shape[0]
    step_e = PAGES_PER_STEP * PAGE_SIZE // CMP_STRIDE

    @pl.when(s == 0)
    def _():
        carry_ref[...] = jnp.zeros(carry_ref.shape, F32)

    for k, page in enumerate(page_refs):
        rows = slice(k * PAGE_SIZE, (k + 1) * PAGE_SIZE)
        xk_ref[rows, :] = page[0:LANES, :].T
        xv_ref[rows, :] = page[LANES:KV_ROW, :].T
    out = _compress_rows([(xk_ref, xv_ref)], pe_ref, w1_ref, b1_ref, w2_ref, b2_ref, carry_ref)
    kcv_ref[pl.ds(pl.multiple_of(s * step_e, step_e), step_e), :] = out.astype(BF16)

    @pl.when(s == n_steps - 1)
    def _():
        n_sel = n_e // (SEL_BLOCK // CMP_STRIDE) + 1
        n_sel_rows = pk_ref.shape[0] // (SEL_BLOCK // CMP_STRIDE) - 2
        kc = kcv_ref[:, 0:LANES]
        vc = kcv_ref[:, LANES:KV_ROW]
        e_idx = lax.broadcasted_iota(I32, (n_e, S_COLS), 0)
        p = _masked_softmax_rows(_dot_nt(kc, q_ref[...]), e_idx >= 1)
        ocmp_ref[...] = _dot_tn(p.astype(BF16), vc)
        pkv = p
        for hh in range(1, GROUP):
            pkv = pkv + pltpu.roll(p, S_COLS - n_t * hh, 1)
        pk_ref[0:n_e, :] = pkv
        pk_ref[n_e:, :] = jnp.zeros((pk_ref.shape[0] - n_e, S_COLS), F32)
        slc = _selection_scores(pk_ref, n_sel_rows)
        j = lax.broadcasted_iota(I32, (n_sel_rows, S_COLS), 0)
        qp = PAST_LEN + (lax.broadcasted_iota(I32, (n_sel_rows, S_COLS), 1) & (n_t - 1))
        qblk = jnp.right_shift(qp, 6)
        forced = (j == 0) | (j == qblk) | (j == qblk - 1)
        in_range = j < n_sel
        score = jnp.where(forced & in_range, jnp.inf,
                          jnp.where((j * SEL_BLOCK <= qp) & in_range, slc, -jnp.inf))
        _, picks = _topk_rounds(score, n_sel_rows)
        for r, mi in enumerate(picks):
            idx_ref[r:r + 1, :] = mi.astype(I32)


def _cmp_topk_sample(page_table, cache_cmp_t, layer, q_cols, cw, n_t):
    assert n_t & (n_t - 1) == 0 and N_HEADS * n_t <= S_COLS
    b, n_pages = page_table.shape
    n_e = n_pages * PAGE_SIZE // CMP_STRIDE
    n_sel_rows = ((n_e // 4 + 1) + 7) // 8 * 8
    pk_rows = 4 * (n_sel_rows + 2)
    n_steps = n_pages // PAGES_PER_STEP
    step_rows = PAGES_PER_STEP * PAGE_SIZE

    def page_spec(k):
        return pl.BlockSpec((None, None, KV_ROW, PAGE_SIZE),
                            lambda bi, s, pt: (layer, pt[bi, s * PAGES_PER_STEP + k], 0, 0))

    per_b = lambda rows, w: pl.BlockSpec((None, rows, w), lambda bi, s, pt: (bi, 0, 0))
    grid_spec = pltpu.PrefetchScalarGridSpec(
        num_scalar_prefetch=1,
        grid=(b, n_steps),
        in_specs=[page_spec(k) for k in range(PAGES_PER_STEP)]
        + [per_b(S_COLS, LANES)] + _cmp_weight_specs(),
        out_specs=[per_b(S_COLS, LANES), per_b(N_SEL, S_COLS)],
        scratch_shapes=[pltpu.VMEM((n_e, KV_ROW), BF16),
                        pltpu.VMEM((pk_rows, S_COLS), F32),
                        pltpu.VMEM((4, 8, CMP_HIDDEN), F32),
                        pltpu.VMEM((step_rows, LANES), F32),
                        pltpu.VMEM((step_rows, LANES), F32)],
    )
    return pl.pallas_call(
        functools.partial(_cmp_topk_sample_kernel, n_t),
        grid_spec=grid_spec,
        out_shape=[jax.ShapeDtypeStruct((b, S_COLS, LANES), F32),
                   jax.ShapeDtypeStruct((b, N_SEL, S_COLS), I32)],
        compiler_params=_cparams(("parallel", "arbitrary")),
        name="cmp_topk_sample",
    )(page_table, *([cache_cmp_t] * PAGES_PER_STEP), q_cols, *_cmp_weight_args(cw))


HROWS = 8


def _selwin_sample_kernel(idx_ref, pt_ref, *refs):
    page_refs = refs[:N_SEL]
    (q_ref, new_s_ref, win_ref, new_w_ref, ocmp_ref, gt_ref, o_ref, k_ref, v_ref) = refs[N_SEL:]
    bi = pl.program_id(0)
    tq = pl.program_id(1)
    kh = pl.program_id(2)
    n_t = pl.num_programs(1)
    qpos = PAST_LEN + tq
    q = q_ref[...]
    base = ((bi * n_t + tq) * N_KV_HEADS + kh) * N_SEL
    new_block = PAST_LEN // SEL_BLOCK

    valid_parts = []
    lane = lax.broadcasted_iota(I32, (1, PAGE_SIZE), 1)
    for r in range(N_SEL):
        j = idx_ref[base + r]
        is_new = j == new_block
        cols = slice(r * PAGE_SIZE, (r + 1) * PAGE_SIZE)

        @pl.when(is_new)
        def _():
            k_ref[:, cols] = new_s_ref[0:LANES, :].astype(BF16)
            v_ref[:, cols] = new_s_ref[LANES:KV_ROW, :].astype(BF16)

        @pl.when(jnp.logical_not(is_new))
        def _():
            k_ref[:, cols] = page_refs[r][0:LANES, :].astype(BF16)
            v_ref[:, cols] = page_refs[r][LANES:KV_ROW, :].astype(BF16)

        kpos = jnp.right_shift(j, 1) * PAGE_SIZE + lane
        valid_parts.append((jnp.right_shift(kpos, 6) == j) & (kpos <= qpos))
    valid = jnp.concatenate(valid_parts, axis=1)
    s = jnp.where(valid, _dot(q, k_ref[...]), -jnp.inf)
    m = jnp.max(s, axis=-1, keepdims=True)
    p = jnp.exp(s - m)
    o_sel = _dot_nt(p.astype(BF16), v_ref[...]) / jnp.sum(p, axis=-1, keepdims=True)

    wb = win_ref.shape[1]
    s_old = _dot(q, win_ref[0:LANES, :].astype(BF16))
    d_old = qpos - (PAST_LEN - wb + lax.broadcasted_iota(I32, (1, wb), 1))
    s_old = jnp.where((d_old >= 0) & (d_old <= WINDOW), s_old, -jnp.inf)
    s_new = _dot(q, new_w_ref[0:LANES, :].astype(BF16))
    d_new = tq - lax.broadcasted_iota(I32, (1, new_w_ref.shape[1]), 1)
    s_new = jnp.where((d_new >= 0) & (d_new <= WINDOW), s_new, -jnp.inf)
    m = jnp.maximum(jnp.max(s_old, axis=-1, keepdims=True), jnp.max(s_new, axis=-1, keepdims=True))
    p_old = jnp.exp(s_old - m)
    p_new = jnp.exp(s_new - m)
    den = jnp.sum(p_old, axis=-1, keepdims=True) + jnp.sum(p_new, axis=-1, keepdims=True)
    o_win = (_dot_nt(p_old.astype(BF16), win_ref[LANES:KV_ROW, :].astype(BF16))
             + _dot_nt(p_new.astype(BF16), new_w_ref[LANES:KV_ROW, :].astype(BF16))) / den

    g = gt_ref[...]
    o = g[:, 0:1] * ocmp_ref[...] + g[:, 1:2] * o_sel + g[:, 2:3] * o_win
    o_ref[...] = o.astype(BF16)


def _selwin_sample(idx_flat, page_table, cache_sel_t, win_t, layer, q_rows, new_s_t, new_w_t, ocmp_rows, gate_rows):
    b, n_t = q_rows.shape[:2]
    wb = win_t.shape[-1]
    n_t_static = n_t
    last_page = page_table.shape[1] - 1

    def page_spec(r):
        def imap(bi, tq, kh, idx, pt):
            j = idx[((bi * n_t_static + tq) * N_KV_HEADS + kh) * N_SEL + r]
            return (layer, pt[bi, jnp.minimum(jnp.right_shift(j, 1), last_page)], 0, 0)
        return pl.BlockSpec((None, None, KV_ROW, PAGE_SIZE), imap)

    row5 = pl.BlockSpec((None, None, None, HROWS, LANES), lambda bi, tq, kh, idx, pt: (bi, tq, kh, 0, 0))
    new_rows = pl.BlockSpec((None, KV_ROW, LANES), lambda bi, tq, kh, idx, pt: (bi, 0, 0))
    grid_spec = pltpu.PrefetchScalarGridSpec(
        num_scalar_prefetch=2,
        grid=(b, n_t, N_KV_HEADS),
        in_specs=[page_spec(r) for r in range(N_SEL)]
        + [row5, new_rows,
           pl.BlockSpec((None, None, KV_ROW, wb), lambda bi, tq, kh, idx, pt: (layer, bi, 0, 0)),
           new_rows, row5, row5],
        out_specs=row5,
        scratch_shapes=[pltpu.VMEM((LANES, N_SEL * PAGE_SIZE), BF16),
                        pltpu.VMEM((LANES, N_SEL * PAGE_SIZE), BF16)],
    )
    return pl.pallas_call(
        _selwin_sample_kernel,
        grid_spec=grid_spec,
        out_shape=jax.ShapeDtypeStruct((b, n_t, N_KV_HEADS, HROWS, LANES), BF16),
        compiler_params=_cparams(("arbitrary", "arbitrary", "arbitrary")),
        name="selwin_sample",
    )(idx_flat, page_table, *([cache_sel_t] * N_SEL), q_rows, new_s_t, win_t, new_w_t, ocmp_rows, gate_rows)


def _outproj_ln_kernel(attn_transposed, x_ref, yc_ref, yl_ref, ya_ref, w_ref, g_ref, b_ref, o_ref):
    y = _dot(yc_ref[...], w_ref[0:D_CONV, :])
    y = y + _dot(yl_ref[...], w_ref[D_CONV:D_CONV + D_LRU, :])
    w_attn = w_ref[D_CONV + D_LRU:, :]
    y = y + (_dot_tn(ya_ref[...], w_attn) if attn_transposed else _dot(ya_ref[...], w_attn))
    o_ref[...] = _layernorm(ALPHA * x_ref[...] + y, g_ref[...], b_ref[...])


def _outproj_ln(x, yc, yl, ya, ow, tm):
    n = x.shape[0]
    tok = lambda w: pl.BlockSpec((tm, w), lambda i: (i, 0))
    attn_transposed = ya.ndim == 3
    if attn_transposed:
        tps = ya.shape[2] // tm
        ya_spec = pl.BlockSpec((None, Q_PAD, tm), lambda i: (i // tps, 0, i % tps))
    else:
        ya_spec = tok(Q_PAD)
    return pl.pallas_call(
        functools.partial(_outproj_ln_kernel, attn_transposed),
        grid=(n // tm,),
        in_specs=[tok(D_MODEL), tok(D_CONV), tok(D_LRU), ya_spec,
                  _const_spec((D_CONV + D_LRU + Q_PAD, D_MODEL)),
                  _const_spec((1, D_MODEL)), _const_spec((1, D_MODEL))],
        out_specs=tok(D_MODEL),
        out_shape=jax.ShapeDtypeStruct((n, D_MODEL), F32),
        compiler_params=_cparams(("parallel",)),
        name="outproj_ln",
    )(x, yc, yl, ya, ow["w"], ow["g"], ow["b"])


def _rope_tables(pos):
    half = ROPE_DIM // 2
    inv = ROPE_THETA ** (-jnp.arange(half, dtype=F32) / half)
    ang = pos.astype(F32)[:, None] * inv[None, :]
    cos, sin = jnp.cos(ang), jnp.sin(ang)
    n = pos.shape[0]
    rest = HEAD_DIM - ROPE_DIM
    zeros8 = jnp.zeros((n, half), F32)
    c = jnp.concatenate([cos, cos, jnp.ones((n, rest), F32)], axis=1)
    s1 = jnp.concatenate([zeros8, sin, jnp.zeros((n, rest), F32)], axis=1)
    s2 = jnp.concatenate([-sin, zeros8, jnp.zeros((n, rest), F32)], axis=1)
    rep = LANES // HEAD_DIM
    return tuple(jnp.tile(a, (1, rep)) for a in (c, s1, s2))


def _head_pad_index():
    h = np.arange(D_ATTN) // HEAD_DIM
    d = np.arange(D_ATTN) % HEAD_DIM
    return h * PAD_HEAD + (h // GROUP) * HEAD_DIM + d


def _prep_layer(l, ln_g, ln_b, ffn_w_in, ffn_w_out, w_in, conv_w, conv_b, conv_ln_g, conv_ln_b,
                lru_conv_w, lru_conv_b, lru_w_gate, lru_b_gate, lru_lambda,
                cmp_pe, cmp_w1, cmp_b1, cmp_w2, cmp_b2, w_out):
    row = lambda v: v.reshape(1, -1).astype(F32)
    ffn = []
    for f, ln_i in ((0, 0), (1, 2)):
        wi = ffn_w_in[l, f]
        wg = wi[:, :D_FF].reshape(D_MODEL, N_FF_CHUNKS, FF_CHUNK).transpose(1, 0, 2).astype(BF16)
        wu = wi[:, D_FF:].reshape(D_MODEL, N_FF_CHUNKS, FF_CHUNK).transpose(1, 0, 2).astype(BF16)
        wo = ffn_w_out[l, f].reshape(N_FF_CHUNKS, FF_CHUNK, D_MODEL).astype(BF16)
        ffn.append({"wg": wg, "wu": wu, "wo": wo, "g": row(ln_g[l, ln_i]), "b": row(ln_b[l, ln_i])})

    wl = w_in[l]
    o_q = 2 * D_CONV + 2 * D_LRU
    o_kv = o_q + D_ATTN
    o_g = o_kv + 3 * KV_ROW
    pad_idx = _head_pad_index()
    wq = jnp.zeros((D_MODEL, Q_PAD), F32).at[:, pad_idx].set(wl[:, o_q:o_kv])
    hh = np.arange(3 * N_HEADS) // 3
    gate_idx = (hh // GROUP) * LANES + (hh % GROUP) * 3 + np.arange(3 * N_HEADS) % 3
    wgt = jnp.zeros((D_MODEL, N_KV_HEADS * LANES), F32).at[:, gate_idx].set(wl[:, o_g:])
    w_all = jnp.concatenate([wl[:, :o_q], wq, wl[:, o_kv:o_g], wgt], axis=1).astype(BF16)
    w_t = jnp.concatenate([w_all[:, C_Q:C_KVC], w_all[:, C_KVS + LANES:C_KVW],
                           w_all[:, C_KVW + LANES:C_GATE], w_all[:, C_GATE:]], axis=1).T

    conv = {"w": jnp.pad(conv_w[l], ((0, CONV_HALO - CONV_WIDTH), (0, 0))),
            "b": row(conv_b[l]), "ln_g": row(conv_ln_g[l]), "ln_b": row(conv_ln_b[l])}

    def blockdiag(w):
        out = jnp.zeros((D_LRU, D_LRU), F32)
        for n in range(LRU_BLOCKS):
            out = out.at[n * LRU_BW:(n + 1) * LRU_BW, n * LRU_BW:(n + 1) * LRU_BW].set(w[n])
        return out.astype(BF16)

    lru = {"cw": jnp.pad(lru_conv_w[l], ((0, LRU_HALO - LRU_CONV_WIDTH), (0, 0))),
           "cb": row(lru_conv_b[l]),
           "wr": blockdiag(lru_w_gate[l, 0]), "wi": blockdiag(lru_w_gate[l, 1]),
           "bg": lru_b_gate[l].astype(F32), "lam": row(lru_lambda[l])}

    pe = cmp_pe[l].reshape(2, 2, CMP_STRIDE, HEAD_DIM)
    pe_rows = jnp.stack([jnp.concatenate([pe[0, j], pe[0, j], pe[1, j], pe[1, j]], axis=-1)
                         for j in range(2)])
    w2e = jnp.zeros((4, CMP_HIDDEN, KV_ROW), F32)
    for g in range(4):
        w2e = w2e.at[g, :, g * HEAD_DIM:(g + 1) * HEAD_DIM].set(cmp_w2[l, g // 2])
    cmp = {"pe": pe_rows.astype(F32),
           "w1": cmp_w1[l].reshape(2, 2, CMP_STRIDE * HEAD_DIM, CMP_HIDDEN).astype(BF16),
           "b1": cmp_b1[l].astype(F32),
           "w2": w2e.astype(BF16),
           "b2": jnp.concatenate([cmp_b2[l, 0], cmp_b2[l, 0], cmp_b2[l, 1], cmp_b2[l, 1]]).reshape(1, -1)}

    wo = w_out[l]
    wo_attn = jnp.zeros((Q_PAD, D_MODEL), F32).at[pad_idx, :].set(wo[D_CONV + D_LRU:])
    out = {"w": jnp.concatenate([wo[:D_CONV + D_LRU], wo_attn], axis=0).astype(BF16),
           "g": row(ln_g[l, 1]), "b": row(ln_b[l, 1])}
    return {"ffn": ffn, "w_all": w_all, "w_t": w_t, "conv": conv, "lru": lru, "cmp": cmp, "out": out}


def _pad_front(a, rows):
    return jnp.pad(a, ((0, 0), (rows - a.shape[1], 0), (0, 0)))


def _kv6(a, lead):
    return a.reshape(lead + (2, N_KV_HEADS, HEAD_DIM))


TM_PROMPT = 512
TC_PROMPT = 512
TL_PROMPT = 256


def _layer_prompt(x, bsz, t, lw, tabs):
    n = bsz * t
    x = _ffn_ln(x, lw["ffn"][0], TM_PROMPT)
    tabs_t = tuple(a.T for a in tabs)
    (u, lx, lg, kvc, kvs, kvw, ks, kw, qc_t, qr_t, vs_t, vw_t, gates_t) = _inproj_t(
        x, lw["w_all"], lw["w_t"], tabs, tabs_t, bsz, t, TM_PROMPT)
    s3 = lambda a: a.reshape(bsz, t, a.shape[-1])
    u3, lx3 = s3(u), s3(lx)
    yc = _conv_group(u3, jnp.zeros((bsz, CONV_HALO, D_CONV), F32), lw["conv"], TC_PROMPT)
    yl, h_last = _lru_group(lx3, s3(lg), jnp.zeros((bsz, LRU_HALO, D_LRU), F32),
                            jnp.zeros((bsz, LRU_HALO, D_LRU), F32), lw["lru"], TL_PROMPT, TL_PROMPT - 1)
    kc, vc_t = _compress_prompt(s3(kvc), lw["cmp"])
    ocmp_t, mneg = _cmp_topk_prompt(qc_t, kc, vc_t)
    ya_t = _selwin_prompt(qr_t, mneg, s3(ks), vs_t, s3(kw), vw_t, ocmp_t, gates_t)
    x = _outproj_ln(x, yc.reshape(n, D_CONV), yl.reshape(n, D_LRU), ya_t, lw["out"], TM_PROMPT)
    x = _ffn_ln(x, lw["ffn"][1], TM_PROMPT)
    kvw3 = s3(kvw)
    state = (u3[:, t - (CONV_WIDTH - 1):], lx3[:, t - (LRU_CONV_WIDTH - 1):], h_last[:, 0],
             _kv6(kvc, (bsz, t)), _kv6(kvs, (bsz, t)), _kv6(kvw3[:, t - min(WINDOW, t):], (bsz, min(WINDOW, t))))
    return x, state


T_PAD = 8


def _layer_sample(x, bsz, t, lw, tabs, conv_buf, lru_buf, lru_h, cache_cmp_t, cache_sel_t, win_t, layer, page_table):
    n = bsz * t
    x = _ffn_ln(x, lw["ffn"][0], n)
    (u, lx, lg, qc, qr, kvc, kvs, kvs_b, kvw, kvw_b, gates) = _inproj(x, lw["w_all"], tabs, n, 1)
    s3 = lambda a: a.reshape(bsz, t, a.shape[-1])
    padt = lambda a: jnp.pad(s3(a), ((0, 0), (0, T_PAD - t), (0, 0)))
    yc = _conv_group(padt(u), _pad_front(conv_buf, CONV_HALO), lw["conv"], T_PAD)[:, :t]
    h0 = jnp.broadcast_to(lru_h[:, None, :], (bsz, LRU_HALO, D_LRU))
    yl, h_all = _lru_group(padt(lx), padt(lg), _pad_front(lru_buf, LRU_HALO), h0, lw["lru"], T_PAD, t - 1)
    yl = yl[:, :t]

    def head_cols(a):
        a = a.reshape(bsz, t, N_HEADS, PAD_HEAD).transpose(0, 2, 1, 3).reshape(bsz, N_HEADS * t, PAD_HEAD)
        return jnp.pad(a, ((0, 0), (0, S_COLS - N_HEADS * t), (0, 0)))

    ocmp_cols, picks = _cmp_topk_sample(page_table, cache_cmp_t, layer, head_cols(qc), lw["cmp"], t)
    pk = picks[:, :, :N_HEADS * t].reshape(bsz, N_SEL, N_KV_HEADS, GROUP, t)[:, :, :, 0, :]
    idx_flat = pk.transpose(0, 3, 2, 1).reshape(-1).astype(I32)

    def head_rows(a, dt):
        a = a.reshape(bsz, N_KV_HEADS, GROUP, t, PAD_HEAD).transpose(0, 3, 1, 2, 4)
        return jnp.pad(a, ((0, 0), (0, 0), (0, 0), (0, HROWS - GROUP), (0, 0))).astype(dt)

    q_rows = head_rows(qr.reshape(bsz, t, N_HEADS, PAD_HEAD).transpose(0, 2, 1, 3).reshape(bsz, N_HEADS * t, PAD_HEAD), BF16)
    ocmp_rows = head_rows(ocmp_cols[:, :N_HEADS * t], F32)
    g3 = gates.reshape(bsz, t, N_KV_HEADS, LANES)[..., :3 * GROUP].reshape(bsz, t, N_KV_HEADS, GROUP, 3)
    gate_rows = jnp.pad(g3, ((0, 0), (0, 0), (0, 0), (0, HROWS - GROUP), (0, LANES - 3)))
    pos_last = lambda a: s3(a).transpose(0, 2, 1)
    pad_cols = lambda a: jnp.pad(a, ((0, 0), (0, 0), (0, LANES - t)))
    kvs_t, kvw_t = pos_last(kvs), pos_last(kvw)
    ya_rows = _selwin_sample(idx_flat, page_table, cache_sel_t, win_t, layer, q_rows, pad_cols(kvs_t),
                             pad_cols(kvw_t), ocmp_rows, gate_rows)
    ya = ya_rows[:, :, :, :GROUP].reshape(bsz, t, Q_PAD)

    x = _outproj_ln(x, yc.reshape(n, D_CONV), yl.reshape(n, D_LRU), ya.reshape(n, Q_PAD), lw["out"], n)
    x = _ffn_ln(x, lw["ffn"][1], n)
    new_conv = jnp.concatenate([conv_buf, s3(u)], axis=1)[:, t:]
    new_lru = jnp.concatenate([lru_buf, s3(lx)], axis=1)[:, t:]
    win_all = jnp.concatenate([win_t[layer], kvw_t], axis=-1)
    n_win = min(WINDOW, win_all.shape[-1])
    new_win = win_all[..., win_all.shape[-1] - n_win:]
    new_win = new_win.reshape(bsz, 2, N_KV_HEADS, HEAD_DIM, n_win).transpose(0, 4, 1, 2, 3)
    state = (new_conv, new_lru, h_all[:, 0], _kv6(kvc, (bsz, t)), _kv6(kvs, (bsz, t)), new_win)
    return x, state


def kernel(x_prompt, x_sample, state_conv, state_lru_conv, state_lru_h, cache_cmp_kv, cache_sel_kv, cache_win_kv,
           page_table, ln_g, ln_b, ffn_w_in, ffn_w_out, w_in, conv_w, conv_b, conv_ln_g, conv_ln_b,
           lru_conv_w, lru_conv_b, lru_w_gate, lru_b_gate, lru_lambda, cmp_pe, cmp_w1, cmp_b1, cmp_w2, cmp_b2, w_out):
    bp, tp, _ = x_prompt.shape
    bs, ts, _ = x_sample.shape
    depth = ln_g.shape[0]
    past = page_table.shape[1] * PAGE_SIZE
    assert past == PAST_LEN and past % SEL_BLOCK == 0
    tabs_p = _rope_tables(jnp.arange(tp))
    tabs_s = _rope_tables(jnp.tile(past + jnp.arange(ts), bs))
    xp = x_prompt.reshape(bp * tp, D_MODEL)
    xs = x_sample.reshape(bs * ts, D_MODEL)
    n_pool = cache_cmp_kv.shape[1]
    pos_last = lambda c: jnp.transpose(c, (0, 1, 3, 4, 5, 2)).reshape(c.shape[:2] + (KV_ROW, c.shape[2]))
    cache_cmp_t, cache_sel_t, win_t = pos_last(cache_cmp_kv), pos_last(cache_sel_kv), pos_last(cache_win_kv)
    st_p, st_s = [], []
    for l in range(depth):
        lw = _prep_layer(l, ln_g, ln_b, ffn_w_in, ffn_w_out, w_in, conv_w, conv_b, conv_ln_g, conv_ln_b,
                         lru_conv_w, lru_conv_b, lru_w_gate, lru_b_gate, lru_lambda,
                         cmp_pe, cmp_w1, cmp_b1, cmp_w2, cmp_b2, w_out)
        xp, sp = _layer_prompt(xp, bp, tp, lw, tabs_p)
        xs, ss = _layer_sample(
            xs, bs, ts, lw, tabs_s, state_conv[l], state_lru_conv[l], state_lru_h[l],
            cache_cmp_t, cache_sel_t, win_t, l, page_table)
        st_p.append(sp)
        st_s.append(ss)
    outs = [xp.reshape(bp, tp, D_MODEL), xs.reshape(bs, ts, D_MODEL)]
    for k in range(6):
        outs.append(jnp.stack([s[k] for s in st_p]))
        outs.append(jnp.stack([s[k] for s in st_s]))
    return tuple(outs)
```

```python
import functools

import numpy as np
import jax
import jax.numpy as jnp
from jax import lax
from jax.experimental import pallas as pl
from jax.experimental.pallas import tpu as pltpu

F32 = jnp.float32
BF16 = jnp.bfloat16
I32 = jnp.int32

D_MODEL = 1024
DEPTH = 2
PAST_LEN = 16384
PAGE_SIZE = 128
D_CONV = 256
CONV_WIDTH = 31
D_LRU = 256
LRU_BLOCKS = 4
LRU_BW = D_LRU // LRU_BLOCKS
LRU_CONV_WIDTH = 4
LRU_C = 8.0
D_ATTN = 512
N_HEADS = 8
HEAD_DIM = 64
N_KV_HEADS = 2
GROUP = N_HEADS // N_KV_HEADS
KV_ROW = 2 * N_KV_HEADS * HEAD_DIM
ROPE_DIM = 16
ROPE_THETA = 500000.0
CMP_BLOCK = 32
CMP_STRIDE = 16
CMP_HIDDEN = 256
SEL_BLOCK = 64
N_SEL = 16
WINDOW = 512
Q_BLOCK = 128
D_FF = 2816
ALPHA = (2 * DEPTH) ** 0.25
LN_EPS = 1e-5
SCALE = HEAD_DIM ** -0.5

LANES = 128
VMEM_LIMIT = 56 * 1024 * 1024
FF_CHUNK = 256
N_FF_CHUNKS = D_FF // FF_CHUNK
PAD_HEAD = 128
Q_PAD = N_HEADS * PAD_HEAD
KV_GROUP_LANES = GROUP * PAD_HEAD
MASK_NEG = -(2.0 ** 60)
SEL_TILE = 512
WIN_KEYS = WINDOW + Q_BLOCK

C_GLU = 0
C_LRUX = 512
C_LRUG = 768
C_Q = 1024
C_KVC = C_Q + Q_PAD
C_KVS = C_KVC + KV_ROW
C_KVW = C_KVS + KV_ROW
C_GATE = C_KVW + KV_ROW
N_PROJ = C_GATE + N_KV_HEADS * LANES


def _cparams(sem):
    return pltpu.CompilerParams(dimension_semantics=sem, vmem_limit_bytes=VMEM_LIMIT)


def _const_spec(shape):
    nd = len(shape)
    return pl.BlockSpec(shape, lambda *_: (0,) * nd, pipeline_mode=pl.Buffered(1))


def _layernorm(y, g, b):
    mu = jnp.mean(y, axis=-1, keepdims=True)
    d = y - mu
    var = jnp.mean(d * d, axis=-1, keepdims=True)
    return d * lax.rsqrt(var + LN_EPS) * g + b


def _dot(a, b):
    return jnp.dot(a, b, preferred_element_type=F32)


def _dot_nt(a, b):
    return lax.dot_general(a, b, (((1,), (1,)), ((), ())), preferred_element_type=F32)


def _dot_tn(a, b):
    return lax.dot_general(a, b, (((0,), (0,)), ((), ())), preferred_element_type=F32)


def _ffn_ln_kernel(x_ref, wg_ref, wu_ref, wo_ref, g_ref, b_ref, o_ref):
    x = x_ref[...]
    xb = x.astype(BF16)
    acc = jnp.zeros(x.shape, F32)
    for c in range(N_FF_CHUNKS):
        gate = _dot(xb, wg_ref[c])
        up = _dot(xb, wu_ref[c])
        h = (gate * jax.nn.sigmoid(gate)) * up
        acc = acc + _dot(h.astype(BF16), wo_ref[c])
    y = ALPHA * x + 0.5 * acc
    o_ref[...] = _layernorm(y, g_ref[...], b_ref[...])


def _ffn_ln(x, fw, tm):
    n = x.shape[0]
    return pl.pallas_call(
        _ffn_ln_kernel,
        grid=(n // tm,),
        in_specs=[
            pl.BlockSpec((tm, D_MODEL), lambda i: (i, 0)),
            _const_spec((N_FF_CHUNKS, D_MODEL, FF_CHUNK)),
            _const_spec((N_FF_CHUNKS, D_MODEL, FF_CHUNK)),
            _const_spec((N_FF_CHUNKS, FF_CHUNK, D_MODEL)),
            _const_spec((1, D_MODEL)),
            _const_spec((1, D_MODEL)),
        ],
        out_specs=pl.BlockSpec((tm, D_MODEL), lambda i: (i, 0)),
        out_shape=jax.ShapeDtypeStruct((n, D_MODEL), F32),
        compiler_params=_cparams(("parallel",)),
        name="ffn_ln",
    )(x, fw["wg"], fw["wu"], fw["wo"], fw["g"], fw["b"])


def _rope(v, cos, s1, s2):
    return v * cos + pltpu.roll(v, 8, 1) * s1 + pltpu.roll(v, LANES - 8, 1) * s2


def _inproj_kernel(x_ref, w_ref, cos_ref, s1_ref, s2_ref,
                   u_ref, lx_ref, lg_ref, qc_ref, qr_ref,
                   kvc_ref, kvs_ref, kvsb_ref, kvw_ref, kvwb_ref, gt_ref):
    xb = x_ref[...].astype(BF16)

    def mm(lo, hi):
        return _dot(xb, w_ref[:, lo:hi])

    glu = mm(C_GLU, C_GLU + 2 * D_CONV)
    u_ref[...] = glu[:, :D_CONV] * jax.nn.sigmoid(glu[:, D_CONV:])
    lx_ref[...] = mm(C_LRUX, C_LRUX + D_LRU)
    lg_ref[...] = mm(C_LRUG, C_LRUG + D_LRU)
    cos = cos_ref[...]
    s1 = s1_ref[...]
    s2 = s2_ref[...]
    for h in range(N_HEADS):
        qh = mm(C_Q + h * PAD_HEAD, C_Q + (h + 1) * PAD_HEAD) * SCALE
        qc_ref[:, h * PAD_HEAD:(h + 1) * PAD_HEAD] = qh.astype(BF16)
        qr_ref[:, h * PAD_HEAD:(h + 1) * PAD_HEAD] = _rope(qh, cos, s1, s2).astype(BF16)
    kvc_ref[...] = mm(C_KVC, C_KVC + KV_ROW)
    for c0, f_ref, b_ref in ((C_KVS, kvs_ref, kvsb_ref), (C_KVW, kvw_ref, kvwb_ref)):
        kv = mm(c0, c0 + KV_ROW)
        k = _rope(kv[:, :LANES], cos, s1, s2)
        v = kv[:, LANES:]
        f_ref[:, 0:LANES] = k
        f_ref[:, LANES:KV_ROW] = v
        b_ref[:, 0:LANES] = k.astype(BF16)
        b_ref[:, LANES:KV_ROW] = v.astype(BF16)
    gt_ref[...] = jax.nn.sigmoid(mm(C_GATE, N_PROJ))


def _rope_t(v, cos, s1, s2):
    return v * cos + pltpu.roll(v, 8, 0) * s1 + pltpu.roll(v, PAD_HEAD - 8, 0) * s2


R_Q = 0
R_VS = Q_PAD
R_VW = R_VS + LANES
R_GATE = R_VW + LANES
N_PROJ_T = R_GATE + N_KV_HEADS * LANES


LOG2E = 1.4426950408889634
SUM_ROWS = (HEAD_DIM, 0)


def _inproj_t_kernel(x_ref, w_ref, wt_ref, cos_ref, s1_ref, s2_ref, cos_t_ref, s1_t_ref, s2_t_ref, blk_ref,
                     u_ref, lx_ref, lg_ref, kvc_ref, kvs_ref, kvw_ref, ks_ref, kw_ref,
                     qc_ref, qr_ref, vs_ref, vw_ref, gt_ref):
    xb = x_ref[...].astype(BF16)
    tm = xb.shape[0]

    def mm(lo, hi):
        return _dot(xb, w_ref[:, lo:hi])

    def mm_t(lo, hi):
        return _dot_nt(wt_ref[lo:hi, :], xb)

    glu = mm(C_GLU, C_GLU + 2 * D_CONV)
    u_ref[...] = glu[:, :D_CONV] * jax.nn.sigmoid(glu[:, D_CONV:])
    lx_ref[...] = mm(C_LRUX, C_LRUX + D_LRU)
    lg_ref[...] = mm(C_LRUG, C_LRUG + D_LRU)
    kvc_ref[...] = mm(C_KVC, C_KVC + KV_ROW)
    cos = cos_ref[...]
    s1 = s1_ref[...]
    s2 = s2_ref[...]
    for c0, f_ref, b_ref in ((C_KVS, kvs_ref, ks_ref), (C_KVW, kvw_ref, kw_ref)):
        kv = mm(c0, c0 + KV_ROW)
        k = _rope(kv[:, :LANES], cos, s1, s2)
        f_ref[:, 0:LANES] = k
        f_ref[:, LANES:KV_ROW] = kv[:, LANES:]
        b_ref[:, 0:LANES] = k.astype(BF16)
    ks_ref[:, LANES:] = blk_ref[...]
    cos_t = cos_t_ref[...]
    s1_t = s1_t_ref[...]
    s2_t = s2_t_ref[...]
    for h in range(N_HEADS):
        rows = slice(h * PAD_HEAD, (h + 1) * PAD_HEAD)
        qh = mm_t(R_Q + h * PAD_HEAD, R_Q + (h + 1) * PAD_HEAD)
        qc_ref[rows, :] = (qh * SCALE).astype(BF16)
        qr_ref[rows, :] = (_rope_t(qh, cos_t, s1_t, s2_t) * (SCALE * LOG2E)).astype(BF16)
    row = lax.broadcasted_iota(I32, (LANES, tm), 0)
    for r0, o_ref in ((R_VS, vs_ref), (R_VW, vw_ref)):
        vt = mm_t(r0, r0 + LANES)
        per_head = (jnp.where(row < HEAD_DIM, vt, jnp.where(row == SUM_ROWS[0], 1.0, 0.0)),
                    jnp.where(row >= HEAD_DIM, vt, jnp.where(row == SUM_ROWS[1], 1.0, 0.0)))
        for kv, vk in enumerate(per_head):
            vk = vk.astype(BF16)
            for c in range(tm // LANES):
                o_ref[kv, c] = vk[:, c * LANES:(c + 1) * LANES]
    gt_ref[...] = jax.nn.sigmoid(mm_t(R_GATE, N_PROJ_T))


def _inproj_t(x, w_all, w_t, rope_tabs, rope_tabs_t, bsz, t, tm):
    n = x.shape[0]
    tps = t // tm
    tok = lambda w: pl.BlockSpec((tm, w), lambda i: (i, 0))
    tab = pl.BlockSpec((tm, LANES), lambda i: (i % tps, 0))
    tab_t = pl.BlockSpec((PAD_HEAD, tm), lambda i: (0, i % tps))
    feat_t = lambda r: pl.BlockSpec((None, r, tm), lambda i: (i // tps, 0, i % tps))
    vt_spec = pl.BlockSpec((None, N_KV_HEADS, tm // LANES, LANES, LANES), lambda i: (i // tps, 0, i % tps, 0, 0))
    n_sel = t // SEL_BLOCK
    row_outs = [(D_CONV, F32), (D_LRU, F32), (D_LRU, F32), (KV_ROW, F32), (KV_ROW, F32), (KV_ROW, F32),
                (LANES + n_sel, BF16), (LANES, BF16)]
    vt_shape = jax.ShapeDtypeStruct((bsz, N_KV_HEADS, t // LANES, LANES, LANES), BF16)
    block_id = (jnp.arange(t)[:, None] // SEL_BLOCK == jnp.arange(n_sel)[None, :]).astype(BF16)
    return pl.pallas_call(
        _inproj_t_kernel,
        grid=(n // tm,),
        in_specs=[tok(D_MODEL), _const_spec((D_MODEL, N_PROJ)), _const_spec((N_PROJ_T, D_MODEL)),
                  tab, tab, tab, tab_t, tab_t, tab_t,
                  pl.BlockSpec((tm, n_sel), lambda i: (i % tps, 0))],
        out_specs=[tok(w) for w, _ in row_outs]
        + [feat_t(Q_PAD), feat_t(Q_PAD), vt_spec, vt_spec, feat_t(N_KV_HEADS * LANES)],
        out_shape=[jax.ShapeDtypeStruct((n, w), dt) for w, dt in row_outs]
        + [jax.ShapeDtypeStruct((bsz, Q_PAD, t), BF16), jax.ShapeDtypeStruct((bsz, Q_PAD, t), BF16),
           vt_shape, vt_shape, jax.ShapeDtypeStruct((bsz, N_KV_HEADS * LANES, t), F32)],
        compiler_params=_cparams(("parallel",)),
        name="inproj_t",
    )(x, w_all, w_t, *rope_tabs, *rope_tabs_t, block_id)


def _inproj(x, w_all, rope_tabs, tm, tiles_per_seq):
    n = x.shape[0]
    cos, s1, s2 = rope_tabs
    tok = lambda w: pl.BlockSpec((tm, w), lambda i: (i, 0))
    tab = pl.BlockSpec((tm, LANES), lambda i: (i % tiles_per_seq, 0))
    outs = [
        (D_CONV, F32), (D_LRU, F32), (D_LRU, F32), (Q_PAD, BF16), (Q_PAD, BF16),
        (KV_ROW, F32), (KV_ROW, F32), (KV_ROW, BF16), (KV_ROW, F32), (KV_ROW, BF16),
        (N_KV_HEADS * LANES, F32),
    ]
    return pl.pallas_call(
        _inproj_kernel,
        grid=(n // tm,),
        in_specs=[tok(D_MODEL), _const_spec((D_MODEL, N_PROJ)), tab, tab, tab],
        out_specs=[tok(w) for w, _ in outs],
        out_shape=[jax.ShapeDtypeStruct((n, w), dt) for w, dt in outs],
        compiler_params=_cparams(("parallel",)),
        name="inproj",
    )(x, w_all, cos, s1, s2)


CONV_HALO = 32
CONV_ROWS = 64


def _conv_kernel(u_ref, buf_ref, w_ref, b_ref, g_ref, bb_ref, o_ref, ext_ref):
    t = pl.program_id(1)
    tc = u_ref.shape[0]

    @pl.when(t == 0)
    def _():
        ext_ref[0:CONV_HALO, :] = buf_ref[...]

    ext_ref[CONV_HALO:CONV_HALO + tc, :] = u_ref[...]
    off = CONV_HALO - (CONV_WIDTH - 1)
    rows = min(CONV_ROWS, tc)
    for r0 in range(0, tc, rows):
        acc = jnp.zeros((rows, D_CONV), F32)
        for k in range(CONV_WIDTH):
            acc = acc + ext_ref[r0 + off + k:r0 + off + k + rows, :] * w_ref[k:k + 1, :]
        y = acc + b_ref[...]
        y = _layernorm(y, g_ref[...], bb_ref[...])
        o_ref[r0:r0 + rows, :] = (y * jax.nn.sigmoid(y)).astype(BF16)
    ext_ref[0:CONV_HALO, :] = ext_ref[tc:tc + CONV_HALO, :]


def _conv_group(u, buf, cw, tc):
    b, t, _ = u.shape
    return pl.pallas_call(
        _conv_kernel,
        grid=(b, t // tc),
        in_specs=[
            pl.BlockSpec((None, tc, D_CONV), lambda i, j: (i, j, 0)),
            pl.BlockSpec((None, CONV_HALO, D_CONV), lambda i, j: (i, 0, 0)),
            _const_spec((CONV_HALO, D_CONV)),
            _const_spec((1, D_CONV)), _const_spec((1, D_CONV)), _const_spec((1, D_CONV)),
        ],
        out_specs=pl.BlockSpec((None, tc, D_CONV), lambda i, j: (i, j, 0)),
        out_shape=jax.ShapeDtypeStruct((b, t, D_CONV), BF16),
        scratch_shapes=[pltpu.VMEM((tc + CONV_HALO, D_CONV), F32)],
        compiler_params=_cparams(("parallel", "arbitrary")),
        name="conv_group",
    )(u, buf, cw["w"], cw["b"], cw["ln_g"], cw["ln_b"])


LRU_HALO = 8


def _lru_kernel(last_row, x_ref, gate_ref, buf_ref, h0_ref, cw_ref, cb_ref, wr_ref, wi_ref,
                bg_ref, lam_ref, y_ref, hl_ref, ext_ref, hc_ref):
    t = pl.program_id(1)
    tl = x_ref.shape[0]

    @pl.when(t == 0)
    def _():
        ext_ref[0:LRU_HALO, :] = buf_ref[...]
        hc_ref[...] = h0_ref[...]

    ext_ref[LRU_HALO:LRU_HALO + tl, :] = x_ref[...]
    off = LRU_HALO - (LRU_CONV_WIDTH - 1)
    xl = jnp.zeros((tl, D_LRU), F32)
    for k in range(LRU_CONV_WIDTH):
        xl = xl + ext_ref[off + k:off + k + tl, :] * cw_ref[k:k + 1, :]
    xl = xl + cb_ref[...]
    ext_ref[0:LRU_HALO, :] = ext_ref[tl:tl + LRU_HALO, :]

    xb = xl.astype(BF16)
    r_gate = jax.nn.sigmoid(_dot(xb, wr_ref[...]) + bg_ref[0:1, :])
    i_gate = jax.nn.sigmoid(_dot(xb, wi_ref[...]) + bg_ref[1:2, :])
    log_a = LRU_C * r_gate * jax.nn.log_sigmoid(lam_ref[...])
    a = jnp.exp(log_a)
    bv = jnp.sqrt(-jnp.tanh(log_a) * (a * a + 1.0)) * (i_gate * xl)

    row = lax.broadcasted_iota(I32, (tl, D_LRU), 0)
    s = 1
    while s < tl:
        keep = row >= s
        a_sh = jnp.where(keep, pltpu.roll(a, s, 0), 1.0)
        b_sh = jnp.where(keep, pltpu.roll(bv, s, 0), 0.0)
        bv = a * b_sh + bv
        a = a * a_sh
        s *= 2
    h = a * hc_ref[0:1, :] + bv
    hc_ref[...] = jnp.broadcast_to(h[last_row:last_row + 1, :], hc_ref.shape)
    hl_ref[...] = hc_ref[...]
    y_ref[...] = (h * jax.nn.gelu(gate_ref[...])).astype(BF16)


def _lru_group(x, gate, buf, h0, lw, tl, last_row):
    b, t, _ = x.shape
    assert last_row == tl - 1 or t == tl
    seq = pl.BlockSpec((None, tl, D_LRU), lambda i, j: (i, j, 0))
    per_b = pl.BlockSpec((None, LRU_HALO, D_LRU), lambda i, j: (i, 0, 0))
    return pl.pallas_call(
        functools.partial(_lru_kernel, last_row),
        grid=(b, t // tl),
        in_specs=[seq, seq, per_b, per_b,
                  _const_spec((LRU_HALO, D_LRU)), _const_spec((1, D_LRU)),
                  _const_spec((D_LRU, D_LRU)), _const_spec((D_LRU, D_LRU)),
                  _const_spec((2, D_LRU)), _const_spec((1, D_LRU))],
        out_specs=[seq, per_b],
        out_shape=[jax.ShapeDtypeStruct((b, t, D_LRU), BF16),
                   jax.ShapeDtypeStruct((b, LRU_HALO, D_LRU), F32)],
        scratch_shapes=[pltpu.VMEM((tl + LRU_HALO, D_LRU), F32),
                        pltpu.VMEM((LRU_HALO, D_LRU), F32)],
        compiler_params=_cparams(("parallel", "arbitrary")),
        name="lru_group",
    )(x, gate, buf, h0, lw["cw"], lw["cb"], lw["wr"], lw["wi"], lw["bg"], lw["lam"])


def _compress_rows(row_refs, pe_ref, w1_ref, b1_ref, w2_ref, b2_ref, carry_ref):
    lo = lax.broadcasted_iota(I32, (1, LANES), 1) < HEAD_DIM
    xs = [[] for _ in range(4)]
    for halves in row_refs:
        n = halves[0].shape[0] // CMP_STRIDE
        cols = [[] for _ in range(4)]
        for rp in range(CMP_STRIDE // 2):
            for half, ref in enumerate(halves):
                pa = ref[pl.ds(2 * rp, n, stride=CMP_STRIDE), :]
                pb = ref[pl.ds(2 * rp + 1, n, stride=CMP_STRIDE), :]
                cols[2 * half].append(jnp.where(lo, pa, pltpu.roll(pb, HEAD_DIM, 1)))
                cols[2 * half + 1].append(jnp.where(lo, pltpu.roll(pa, HEAD_DIM, 1), pb))
        for g in range(4):
            xs[g].append(jnp.concatenate(cols[g], axis=1))
    out = None
    for g in range(4):
        sidx = g // 2
        x = jnp.concatenate(xs[g], axis=0)
        p0 = _dot((x + pe_ref[0, g:g + 1, :]).astype(BF16), w1_ref[sidx, 0])
        p1 = _dot((x + pe_ref[1, g:g + 1, :]).astype(BF16), w1_ref[sidx, 1])
        n_tot = p0.shape[0]
        row = lax.broadcasted_iota(I32, p0.shape, 0)
        p0s = jnp.where(row == 0, carry_ref[g, 0:1, :], pltpu.roll(p0, 1, 0))
        carry_ref[g, 0:1, :] = p0[n_tot - 1:n_tot, :]
        h = (b1_ref[sidx:sidx + 1, :] + p0s) + p1
        part = _dot(jax.nn.gelu(h).astype(BF16), w2_ref[g])
        out = part if out is None else out + part
    return out + b2_ref[...]


def _cmp_weight_specs():
    return [
        _const_spec((2, 4, CMP_STRIDE * HEAD_DIM)),
        _const_spec((2, 2, CMP_STRIDE * HEAD_DIM, CMP_HIDDEN)),
        _const_spec((2, CMP_HIDDEN)),
        _const_spec((4, CMP_HIDDEN, KV_ROW)),
        _const_spec((1, KV_ROW)),
    ]


def _cmp_weight_args(cw):
    return (cw["pe"], cw["w1"], cw["b1"], cw["w2"], cw["b2"])


CMP_TILE_ROWS = 2048


def _compress_prompt_kernel(k_ref, v_ref, pe_ref, w1_ref, b1_ref, w2_ref, b2_ref, kc_ref, vct_ref, carry_ref):
    @pl.when(pl.program_id(1) == 0)
    def _():
        carry_ref[...] = jnp.zeros(carry_ref.shape, F32)

    out = _compress_rows([(k_ref, v_ref)], pe_ref, w1_ref, b1_ref, w2_ref, b2_ref, carry_ref)
    kc_ref[...] = out[:, 0:LANES].astype(BF16)
    vct_ref[...] = out[:, LANES:KV_ROW].T.astype(BF16)


def _compress_prompt(kvc, cw):
    b, t, _ = kvc.shape
    n_e = CMP_TILE_ROWS // CMP_STRIDE
    return pl.pallas_call(
        _compress_prompt_kernel,
        grid=(b, t // CMP_TILE_ROWS),
        in_specs=[pl.BlockSpec((None, CMP_TILE_ROWS, LANES), lambda i, j: (i, j, 0)),
                  pl.BlockSpec((None, CMP_TILE_ROWS, LANES), lambda i, j: (i, j, 1))]
        + _cmp_weight_specs(),
        out_specs=[pl.BlockSpec((None, n_e, LANES), lambda i, j: (i, j, 0)),
                   pl.BlockSpec((None, LANES, n_e), lambda i, j: (i, 0, j))],
        out_shape=[jax.ShapeDtypeStruct((b, t // CMP_STRIDE, LANES), BF16),
                   jax.ShapeDtypeStruct((b, LANES, t // CMP_STRIDE), BF16)],
        scratch_shapes=[pltpu.VMEM((4, 8, CMP_HIDDEN), F32)],
        compiler_params=_cparams(("parallel", "arbitrary")),
        name="compress_prompt",
    )(kvc, kvc, *_cmp_weight_args(cw))


def _selection_scores(pk_ref, n_sel):
    ratio = SEL_BLOCK // CMP_STRIDE
    slc = pk_ref[pl.ds(0, n_sel, stride=ratio), :]
    for o in range(1, ratio):
        slc = slc + 2.0 * pk_ref[pl.ds(o, n_sel, stride=ratio), :]
    return slc + pk_ref[pl.ds(ratio, n_sel, stride=ratio), :]


def _topk_rounds(score, n_rows):
    j = lax.broadcasted_iota(I32, score.shape, 0).astype(F32)
    sel = jnp.zeros(score.shape, F32)
    picks = []
    for _ in range(N_SEL):
        cm = jnp.max(score, axis=0, keepdims=True)
        mi = jnp.min(jnp.where(score == cm, j, float(n_rows)), axis=0, keepdims=True)
        hit = j == mi
        sel = jnp.where(hit, 1.0, sel)
        score = jnp.where(hit, -jnp.inf, score)
        picks.append(mi)
    return sel, picks


def _masked_softmax_rows(s_t, valid):
    s_t = jnp.where(valid, s_t, -jnp.inf)
    m = jnp.max(s_t, axis=0, keepdims=True)
    m = jnp.where(m > -jnp.inf, m, 0.0)
    e = jnp.exp(s_t - m)
    d = jnp.sum(e, axis=0, keepdims=True)
    return e / jnp.where(d > 0, d, 1.0)


def _cmp_topk_prompt_kernel(qc_ref, kc_ref, vct_ref, ocmp_ref, mneg_ref, pk_ref):
    i = pl.program_id(2)
    n_e = kc_ref.shape[0]
    n_sel = n_e // (SEL_BLOCK // CMP_STRIDE)
    kc = kc_ref[...]
    vct = vct_ref[...]
    e_idx = lax.broadcasted_iota(I32, (n_e, Q_BLOCK), 0)
    qpos = i * Q_BLOCK + lax.broadcasted_iota(I32, (n_e, Q_BLOCK), 1)
    valid = (e_idx >= 1) & (CMP_STRIDE * e_idx + (CMP_STRIDE - 1) <= qpos)
    pkv = jnp.zeros((n_e, Q_BLOCK), F32)
    for h in range(GROUP):
        rows = slice(h * PAD_HEAD, (h + 1) * PAD_HEAD)
        p = _masked_softmax_rows(_dot(kc, qc_ref[rows, :]), valid)
        pkv = pkv + p
        ocmp_ref[rows, :] = _dot(vct, p.astype(BF16))
    pk_ref[0:n_e, :] = pkv
    pk_ref[n_e:n_e + 8, :] = jnp.zeros((8, Q_BLOCK), F32)
    slc = _selection_scores(pk_ref, n_sel)

    j = lax.broadcasted_iota(I32, (n_sel, Q_BLOCK), 0)
    qp = i * Q_BLOCK + lax.broadcasted_iota(I32, (n_sel, Q_BLOCK), 1)
    qblk = jnp.right_shift(qp, 6)
    forced = (j == 0) | (j == qblk) | (j == qblk - 1)
    score = jnp.where(forced, jnp.inf, jnp.where(j * SEL_BLOCK <= qp, slc, -jnp.inf))
    sel, _ = _topk_rounds(score, n_sel)
    mneg_ref[...] = jnp.where(sel > 0, 0.0, MASK_NEG).astype(BF16)


def _cmp_topk_prompt(qc_t, kc, vc_t):
    b, _, t = qc_t.shape
    n_e = kc.shape[1]
    n_sel = t // SEL_BLOCK
    qblk = pl.BlockSpec((None, KV_GROUP_LANES, Q_BLOCK), lambda bi, k, i: (bi, k, i))
    return pl.pallas_call(
        _cmp_topk_prompt_kernel,
        grid=(b, N_KV_HEADS, t // Q_BLOCK),
        in_specs=[
            qblk,
            pl.BlockSpec((None, n_e, LANES), lambda bi, k, i: (bi, 0, 0)),
            pl.BlockSpec((None, LANES, n_e), lambda bi, k, i: (bi, 0, 0)),
        ],
        out_specs=[
            qblk,
            pl.BlockSpec((None, None, n_sel, Q_BLOCK), lambda bi, k, i: (bi, k, 0, i)),
        ],
        out_shape=[jax.ShapeDtypeStruct((b, Q_PAD, t), F32),
                   jax.ShapeDtypeStruct((b, N_KV_HEADS, n_sel, t), BF16)],
        scratch_shapes=[pltpu.VMEM((n_e + 8, Q_BLOCK), F32)],
        compiler_params=_cparams(("parallel", "parallel", "arbitrary")),
        name="cmp_topk_prompt",
    )(qc_t, kc, vc_t)


def _selwin_prompt_kernel(qr_ref, mneg_ref, ks_ref, vs_ref, kw_ref, vw_ref, ocmp_ref, gt_ref, o_ref,
                          qa_ref, sa_ref, sb_ref, pa_ref, pb_ref):
    i = pl.program_id(1)
    n_sel = mneg_ref.shape[1]
    cols = N_HEADS * Q_BLOCK
    kv_cols = GROUP * Q_BLOCK
    for h in range(N_HEADS):
        c = slice(h * Q_BLOCK, (h + 1) * Q_BLOCK)
        qa_ref[0:PAD_HEAD, c] = qr_ref[h * PAD_HEAD:(h + 1) * PAD_HEAD, :]
        qa_ref[PAD_HEAD:PAD_HEAD + n_sel, c] = mneg_ref[h // GROUP]
    qa = qa_ref[...]
    q0 = i * Q_BLOCK
    qpos = q0 + (lax.broadcasted_iota(I32, (1, cols), 1) & (Q_BLOCK - 1))

    tile = SEL_TILE
    key_col = lax.broadcasted_iota(I32, (tile, 1), 0)

    def scores(t):
        return _dot(ks_ref[pl.ds(pl.multiple_of(t * tile, tile), tile), :], qa)

    def weighted_values(v_ref, first_tile, n_tiles, p):
        outs = []
        for kv in range(N_KV_HEADS):
            vk = v_ref[kv, pl.ds(first_tile, n_tiles)]
            vt = jnp.concatenate([vk[c] for c in range(n_tiles)], axis=1)
            outs.append(_dot(vt, p[:, kv * kv_cols:(kv + 1) * kv_cols]))
        return jnp.concatenate(outs, axis=1)

    def normalise(acc):
        sums = [acc[SUM_ROWS[kv]:SUM_ROWS[kv] + 1, kv * kv_cols:(kv + 1) * kv_cols] for kv in range(N_KV_HEADS)]
        return acc / jnp.concatenate(sums, axis=1)

    sub = tile // LANES

    def stage(t, cur, nxt, carry, masked, prefetch):
        m, acc, alpha_prev = carry
        if prefetch:
            s_refs[nxt][...] = scores(t + 1)
        acc = alpha_prev * acc + weighted_values(vs_ref, jnp.maximum(t - 1, 0) * sub, sub, p_refs[nxt][...])
        s = s_refs[cur][...]
        if masked:
            s = jnp.where(t * tile + key_col <= qpos, s, MASK_NEG)
        m_new = jnp.maximum(m, jnp.max(s, axis=0, keepdims=True))
        alpha = jnp.exp2(m - m_new)
        p_refs[cur][...] = jnp.exp2(s - m_new).astype(BF16)
        return m_new, acc, alpha

    s_refs = (sa_ref, sb_ref)
    p_refs = (pa_ref, pb_ref)
    sa_ref[...] = scores(0)
    pb_ref[...] = jnp.zeros(pb_ref.shape, BF16)
    init = (jnp.full((1, cols), MASK_NEG, F32), jnp.zeros((LANES, cols), F32), jnp.ones((1, cols), F32))

    def pair(u, carry):
        carry = stage(2 * u, 0, 1, carry, False, True)
        return stage(2 * u + 1, 1, 0, carry, False, True)

    u_diag = q0 // (2 * tile)
    carry = lax.fori_loop(0, u_diag, pair, init)
    carry = stage(2 * u_diag, 0, 1, carry, True, True)
    _, acc_sel, alpha_last = stage(2 * u_diag + 1, 1, 0, carry, True, False)
    acc_sel = alpha_last * acc_sel + weighted_values(vs_ref, (2 * u_diag + 1) * sub, sub, pb_ref[...])
    o_sel = normalise(acc_sel)

    start = pl.multiple_of(jnp.maximum(q0 + Q_BLOCK - WIN_KEYS, 0), Q_BLOCK)
    s = _dot(kw_ref[pl.ds(start, WIN_KEYS), :], qa[0:PAD_HEAD, :])
    dpos = qpos - (start + lax.broadcasted_iota(I32, (WIN_KEYS, 1), 0))
    s = jnp.where((dpos >= 0) & (dpos <= WINDOW), s, -jnp.inf)
    p = jnp.exp2(s - jnp.max(s, axis=0, keepdims=True))
    o_win = normalise(weighted_values(vw_ref, start // LANES, WIN_KEYS // LANES, p.astype(BF16)))

    for h in range(N_HEADS):
        c = slice(h * Q_BLOCK, (h + 1) * Q_BLOCK)
        rows = slice(h * PAD_HEAD, (h + 1) * PAD_HEAD)
        g0 = (h // GROUP) * LANES + 3 * (h % GROUP)
        o = (gt_ref[g0:g0 + 1, :] * ocmp_ref[rows, :] + gt_ref[g0 + 1:g0 + 2, :] * o_sel[:, c]
             + gt_ref[g0 + 2:g0 + 3, :] * o_win[:, c])
        o_ref[rows, :] = o.astype(BF16)


def _selwin_prompt(qr_t, mneg, ks, vs_t, kw, vw_t, ocmp_t, gates_t):
    b, _, t = qr_t.shape
    n_sel = mneg.shape[2]
    assert t % (2 * SEL_TILE) == 0 and t >= WIN_KEYS
    cols = N_HEADS * Q_BLOCK
    per_q = lambda r: pl.BlockSpec((None, r, Q_BLOCK), lambda bi, i: (bi, 0, i))
    per_b = lambda *shape: pl.BlockSpec((None,) + shape, lambda bi, i: (bi,) + (0,) * len(shape),
                                        pipeline_mode=pl.Buffered(1))
    vals = per_b(N_KV_HEADS, t // LANES, LANES, LANES)
    return pl.pallas_call(
        _selwin_prompt_kernel,
        grid=(b, t // Q_BLOCK),
        in_specs=[
            per_q(Q_PAD),
            pl.BlockSpec((None, N_KV_HEADS, n_sel, Q_BLOCK), lambda bi, i: (bi, 0, 0, i)),
            per_b(t, LANES + n_sel), vals, per_b(t, LANES), vals,
            per_q(Q_PAD), per_q(N_KV_HEADS * LANES),
        ],
        out_specs=per_q(Q_PAD),
        out_shape=jax.ShapeDtypeStruct((b, Q_PAD, t), BF16),
        scratch_shapes=[pltpu.VMEM((PAD_HEAD + n_sel, cols), BF16),
                        pltpu.VMEM((SEL_TILE, cols), F32),
                        pltpu.VMEM((SEL_TILE, cols), F32),
                        pltpu.VMEM((SEL_TILE, cols), BF16),
                        pltpu.VMEM((SEL_TILE, cols), BF16)],
        compiler_params=_cparams(("parallel", "arbitrary")),
        name="selwin_prompt",
    )(qr_t, mneg, ks, vs_t, kw, vw_t, ocmp_t, gates_t)


PAGES_PER_STEP = 16
S_COLS = LANES


def _cmp_topk_sample_kernel(n_t, pt_ref, *refs):
    page_refs = refs[:PAGES_PER_STEP]
    (q_ref, pe_ref, w1_ref, b1_ref, w2_ref, b2_ref,
     ocmp_ref, idx_ref, kcv_ref, pk_ref, carry_ref, xk_ref, xv_ref) = refs[PAGES_PER_STEP:]
    s = pl.program_id(1)
    n_steps = pl.num_programs(1)
    n_e = kcv_ref.shape[0]
    step_e = PAGES_PER_STEP * PAGE_SIZE // CMP_STRIDE

    @pl.when(s == 0)
    def _():
        carry_ref[...] = jnp.zeros(carry_ref.shape, F32)

    for k, page in enumerate(page_refs):
        rows = slice(k * PAGE_SIZE, (k + 1) * PAGE_SIZE)
        xk_ref[rows, :] = page[0:LANES, :].T
        xv_ref[rows, :] = page[LANES:KV_ROW, :].T
    out = _compress_rows([(xk_ref, xv_ref)], pe_ref, w1_ref, b1_ref, w2_ref, b2_ref, carry_ref)
    kcv_ref[pl.ds(pl.multiple_of(s * step_e, step_e), step_e), :] = out.astype(BF16)

    @pl.when(s == n_steps - 1)
    def _():
        n_sel = n_e // (SEL_BLOCK // CMP_STRIDE) + 1
        n_sel_rows = pk_ref.shape[0] // (SEL_BLOCK // CMP_STRIDE) - 2
        kc = kcv_ref[:, 0:LANES]
        vc = kcv_ref[:, LANES:KV_ROW]
        e_idx = lax.broadcasted_iota(I32, (n_e, S_COLS), 0)
        p = _masked_softmax_rows(_dot_nt(kc, q_ref[...]), e_idx >= 1)
        ocmp_ref[...] = _dot_tn(p.astype(BF16), vc)
        pkv = p
        for hh in range(1, GROUP):
            pkv = pkv + pltpu.roll(p, S_COLS - n_t * hh, 1)
        pk_ref[0:n_e, :] = pkv
        pk_ref[n_e:, :] = jnp.zeros((pk_ref.shape[0] - n_e, S_COLS), F32)
        slc = _selection_scores(pk_ref, n_sel_rows)
        j = lax.broadcasted_iota(I32, (n_sel_rows, S_COLS), 0)
        qp = PAST_LEN + (lax.broadcasted_iota(I32, (n_sel_rows, S_COLS), 1) & (n_t - 1))
        qblk = jnp.right_shift(qp, 6)
        forced = (j == 0) | (j == qblk) | (j == qblk - 1)
        in_range = j < n_sel
        score = jnp.where(forced & in_range, jnp.inf,
                          jnp.where((j * SEL_BLOCK <= qp) & in_range, slc, -jnp.inf))
        _, picks = _topk_rounds(score, n_sel_rows)
        for r, mi in enumerate(picks):
            idx_ref[r:r + 1, :] = mi.astype(I32)


def _cmp_topk_sample(page_table, cache_cmp_t, layer, q_cols, cw, n_t):
    assert n_t & (n_t - 1) == 0 and N_HEADS * n_t <= S_COLS
    b, n_pages = page_table.shape
    n_e = n_pages * PAGE_SIZE // CMP_STRIDE
    n_sel_rows = ((n_e // 4 + 1) + 7) // 8 * 8
    pk_rows = 4 * (n_sel_rows + 2)
    n_steps = n_pages // PAGES_PER_STEP
    step_rows = PAGES_PER_STEP * PAGE_SIZE

    def page_spec(k):
        return pl.BlockSpec((None, None, KV_ROW, PAGE_SIZE),
                            lambda bi, s, pt: (layer, pt[bi, s * PAGES_PER_STEP + k], 0, 0))

    per_b = lambda rows, w: pl.BlockSpec((None, rows, w), lambda bi, s, pt: (bi, 0, 0))
    grid_spec = pltpu.PrefetchScalarGridSpec(
        num_scalar_prefetch=1,
        grid=(b, n_steps),
        in_specs=[page_spec(k) for k in range(PAGES_PER_STEP)]
        + [per_b(S_COLS, LANES)] + _cmp_weight_specs(),
        out_specs=[per_b(S_COLS, LANES), per_b(N_SEL, S_COLS)],
        scratch_shapes=[pltpu.VMEM((n_e, KV_ROW), BF16),
                        pltpu.VMEM((pk_rows, S_COLS), F32),
                        pltpu.VMEM((4, 8, CMP_HIDDEN), F32),
                        pltpu.VMEM((step_rows, LANES), F32),
                        pltpu.VMEM((step_rows, LANES), F32)],
    )
    return pl.pallas_call(
        functools.partial(_cmp_topk_sample_kernel, n_t),
        grid_spec=grid_spec,
        out_shape=[jax.ShapeDtypeStruct((b, S_COLS, LANES), F32),
                   jax.ShapeDtypeStruct((b, N_SEL, S_COLS), I32)],
        compiler_params=_cparams(("parallel", "arbitrary")),
        name="cmp_topk_sample",
    )(page_table, *([cache_cmp_t] * PAGES_PER_STEP), q_cols, *_cmp_weight_args(cw))


HROWS = 8


def _selwin_sample_kernel(idx_ref, pt_ref, *refs):
    page_refs = refs[:N_SEL]
    (q_ref, new_s_ref, win_ref, new_w_ref, ocmp_ref, gt_ref, o_ref, k_ref, v_ref) = refs[N_SEL:]
    bi = pl.program_id(0)
    tq = pl.program_id(1)
    kh = pl.program_id(2)
    n_t = pl.num_programs(1)
    qpos = PAST_LEN + tq
    q = q_ref[...]
    base = ((bi * n_t + tq) * N_KV_HEADS + kh) * N_SEL
    new_block = PAST_LEN // SEL_BLOCK

    valid_parts = []
    lane = lax.broadcasted_iota(I32, (1, PAGE_SIZE), 1)
    for r in range(N_SEL):
        j = idx_ref[base + r]
        is_new = j == new_block
        cols = slice(r * PAGE_SIZE, (r + 1) * PAGE_SIZE)

        @pl.when(is_new)
        def _():
            k_ref[:, cols] = new_s_ref[0:LANES, :].astype(BF16)
            v_ref[:, cols] = new_s_ref[LANES:KV_ROW, :].astype(BF16)

        @pl.when(jnp.logical_not(is_new))
        def _():
            k_ref[:, cols] = page_refs[r][0:LANES, :].astype(BF16)
            v_ref[:, cols] = page_refs[r][LANES:KV_ROW, :].astype(BF16)

        kpos = jnp.right_shift(j, 1) * PAGE_SIZE + lane
        valid_parts.append((jnp.right_shift(kpos, 6) == j) & (kpos <= qpos))
    valid = jnp.concatenate(valid_parts, axis=1)
    s = jnp.where(valid, _dot(q, k_ref[...]), -jnp.inf)
    m = jnp.max(s, axis=-1, keepdims=True)
    p = jnp.exp(s - m)
    o_sel = _dot_nt(p.astype(BF16), v_ref[...]) / jnp.sum(p, axis=-1, keepdims=True)

    wb = win_ref.shape[1]
    s_old = _dot(q, win_ref[0:LANES, :].astype(BF16))
    d_old = qpos - (PAST_LEN - wb + lax.broadcasted_iota(I32, (1, wb), 1))
    s_old = jnp.where((d_old >= 0) & (d_old <= WINDOW), s_old, -jnp.inf)
    s_new = _dot(q, new_w_ref[0:LANES, :].astype(BF16))
    d_new = tq - lax.broadcasted_iota(I32, (1, new_w_ref.shape[1]), 1)
    s_new = jnp.where((d_new >= 0) & (d_new <= WINDOW), s_new, -jnp.inf)
    m = jnp.maximum(jnp.max(s_old, axis=-1, keepdims=True), jnp.max(s_new, axis=-1, keepdims=True))
    p_old = jnp.exp(s_old - m)
    p_new = jnp.exp(s_new - m)
    den = jnp.sum(p_old, axis=-1, keepdims=True) + jnp.sum(p_new, axis=-1, keepdims=True)
    o_win = (_dot_nt(p_old.astype(BF16), win_ref[LANES:KV_ROW, :].astype(BF16))
             + _dot_nt(p_new.astype(BF16), new_w_ref[LANES:KV_ROW, :].astype(BF16))) / den

    g = gt_ref[...]
    o = g[:, 0:1] * ocmp_ref[...] + g[:, 1:2] * o_sel + g[:, 2:3] * o_win
    o_ref[...] = o.astype(BF16)


def _selwin_sample(idx_flat, page_table, cache_sel_t, win_t, layer, q_rows, new_s_t, new_w_t, ocmp_rows, gate_rows):
    b, n_t = q_rows.shape[:2]
    wb = win_t.shape[-1]
    n_t_static = n_t
    last_page = page_table.shape[1] - 1

    def page_spec(r):
        def imap(bi, tq, kh, idx, pt):
            j = idx[((bi * n_t_static + tq) * N_KV_HEADS + kh) * N_SEL + r]
            return (layer, pt[bi, jnp.minimum(jnp.right_shift(j, 1), last_page)], 0, 0)
        return pl.BlockSpec((None, None, KV_ROW, PAGE_SIZE), imap)

    row5 = pl.BlockSpec((None, None, None, HROWS, LANES), lambda bi, tq, kh, idx, pt: (bi, tq, kh, 0, 0))
    new_rows = pl.BlockSpec((None, KV_ROW, LANES), lambda bi, tq, kh, idx, pt: (bi, 0, 0))
    grid_spec = pltpu.PrefetchScalarGridSpec(
        num_scalar_prefetch=2,
        grid=(b, n_t, N_KV_HEADS),
        in_specs=[page_spec(r) for r in range(N_SEL)]
        + [row5, new_rows,
           pl.BlockSpec((None, None, KV_ROW, wb), lambda bi, tq, kh, idx, pt: (layer, bi, 0, 0)),
           new_rows, row5, row5],
        out_specs=row5,
        scratch_shapes=[pltpu.VMEM((LANES, N_SEL * PAGE_SIZE), BF16),
                        pltpu.VMEM((LANES, N_SEL * PAGE_SIZE), BF16)],
    )
    return pl.pallas_call(
        _selwin_sample_kernel,
        grid_spec=grid_spec,
        out_shape=jax.ShapeDtypeStruct((b, n_t, N_KV_HEADS, HROWS, LANES), BF16),
        compiler_params=_cparams(("arbitrary", "arbitrary", "arbitrary")),
        name="selwin_sample",
    )(idx_flat, page_table, *([cache_sel_t] * N_SEL), q_rows, new_s_t, win_t, new_w_t, ocmp_rows, gate_rows)


def _outproj_ln_kernel(attn_transposed, x_ref, yc_ref, yl_ref, ya_ref, w_ref, g_ref, b_ref, o_ref):
    y = _dot(yc_ref[...], w_ref[0:D_CONV, :])
    y = y + _dot(yl_ref[...], w_ref[D_CONV:D_CONV + D_LRU, :])
    w_attn = w_ref[D_CONV + D_LRU:, :]
    y = y + (_dot_tn(ya_ref[...], w_attn) if attn_transposed else _dot(ya_ref[...], w_attn))
    o_ref[...] = _layernorm(ALPHA * x_ref[...] + y, g_ref[...], b_ref[...])


def _outproj_ln(x, yc, yl, ya, ow, tm):
    n = x.shape[0]
    tok = lambda w: pl.BlockSpec((tm, w), lambda i: (i, 0))
    attn_transposed = ya.ndim == 3
    if attn_transposed:
        tps = ya.shape[2] // tm
        ya_spec = pl.BlockSpec((None, Q_PAD, tm), lambda i: (i // tps, 0, i % tps))
    else:
        ya_spec = tok(Q_PAD)
    return pl.pallas_call(
        functools.partial(_outproj_ln_kernel, attn_transposed),
        grid=(n // tm,),
        in_specs=[tok(D_MODEL), tok(D_CONV), tok(D_LRU), ya_spec,
                  _const_spec((D_CONV + D_LRU + Q_PAD, D_MODEL)),
                  _const_spec((1, D_MODEL)), _const_spec((1, D_MODEL))],
        out_specs=tok(D_MODEL),
        out_shape=jax.ShapeDtypeStruct((n, D_MODEL), F32),
        compiler_params=_cparams(("parallel",)),
        name="outproj_ln",
    )(x, yc, yl, ya, ow["w"], ow["g"], ow["b"])


def _rope_tables(pos):
    half = ROPE_DIM // 2
    inv = ROPE_THETA ** (-jnp.arange(half, dtype=F32) / half)
    ang = pos.astype(F32)[:, None] * inv[None, :]
    cos, sin = jnp.cos(ang), jnp.sin(ang)
    n = pos.shape[0]
    rest = HEAD_DIM - ROPE_DIM
    zeros8 = jnp.zeros((n, half), F32)
    c = jnp.concatenate([cos, cos, jnp.ones((n, rest), F32)], axis=1)
    s1 = jnp.concatenate([zeros8, sin, jnp.zeros((n, rest), F32)], axis=1)
    s2 = jnp.concatenate([-sin, zeros8, jnp.zeros((n, rest), F32)], axis=1)
    rep = LANES // HEAD_DIM
    return tuple(jnp.tile(a, (1, rep)) for a in (c, s1, s2))


def _head_pad_index():
    h = np.arange(D_ATTN) // HEAD_DIM
    d = np.arange(D_ATTN) % HEAD_DIM
    return h * PAD_HEAD + (h // GROUP) * HEAD_DIM + d


def _prep_layer(l, ln_g, ln_b, ffn_w_in, ffn_w_out, w_in, conv_w, conv_b, conv_ln_g, conv_ln_b,
                lru_conv_w, lru_conv_b, lru_w_gate, lru_b_gate, lru_lambda,
                cmp_pe, cmp_w1, cmp_b1, cmp_w2, cmp_b2, w_out):
    row = lambda v: v.reshape(1, -1).astype(F32)
    ffn = []
    for f, ln_i in ((0, 0), (1, 2)):
        wi = ffn_w_in[l, f]
        wg = wi[:, :D_FF].reshape(D_MODEL, N_FF_CHUNKS, FF_CHUNK).transpose(1, 0, 2).astype(BF16)
        wu = wi[:, D_FF:].reshape(D_MODEL, N_FF_CHUNKS, FF_CHUNK).transpose(1, 0, 2).astype(BF16)
        wo = ffn_w_out[l, f].reshape(N_FF_CHUNKS, FF_CHUNK, D_MODEL).astype(BF16)
        ffn.append({"wg": wg, "wu": wu, "wo": wo, "g": row(ln_g[l, ln_i]), "b": row(ln_b[l, ln_i])})

    wl = w_in[l]
    o_q = 2 * D_CONV + 2 * D_LRU
    o_kv = o_q + D_ATTN
    o_g = o_kv + 3 * KV_ROW
    pad_idx = _head_pad_index()
    wq = jnp.zeros((D_MODEL, Q_PAD), F32).at[:, pad_idx].set(wl[:, o_q:o_kv])
    hh = np.arange(3 * N_HEADS) // 3
    gate_idx = (hh // GROUP) * LANES + (hh % GROUP) * 3 + np.arange(3 * N_HEADS) % 3
    wgt = jnp.zeros((D_MODEL, N_KV_HEADS * LANES), F32).at[:, gate_idx].set(wl[:, o_g:])
    w_all = jnp.concatenate([wl[:, :o_q], wq, wl[:, o_kv:o_g], wgt], axis=1).astype(BF16)
    w_t = jnp.concatenate([w_all[:, C_Q:C_KVC], w_all[:, C_KVS + LANES:C_KVW],
                           w_all[:, C_KVW + LANES:C_GATE], w_all[:, C_GATE:]], axis=1).T

    conv = {"w": jnp.pad(conv_w[l], ((0, CONV_HALO - CONV_WIDTH), (0, 0))),
            "b": row(conv_b[l]), "ln_g": row(conv_ln_g[l]), "ln_b": row(conv_ln_b[l])}

    def blockdiag(w):
        out = jnp.zeros((D_LRU, D_LRU), F32)
        for n in range(LRU_BLOCKS):
            out = out.at[n * LRU_BW:(n + 1) * LRU_BW, n * LRU_BW:(n + 1) * LRU_BW].set(w[n])
        return out.astype(BF16)

    lru = {"cw": jnp.pad(lru_conv_w[l], ((0, LRU_HALO - LRU_CONV_WIDTH), (0, 0))),
           "cb": row(lru_conv_b[l]),
           "wr": blockdiag(lru_w_gate[l, 0]), "wi": blockdiag(lru_w_gate[l, 1]),
           "bg": lru_b_gate[l].astype(F32), "lam": row(lru_lambda[l])}

    pe = cmp_pe[l].reshape(2, 2, CMP_STRIDE * HEAD_DIM)
    pe_rows = jnp.stack([jnp.stack([pe[g // 2, j] for g in range(4)]) for j in range(2)])
    w2e = jnp.zeros((4, CMP_HIDDEN, KV_ROW), F32)
    for g in range(4):
        w2e = w2e.at[g, :, g * HEAD_DIM:(g + 1) * HEAD_DIM].set(cmp_w2[l, g // 2])
    cmp = {"pe": pe_rows.astype(F32),
           "w1": cmp_w1[l].reshape(2, 2, CMP_STRIDE * HEAD_DIM, CMP_HIDDEN).astype(BF16),
           "b1": cmp_b1[l].astype(F32),
           "w2": w2e.astype(BF16),
           "b2": jnp.concatenate([cmp_b2[l, 0], cmp_b2[l, 0], cmp_b2[l, 1], cmp_b2[l, 1]]).reshape(1, -1)}

    wo = w_out[l]
    wo_attn = jnp.zeros((Q_PAD, D_MODEL), F32).at[pad_idx, :].set(wo[D_CONV + D_LRU:])
    out = {"w": jnp.concatenate([wo[:D_CONV + D_LRU], wo_attn], axis=0).astype(BF16),
           "g": row(ln_g[l, 1]), "b": row(ln_b[l, 1])}
    return {"ffn": ffn, "w_all": w_all, "w_t": w_t, "conv": conv, "lru": lru, "cmp": cmp, "out": out}


def _pad_front(a, rows):
    return jnp.pad(a, ((0, 0), (rows - a.shape[1], 0), (0, 0)))


def _kv6(a, lead):
    return a.reshape(lead + (2, N_KV_HEADS, HEAD_DIM))


TM_PROMPT = 512
TC_PROMPT = 512
TL_PROMPT = 256


def _layer_prompt(x, bsz, t, lw, tabs):
    n = bsz * t
    x = _ffn_ln(x, lw["ffn"][0], TM_PROMPT)
    tabs_t = tuple(a.T for a in tabs)
    (u, lx, lg, kvc, kvs, kvw, ks, kw, qc_t, qr_t, vs_t, vw_t, gates_t) = _inproj_t(
        x, lw["w_all"], lw["w_t"], tabs, tabs_t, bsz, t, TM_PROMPT)
    s3 = lambda a: a.reshape(bsz, t, a.shape[-1])
    u3, lx3 = s3(u), s3(lx)
    yc = _conv_group(u3, jnp.zeros((bsz, CONV_HALO, D_CONV), F32), lw["conv"], TC_PROMPT)
    yl, h_last = _lru_group(lx3, s3(lg), jnp.zeros((bsz, LRU_HALO, D_LRU), F32),
                            jnp.zeros((bsz, LRU_HALO, D_LRU), F32), lw["lru"], TL_PROMPT, TL_PROMPT - 1)
    kc, vc_t = _compress_prompt(s3(kvc), lw["cmp"])
    ocmp_t, mneg = _cmp_topk_prompt(qc_t, kc, vc_t)
    ya_t = _selwin_prompt(qr_t, mneg, s3(ks), vs_t, s3(kw), vw_t, ocmp_t, gates_t)
    x = _outproj_ln(x, yc.reshape(n, D_CONV), yl.reshape(n, D_LRU), ya_t, lw["out"], TM_PROMPT)
    x = _ffn_ln(x, lw["ffn"][1], TM_PROMPT)
    kvw3 = s3(kvw)
    state = (u3[:, t - (CONV_WIDTH - 1):], lx3[:, t - (LRU_CONV_WIDTH - 1):], h_last[:, 0],
             _kv6(kvc, (bsz, t)), _kv6(kvs, (bsz, t)), _kv6(kvw3[:, t - min(WINDOW, t):], (bsz, min(WINDOW, t))))
    return x, state


T_PAD = 8


def _layer_sample(x, bsz, t, lw, tabs, conv_buf, lru_buf, lru_h, cache_cmp_t, cache_sel_t, win_t, layer, page_table):
    n = bsz * t
    x = _ffn_ln(x, lw["ffn"][0], n)
    (u, lx, lg, qc, qr, kvc, kvs, kvs_b, kvw, kvw_b, gates) = _inproj(x, lw["w_all"], tabs, n, 1)
    s3 = lambda a: a.reshape(bsz, t, a.shape[-1])
    padt = lambda a: jnp.pad(s3(a), ((0, 0), (0, T_PAD - t), (0, 0)))
    yc = _conv_group(padt(u), _pad_front(conv_buf, CONV_HALO), lw["conv"], T_PAD)[:, :t]
    h0 = jnp.broadcast_to(lru_h[:, None, :], (bsz, LRU_HALO, D_LRU))
    yl, h_all = _lru_group(padt(lx), padt(lg), _pad_front(lru_buf, LRU_HALO), h0, lw["lru"], T_PAD, t - 1)
    yl = yl[:, :t]

    def head_cols(a):
        a = a.reshape(bsz, t, N_HEADS, PAD_HEAD).transpose(0, 2, 1, 3).reshape(bsz, N_HEADS * t, PAD_HEAD)
        return jnp.pad(a, ((0, 0), (0, S_COLS - N_HEADS * t), (0, 0)))

    ocmp_cols, picks = _cmp_topk_sample(page_table, cache_cmp_t, layer, head_cols(qc), lw["cmp"], t)
    pk = picks[:, :, :N_HEADS * t].reshape(bsz, N_SEL, N_KV_HEADS, GROUP, t)[:, :, :, 0, :]
    idx_flat = pk.transpose(0, 3, 2, 1).reshape(-1).astype(I32)

    def head_rows(a, dt):
        a = a.reshape(bsz, N_KV_HEADS, GROUP, t, PAD_HEAD).transpose(0, 3, 1, 2, 4)
        return jnp.pad(a, ((0, 0), (0, 0), (0, 0), (0, HROWS - GROUP), (0, 0))).astype(dt)

    q_rows = head_rows(qr.reshape(bsz, t, N_HEADS, PAD_HEAD).transpose(0, 2, 1, 3).reshape(bsz, N_HEADS * t, PAD_HEAD), BF16)
    ocmp_rows = head_rows(ocmp_cols[:, :N_HEADS * t], F32)
    g3 = gates.reshape(bsz, t, N_KV_HEADS, LANES)[..., :3 * GROUP].reshape(bsz, t, N_KV_HEADS, GROUP, 3)
    gate_rows = jnp.pad(g3, ((0, 0), (0, 0), (0, 0), (0, HROWS - GROUP), (0, LANES - 3)))
    pos_last = lambda a: s3(a).transpose(0, 2, 1)
    pad_cols = lambda a: jnp.pad(a, ((0, 0), (0, 0), (0, LANES - t)))
    kvs_t, kvw_t = pos_last(kvs), pos_last(kvw)
    ya_rows = _selwin_sample(idx_flat, page_table, cache_sel_t, win_t, layer, q_rows, pad_cols(kvs_t),
                             pad_cols(kvw_t), ocmp_rows, gate_rows)
    ya = ya_rows[:, :, :, :GROUP].reshape(bsz, t, Q_PAD)

    x = _outproj_ln(x, yc.reshape(n, D_CONV), yl.reshape(n, D_LRU), ya.reshape(n, Q_PAD), lw["out"], n)
    x = _ffn_ln(x, lw["ffn"][1], n)
    new_conv = jnp.concatenate([conv_buf, s3(u)], axis=1)[:, t:]
    new_lru = jnp.concatenate([lru_buf, s3(lx)], axis=1)[:, t:]
    win_all = jnp.concatenate([win_t[layer], kvw_t], axis=-1)
    n_win = min(WINDOW, win_all.shape[-1])
    new_win = win_all[..., win_all.shape[-1] - n_win:]
    new_win = new_win.reshape(bsz, 2, N_KV_HEADS, HEAD_DIM, n_win).transpose(0, 4, 1, 2, 3)
    state = (new_conv, new_lru, h_all[:, 0], _kv6(kvc, (bsz, t)), _kv6(kvs, (bsz, t)), new_win)
    return x, state


def kernel(x_prompt, x_sample, state_conv, state_lru_conv, state_lru_h, cache_cmp_kv, cache_sel_kv, cache_win_kv,
           page_table, ln_g, ln_b, ffn_w_in, ffn_w_out, w_in, conv_w, conv_b, conv_ln_g, conv_ln_b,
           lru_conv_w, lru_conv_b, lru_w_gate, lru_b_gate, lru_lambda, cmp_pe, cmp_w1, cmp_b1, cmp_w2, cmp_b2, w_out):
    bp, tp, _ = x_prompt.shape
    bs, ts, _ = x_sample.shape
    depth = ln_g.shape[0]
    past = page_table.shape[1] * PAGE_SIZE
    assert past == PAST_LEN and past % SEL_BLOCK == 0
    tabs_p = _rope_tables(jnp.arange(tp))
    tabs_s = _rope_tables(jnp.tile(past + jnp.arange(ts), bs))
    xp = x_prompt.reshape(bp * tp, D_MODEL)
    xs = x_sample.reshape(bs * ts, D_MODEL)
    n_pool = cache_cmp_kv.shape[1]
    pos_last = lambda c: jnp.transpose(c, (0, 1, 3, 4, 5, 2)).reshape(c.shape[:2] + (KV_ROW, c.shape[2]))
    cache_cmp_t, cache_sel_t, win_t = pos_last(cache_cmp_kv), pos_last(cache_sel_kv), pos_last(cache_win_kv)
    st_p, st_s = [], []
    for l in range(depth):
        lw = _prep_layer(l, ln_g, ln_b, ffn_w_in, ffn_w_out, w_in, conv_w, conv_b, conv_ln_g, conv_ln_b,
                         lru_conv_w, lru_conv_b, lru_w_gate, lru_b_gate, lru_lambda,
                         cmp_pe, cmp_w1, cmp_b1, cmp_w2, cmp_b2, w_out)
        xp, sp = _layer_prompt(xp, bp, tp, lw, tabs_p)
        xs, ss = _layer_sample(
            xs, bs, ts, lw, tabs_s, state_conv[l], state_lru_conv[l], state_lru_h[l],
            cache_cmp_t, cache_sel_t, win_t, l, page_table)
        st_p.append(sp)
        st_s.append(ss)
    outs = [xp.reshape(bp, tp, D_MODEL), xs.reshape(bs, ts, D_MODEL)]
    for k in range(6):
        outs.append(jnp.stack([s[k] for s in st_p]))
        outs.append(jnp.stack([s[k] for s in st_s]))
    return tuple(outs)
```

```python
import functools

import numpy as np
import jax
import jax.numpy as jnp
from jax import lax
from jax.experimental import pallas as pl
from jax.experimental.pallas import tpu as pltpu

F32 = jnp.float32
BF16 = jnp.bfloat16
I32 = jnp.int32

D_MODEL = 1024
DEPTH = 2
PAST_LEN = 16384
PAGE_SIZE = 128
D_CONV = 256
CONV_WIDTH = 31
D_LRU = 256
LRU_BLOCKS = 4
LRU_BW = D_LRU // LRU_BLOCKS
LRU_CONV_WIDTH = 4
LRU_C = 8.0
D_ATTN = 512
N_HEADS = 8
HEAD_DIM = 64
N_KV_HEADS = 2
GROUP = N_HEADS // N_KV_HEADS
KV_ROW = 2 * N_KV_HEADS * HEAD_DIM
ROPE_DIM = 16
ROPE_THETA = 500000.0
CMP_BLOCK = 32
CMP_STRIDE = 16
CMP_HIDDEN = 256
SEL_BLOCK = 64
N_SEL = 16
WINDOW = 512
Q_BLOCK = 128
D_FF = 2816
ALPHA = (2 * DEPTH) ** 0.25
LN_EPS = 1e-5
SCALE = HEAD_DIM ** -0.5

LANES = 128
VMEM_LIMIT = 56 * 1024 * 1024
FF_CHUNK = 256
N_FF_CHUNKS = D_FF // FF_CHUNK
PAD_HEAD = 128
Q_PAD = N_HEADS * PAD_HEAD
KV_GROUP_LANES = GROUP * PAD_HEAD
MASK_NEG = -(2.0 ** 60)
SEL_TILE = 512
WIN_KEYS = WINDOW + Q_BLOCK

C_GLU = 0
C_LRUX = 512
C_LRUG = 768
C_Q = 1024
C_KVC = C_Q + Q_PAD
C_KVS = C_KVC + KV_ROW
C_KVW = C_KVS + KV_ROW
C_GATE = C_KVW + KV_ROW
N_PROJ = C_GATE + N_KV_HEADS * LANES


def _cparams(sem):
    return pltpu.CompilerParams(dimension_semantics=sem, vmem_limit_bytes=VMEM_LIMIT)


def _const_spec(shape):
    nd = len(shape)
    return pl.BlockSpec(shape, lambda *_: (0,) * nd, pipeline_mode=pl.Buffered(1))


def _layernorm(y, g, b):
    mu = jnp.mean(y, axis=-1, keepdims=True)
    d = y - mu
    var = jnp.mean(d * d, axis=-1, keepdims=True)
    return d * lax.rsqrt(var + LN_EPS) * g + b


def _dot(a, b):
    return jnp.dot(a, b, preferred_element_type=F32)


def _dot_nt(a, b):
    return lax.dot_general(a, b, (((1,), (1,)), ((), ())), preferred_element_type=F32)


def _dot_tn(a, b):
    return lax.dot_general(a, b, (((0,), (0,)), ((), ())), preferred_element_type=F32)


def _ffn_ln_apply(x, wg_ref, wu_ref, wo_ref, g_ref, b_ref):
    xb = x.astype(BF16)
    acc = jnp.zeros(x.shape, F32)
    for c in range(N_FF_CHUNKS):
        gate = _dot(xb, wg_ref[c])
        up = _dot(xb, wu_ref[c])
        h = (gate * jax.nn.sigmoid(gate)) * up
        acc = acc + _dot(h.astype(BF16), wo_ref[c])
    y = ALPHA * x + 0.5 * acc
    return _layernorm(y, g_ref[...], b_ref[...])


def _ffn_ln_kernel(x_ref, wg_ref, wu_ref, wo_ref, g_ref, b_ref, o_ref):
    o_ref[...] = _ffn_ln_apply(x_ref[...], wg_ref, wu_ref, wo_ref, g_ref, b_ref)


def _ffn_weight_specs():
    return [
        _const_spec((N_FF_CHUNKS, D_MODEL, FF_CHUNK)),
        _const_spec((N_FF_CHUNKS, D_MODEL, FF_CHUNK)),
        _const_spec((N_FF_CHUNKS, FF_CHUNK, D_MODEL)),
        _const_spec((1, D_MODEL)),
        _const_spec((1, D_MODEL)),
    ]


def _ffn_weight_args(fw):
    return (fw["wg"], fw["wu"], fw["wo"], fw["g"], fw["b"])


def _ffn_ln(x, fw, tm):
    n = x.shape[0]
    return pl.pallas_call(
        _ffn_ln_kernel,
        grid=(n // tm,),
        in_specs=[pl.BlockSpec((tm, D_MODEL), lambda i: (i, 0))] + _ffn_weight_specs(),
        out_specs=pl.BlockSpec((tm, D_MODEL), lambda i: (i, 0)),
        out_shape=jax.ShapeDtypeStruct((n, D_MODEL), F32),
        compiler_params=_cparams(("parallel",)),
        name="ffn_ln",
    )(x, *_ffn_weight_args(fw))


def _rope(v, cos, s1, s2):
    return v * cos + pltpu.roll(v, 8, 1) * s1 + pltpu.roll(v, LANES - 8, 1) * s2


def _inproj_kernel(x_ref, w_ref, cos_ref, s1_ref, s2_ref,
                   u_ref, lx_ref, lg_ref, qc_ref, qr_ref,
                   kvc_ref, kvs_ref, kvsb_ref, kvw_ref, kvwb_ref, gt_ref):
    xb = x_ref[...].astype(BF16)

    def mm(lo, hi):
        return _dot(xb, w_ref[:, lo:hi])

    glu = mm(C_GLU, C_GLU + 2 * D_CONV)
    u_ref[...] = glu[:, :D_CONV] * jax.nn.sigmoid(glu[:, D_CONV:])
    lx_ref[...] = mm(C_LRUX, C_LRUX + D_LRU)
    lg_ref[...] = mm(C_LRUG, C_LRUG + D_LRU)
    cos = cos_ref[...]
    s1 = s1_ref[...]
    s2 = s2_ref[...]
    for h in range(N_HEADS):
        qh = mm(C_Q + h * PAD_HEAD, C_Q + (h + 1) * PAD_HEAD) * SCALE
        qc_ref[:, h * PAD_HEAD:(h + 1) * PAD_HEAD] = qh.astype(BF16)
        qr_ref[:, h * PAD_HEAD:(h + 1) * PAD_HEAD] = _rope(qh, cos, s1, s2).astype(BF16)
    kvc_ref[...] = mm(C_KVC, C_KVC + KV_ROW)
    for c0, f_ref, b_ref in ((C_KVS, kvs_ref, kvsb_ref), (C_KVW, kvw_ref, kvwb_ref)):
        kv = mm(c0, c0 + KV_ROW)
        k = _rope(kv[:, :LANES], cos, s1, s2)
        v = kv[:, LANES:]
        f_ref[:, 0:LANES] = k
        f_ref[:, LANES:KV_ROW] = v
        b_ref[:, 0:LANES] = k.astype(BF16)
        b_ref[:, LANES:KV_ROW] = v.astype(BF16)
    gt_ref[...] = jax.nn.sigmoid(mm(C_GATE, N_PROJ))


def _rope_t(v, cos, s1, s2):
    return v * cos + pltpu.roll(v, 8, 0) * s1 + pltpu.roll(v, PAD_HEAD - 8, 0) * s2


R_Q = 0
R_KVS = Q_PAD
R_KVW = R_KVS + KV_ROW
R_KVC = R_KVW + KV_ROW
R_GATE = R_KVC + KV_ROW
N_PROJ_T = R_GATE + N_KV_HEADS * LANES


LOG2E = 1.4426950408889634
SUM_ROWS = (HEAD_DIM, 0)


def _inproj_t_kernel(x_ref, wg_ref, wu_ref, wo_ref, g_ref, b_ref,
                     w_ref, wt_ref, cos_ref, s1_ref, s2_ref, cos_t_ref, s1_t_ref, s2_t_ref, blk_ref,
                     x1_ref, u_ref, lx_ref, lg_ref, kvc_ref, ks_ref, kw_ref,
                     kvct_ref, kvst_ref, kvwt_ref, qc_ref, qr_ref, vs_ref, vw_ref, gt_ref):
    x1 = _ffn_ln_apply(x_ref[...], wg_ref, wu_ref, wo_ref, g_ref, b_ref)
    x1_ref[...] = x1
    xb = x1.astype(BF16)
    tm = xb.shape[0]

    def mm(lo, hi):
        return _dot(xb, w_ref[:, lo:hi])

    def mm_t(lo, hi):
        return _dot_nt(wt_ref[lo:hi, :], xb)

    glu = mm(C_GLU, C_GLU + 2 * D_CONV)
    u_ref[...] = glu[:, :D_CONV] * jax.nn.sigmoid(glu[:, D_CONV:])
    lx_ref[...] = mm(C_LRUX, C_LRUX + D_LRU)
    lg_ref[...] = mm(C_LRUG, C_LRUG + D_LRU)
    kvc_ref[...] = mm(C_KVC, C_KVC + KV_ROW)
    cos = cos_ref[...]
    s1 = s1_ref[...]
    s2 = s2_ref[...]
    for c0, b_ref in ((C_KVS, ks_ref), (C_KVW, kw_ref)):
        b_ref[:, 0:LANES] = _rope(mm(c0, c0 + LANES), cos, s1, s2).astype(BF16)
    ks_ref[:, LANES:] = blk_ref[...]
    cos_t = cos_t_ref[...]
    s1_t = s1_t_ref[...]
    s2_t = s2_t_ref[...]
    for h in range(N_HEADS):
        rows = slice(h * PAD_HEAD, (h + 1) * PAD_HEAD)
        qh = mm_t(R_Q + h * PAD_HEAD, R_Q + (h + 1) * PAD_HEAD)
        qc_ref[rows, :] = (qh * SCALE).astype(BF16)
        qr_ref[rows, :] = (_rope_t(qh, cos_t, s1_t, s2_t) * (SCALE * LOG2E)).astype(BF16)
    row = lax.broadcasted_iota(I32, (LANES, tm), 0)
    kvct_ref[...] = mm_t(R_KVC, R_KVC + KV_ROW)
    for r0, leaf_ref, o_ref in ((R_KVS, kvst_ref, vs_ref), (R_KVW, kvwt_ref, vw_ref)):
        leaf_ref[0:LANES, :] = _rope_t(mm_t(r0, r0 + LANES), cos_t, s1_t, s2_t)
        vt = mm_t(r0 + LANES, r0 + KV_ROW)
        leaf_ref[LANES:KV_ROW, :] = vt
        per_head = (jnp.where(row < HEAD_DIM, vt, jnp.where(row == SUM_ROWS[0], 1.0, 0.0)),
                    jnp.where(row >= HEAD_DIM, vt, jnp.where(row == SUM_ROWS[1], 1.0, 0.0)))
        for kv, vk in enumerate(per_head):
            vk = vk.astype(BF16)
            for c in range(tm // LANES):
                o_ref[kv, c] = vk[:, c * LANES:(c + 1) * LANES]
    gt_ref[...] = jax.nn.sigmoid(mm_t(R_GATE, N_PROJ_T))


def _ffn_inproj_t(x, fw, w_all, w_t, rope_tabs, rope_tabs_t, bsz, t, tm):
    n = x.shape[0]
    tps = t // tm
    tok = lambda w: pl.BlockSpec((tm, w), lambda i: (i, 0))
    tab = pl.BlockSpec((tm, LANES), lambda i: (i % tps, 0))
    tab_t = pl.BlockSpec((PAD_HEAD, tm), lambda i: (0, i % tps))
    feat_t = lambda r: pl.BlockSpec((None, r, tm), lambda i: (i // tps, 0, i % tps))
    vt_spec = pl.BlockSpec((None, N_KV_HEADS, tm // LANES, LANES, LANES), lambda i: (i // tps, 0, i % tps, 0, 0))
    n_sel = t // SEL_BLOCK
    row_outs = [(D_MODEL, F32), (D_CONV, F32), (D_LRU, F32), (D_LRU, F32), (KV_ROW, F32),
                (LANES + n_sel, BF16), (LANES, BF16)]
    leaf_t = jax.ShapeDtypeStruct((bsz, KV_ROW, t), F32)
    vt_shape = jax.ShapeDtypeStruct((bsz, N_KV_HEADS, t // LANES, LANES, LANES), BF16)
    block_id = (jnp.arange(t)[:, None] // SEL_BLOCK == jnp.arange(n_sel)[None, :]).astype(BF16)
    return pl.pallas_call(
        _inproj_t_kernel,
        grid=(n // tm,),
        in_specs=[tok(D_MODEL)] + _ffn_weight_specs()
        + [_const_spec((D_MODEL, N_PROJ)), _const_spec((N_PROJ_T, D_MODEL)),
           tab, tab, tab, tab_t, tab_t, tab_t,
           pl.BlockSpec((tm, n_sel), lambda i: (i % tps, 0))],
        out_specs=[tok(w) for w, _ in row_outs]
        + [feat_t(KV_ROW)] * 3
        + [feat_t(Q_PAD), feat_t(Q_PAD), vt_spec, vt_spec, feat_t(N_KV_HEADS * LANES)],
        out_shape=[jax.ShapeDtypeStruct((n, w), dt) for w, dt in row_outs]
        + [leaf_t] * 3
        + [jax.ShapeDtypeStruct((bsz, Q_PAD, t), BF16), jax.ShapeDtypeStruct((bsz, Q_PAD, t), BF16),
           vt_shape, vt_shape, jax.ShapeDtypeStruct((bsz, N_KV_HEADS * LANES, t), F32)],
        compiler_params=_cparams(("parallel",)),
        name="ffn_inproj_t",
    )(x, *_ffn_weight_args(fw), w_all, w_t, *rope_tabs, *rope_tabs_t, block_id)


def _inproj(x, w_all, rope_tabs, tm, tiles_per_seq):
    n = x.shape[0]
    cos, s1, s2 = rope_tabs
    tok = lambda w: pl.BlockSpec((tm, w), lambda i: (i, 0))
    tab = pl.BlockSpec((tm, LANES), lambda i: (i % tiles_per_seq, 0))
    outs = [
        (D_CONV, F32), (D_LRU, F32), (D_LRU, F32), (Q_PAD, BF16), (Q_PAD, BF16),
        (KV_ROW, F32), (KV_ROW, F32), (KV_ROW, BF16), (KV_ROW, F32), (KV_ROW, BF16),
        (N_KV_HEADS * LANES, F32),
    ]
    return pl.pallas_call(
        _inproj_kernel,
        grid=(n // tm,),
        in_specs=[tok(D_MODEL), _const_spec((D_MODEL, N_PROJ)), tab, tab, tab],
        out_specs=[tok(w) for w, _ in outs],
        out_shape=[jax.ShapeDtypeStruct((n, w), dt) for w, dt in outs],
        compiler_params=_cparams(("parallel",)),
        name="inproj",
    )(x, w_all, cos, s1, s2)


CONV_HALO = 32
CONV_ROWS = 64


def _conv_kernel(u_ref, buf_ref, w_ref, b_ref, g_ref, bb_ref, o_ref, ext_ref):
    t = pl.program_id(1)
    tc = u_ref.shape[0]

    @pl.when(t == 0)
    def _():
        ext_ref[0:CONV_HALO, :] = buf_ref[...]

    ext_ref[CONV_HALO:CONV_HALO + tc, :] = u_ref[...]
    off = CONV_HALO - (CONV_WIDTH - 1)
    rows = min(CONV_ROWS, tc)
    for r0 in range(0, tc, rows):
        acc = jnp.zeros((rows, D_CONV), F32)
        for k in range(CONV_WIDTH):
            acc = acc + ext_ref[r0 + off + k:r0 + off + k + rows, :] * w_ref[k:k + 1, :]
        y = acc + b_ref[...]
        y = _layernorm(y, g_ref[...], bb_ref[...])
        o_ref[r0:r0 + rows, :] = (y * jax.nn.sigmoid(y)).astype(BF16)
    ext_ref[0:CONV_HALO, :] = ext_ref[tc:tc + CONV_HALO, :]


def _conv_group(u, buf, cw, tc):
    b, t, _ = u.shape
    return pl.pallas_call(
        _conv_kernel,
        grid=(b, t // tc),
        in_specs=[
            pl.BlockSpec((None, tc, D_CONV), lambda i, j: (i, j, 0)),
            pl.BlockSpec((None, CONV_HALO, D_CONV), lambda i, j: (i, 0, 0)),
            _const_spec((CONV_HALO, D_CONV)),
            _const_spec((1, D_CONV)), _const_spec((1, D_CONV)), _const_spec((1, D_CONV)),
        ],
        out_specs=pl.BlockSpec((None, tc, D_CONV), lambda i, j: (i, j, 0)),
        out_shape=jax.ShapeDtypeStruct((b, t, D_CONV), BF16),
        scratch_shapes=[pltpu.VMEM((tc + CONV_HALO, D_CONV), F32)],
        compiler_params=_cparams(("parallel", "arbitrary")),
        name="conv_group",
    )(u, buf, cw["w"], cw["b"], cw["ln_g"], cw["ln_b"])


LRU_HALO = 8


def _lru_kernel(last_row, x_ref, gate_ref, buf_ref, h0_ref, cw_ref, cb_ref, wr_ref, wi_ref,
                bg_ref, lam_ref, y_ref, hl_ref, ext_ref, hc_ref):
    t = pl.program_id(1)
    tl = x_ref.shape[0]

    @pl.when(t == 0)
    def _():
        ext_ref[0:LRU_HALO, :] = buf_ref[...]
        hc_ref[...] = h0_ref[...]

    ext_ref[LRU_HALO:LRU_HALO + tl, :] = x_ref[...]
    off = LRU_HALO - (LRU_CONV_WIDTH - 1)
    xl = jnp.zeros((tl, D_LRU), F32)
    for k in range(LRU_CONV_WIDTH):
        xl = xl + ext_ref[off + k:off + k + tl, :] * cw_ref[k:k + 1, :]
    xl = xl + cb_ref[...]
    ext_ref[0:LRU_HALO, :] = ext_ref[tl:tl + LRU_HALO, :]

    xb = xl.astype(BF16)
    r_gate = jax.nn.sigmoid(_dot(xb, wr_ref[...]) + bg_ref[0:1, :])
    i_gate = jax.nn.sigmoid(_dot(xb, wi_ref[...]) + bg_ref[1:2, :])
    log_a = LRU_C * r_gate * jax.nn.log_sigmoid(lam_ref[...])
    a = jnp.exp(log_a)
    bv = jnp.sqrt(-jnp.tanh(log_a) * (a * a + 1.0)) * (i_gate * xl)

    row = lax.broadcasted_iota(I32, (tl, D_LRU), 0)
    s = 1
    while s < tl:
        keep = row >= s
        a_sh = jnp.where(keep, pltpu.roll(a, s, 0), 1.0)
        b_sh = jnp.where(keep, pltpu.roll(bv, s, 0), 0.0)
        bv = a * b_sh + bv
        a = a * a_sh
        s *= 2
    h = a * hc_ref[0:1, :] + bv
    hc_ref[...] = jnp.broadcast_to(h[last_row:last_row + 1, :], hc_ref.shape)
    hl_ref[...] = hc_ref[...]
    y_ref[...] = (h * jax.nn.gelu(gate_ref[...])).astype(BF16)


def _lru_group(x, gate, buf, h0, lw, tl, last_row):
    b, t, _ = x.shape
    assert last_row == tl - 1 or t == tl
    seq = pl.BlockSpec((None, tl, D_LRU), lambda i, j: (i, j, 0))
    per_b = pl.BlockSpec((None, LRU_HALO, D_LRU), lambda i, j: (i, 0, 0))
    return pl.pallas_call(
        functools.partial(_lru_kernel, last_row),
        grid=(b, t // tl),
        in_specs=[seq, seq, per_b, per_b,
                  _const_spec((LRU_HALO, D_LRU)), _const_spec((1, D_LRU)),
                  _const_spec((D_LRU, D_LRU)), _const_spec((D_LRU, D_LRU)),
                  _const_spec((2, D_LRU)), _const_spec((1, D_LRU))],
        out_specs=[seq, per_b],
        out_shape=[jax.ShapeDtypeStruct((b, t, D_LRU), BF16),
                   jax.ShapeDtypeStruct((b, LRU_HALO, D_LRU), F32)],
        scratch_shapes=[pltpu.VMEM((tl + LRU_HALO, D_LRU), F32),
                        pltpu.VMEM((LRU_HALO, D_LRU), F32)],
        compiler_params=_cparams(("parallel", "arbitrary")),
        name="lru_group",
    )(x, gate, buf, h0, lw["cw"], lw["cb"], lw["wr"], lw["wi"], lw["bg"], lw["lam"])


def _compress_rows(row_refs, pe_ref, w1_ref, b1_ref, w2_ref, b2_ref, carry_ref):
    lo = lax.broadcasted_iota(I32, (1, LANES), 1) < HEAD_DIM
    xs = [[] for _ in range(4)]
    for halves in row_refs:
        n = halves[0].shape[0] // CMP_STRIDE
        cols = [[] for _ in range(4)]
        for rp in range(CMP_STRIDE // 2):
            for half, ref in enumerate(halves):
                pa = ref[pl.ds(2 * rp, n, stride=CMP_STRIDE), :]
                pb = ref[pl.ds(2 * rp + 1, n, stride=CMP_STRIDE), :]
                cols[2 * half].append(jnp.where(lo, pa, pltpu.roll(pb, HEAD_DIM, 1)))
                cols[2 * half + 1].append(jnp.where(lo, pltpu.roll(pa, HEAD_DIM, 1), pb))
        for g in range(4):
            xs[g].append(jnp.concatenate(cols[g], axis=1))
    out = None
    for g in range(4):
        sidx = g // 2
        x = jnp.concatenate(xs[g], axis=0)
        p0 = _dot((x + pe_ref[0, g:g + 1, :]).astype(BF16), w1_ref[sidx, 0])
        p1 = _dot((x + pe_ref[1, g:g + 1, :]).astype(BF16), w1_ref[sidx, 1])
        n_tot = p0.shape[0]
        row = lax.broadcasted_iota(I32, p0.shape, 0)
        p0s = jnp.where(row == 0, carry_ref[g, 0:1, :], pltpu.roll(p0, 1, 0))
        carry_ref[g, 0:1, :] = p0[n_tot - 1:n_tot, :]
        h = (b1_ref[sidx:sidx + 1, :] + p0s) + p1
        part = _dot(jax.nn.gelu(h).astype(BF16), w2_ref[g])
        out = part if out is None else out + part
    return out + b2_ref[...]


def _cmp_weight_specs():
    return [
        _const_spec((2, 4, CMP_STRIDE * HEAD_DIM)),
        _const_spec((2, 2, CMP_STRIDE * HEAD_DIM, CMP_HIDDEN)),
        _const_spec((2, CMP_HIDDEN)),
        _const_spec((4, CMP_HIDDEN, KV_ROW)),
        _const_spec((1, KV_ROW)),
    ]


def _cmp_weight_args(cw):
    return (cw["pe"], cw["w1"], cw["b1"], cw["w2"], cw["b2"])


CMP_TILE_ROWS = 2048


def _compress_prompt_kernel(k_ref, v_ref, pe_ref, w1_ref, b1_ref, w2_ref, b2_ref, kc_ref, vct_ref, carry_ref):
    @pl.when(pl.program_id(1) == 0)
    def _():
        carry_ref[...] = jnp.zeros(carry_ref.shape, F32)

    out = _compress_rows([(k_ref, v_ref)], pe_ref, w1_ref, b1_ref, w2_ref, b2_ref, carry_ref)
    kc_ref[...] = out[:, 0:LANES].astype(BF16)
    vct_ref[...] = out[:, LANES:KV_ROW].T.astype(BF16)


def _compress_prompt(kvc, cw):
    b, t, _ = kvc.shape
    n_e = CMP_TILE_ROWS // CMP_STRIDE
    return pl.pallas_call(
        _compress_prompt_kernel,
        grid=(b, t // CMP_TILE_ROWS),
        in_specs=[pl.BlockSpec((None, CMP_TILE_ROWS, LANES), lambda i, j: (i, j, 0)),
                  pl.BlockSpec((None, CMP_TILE_ROWS, LANES), lambda i, j: (i, j, 1))]
        + _cmp_weight_specs(),
        out_specs=[pl.BlockSpec((None, n_e, LANES), lambda i, j: (i, j, 0)),
                   pl.BlockSpec((None, LANES, n_e), lambda i, j: (i, 0, j))],
        out_shape=[jax.ShapeDtypeStruct((b, t // CMP_STRIDE, LANES), BF16),
                   jax.ShapeDtypeStruct((b, LANES, t // CMP_STRIDE), BF16)],
        scratch_shapes=[pltpu.VMEM((4, 8, CMP_HIDDEN), F32)],
        compiler_params=_cparams(("parallel", "arbitrary")),
        name="compress_prompt",
    )(kvc, kvc, *_cmp_weight_args(cw))


def _selection_scores(pk_ref, n_sel):
    ratio = SEL_BLOCK // CMP_STRIDE
    slc = pk_ref[pl.ds(0, n_sel, stride=ratio), :]
    for o in range(1, ratio):
        slc = slc + 2.0 * pk_ref[pl.ds(o, n_sel, stride=ratio), :]
    return slc + pk_ref[pl.ds(ratio, n_sel, stride=ratio), :]


def _topk_rounds(score, n_rows):
    j = lax.broadcasted_iota(I32, score.shape, 0).astype(F32)
    sel = jnp.zeros(score.shape, F32)
    picks = []
    for _ in range(N_SEL):
        cm = jnp.max(score, axis=0, keepdims=True)
        mi = jnp.min(jnp.where(score == cm, j, float(n_rows)), axis=0, keepdims=True)
        hit = j == mi
        sel = jnp.where(hit, 1.0, sel)
        score = jnp.where(hit, -jnp.inf, score)
        picks.append(mi)
    return sel, picks


def _masked_softmax_rows(s_t, valid):
    s_t = jnp.where(valid, s_t, -jnp.inf)
    m = jnp.max(s_t, axis=0, keepdims=True)
    m = jnp.where(m > -jnp.inf, m, 0.0)
    e = jnp.exp(s_t - m)
    d = jnp.sum(e, axis=0, keepdims=True)
    return e / jnp.where(d > 0, d, 1.0)


CMP_ENTRY_CHUNK = 128


def _cmp_topk_prompt_kernel(qc_ref, kc_ref, vct_ref, ocmp_ref, mneg_ref, pk0_ref, pk1_ref):
    i = pl.program_id(1)
    n_e = kc_ref.shape[0]
    n_sel = n_e // (SEL_BLOCK // CMP_STRIDE)
    pk_refs = (pk0_ref, pk1_ref)

    def attend(n_use):
        kc = kc_ref[0:n_use, :]
        vct = vct_ref[:, 0:n_use]
        e_idx = lax.broadcasted_iota(I32, (n_use, Q_BLOCK), 0)
        qpos = i * Q_BLOCK + lax.broadcasted_iota(I32, (n_use, Q_BLOCK), 1)
        valid = (e_idx >= 1) & (CMP_STRIDE * e_idx + (CMP_STRIDE - 1) <= qpos)
        for kv in range(N_KV_HEADS):
            pkv = jnp.zeros((n_use, Q_BLOCK), F32)
            for h in range(kv * GROUP, (kv + 1) * GROUP):
                rows = slice(h * PAD_HEAD, (h + 1) * PAD_HEAD)
                p = _masked_softmax_rows(_dot(kc, qc_ref[rows, :]), valid)
                pkv = pkv + p
                ocmp_ref[rows, :] = _dot(vct, p.astype(BF16))
            pk_refs[kv][0:n_use, :] = pkv
            pk_refs[kv][n_use:, :] = jnp.zeros((n_e + 8 - n_use, Q_BLOCK), F32)

    n_var = n_e // CMP_ENTRY_CHUNK
    variant = jnp.minimum((8 * i + 7) // CMP_ENTRY_CHUNK, n_var - 1)
    for k in range(n_var):
        pl.when(variant == k)(functools.partial(attend, CMP_ENTRY_CHUNK * (k + 1)))

    j = lax.broadcasted_iota(I32, (n_sel, Q_BLOCK), 0)
    qp = i * Q_BLOCK + lax.broadcasted_iota(I32, (n_sel, Q_BLOCK), 1)
    qblk = jnp.right_shift(qp, 6)
    forced = (j == 0) | (j == qblk) | (j == qblk - 1)
    scores = [jnp.where(forced, jnp.inf, jnp.where(j * SEL_BLOCK <= qp, _selection_scores(r, n_sel), -jnp.inf))
              for r in pk_refs]
    sel, _ = _topk_rounds(jnp.concatenate(scores, axis=1), n_sel)
    mneg = jnp.where(sel > 0, 0.0, MASK_NEG).astype(BF16)
    for kv in range(N_KV_HEADS):
        mneg_ref[kv] = mneg[:, kv * Q_BLOCK:(kv + 1) * Q_BLOCK]


def _cmp_topk_prompt(qc_t, kc, vc_t):
    b, _, t = qc_t.shape
    n_e = kc.shape[1]
    n_sel = t // SEL_BLOCK
    assert n_e % CMP_ENTRY_CHUNK == 0
    qblk = pl.BlockSpec((None, Q_PAD, Q_BLOCK), lambda bi, i: (bi, 0, i))
    return pl.pallas_call(
        _cmp_topk_prompt_kernel,
        grid=(b, t // Q_BLOCK),
        in_specs=[
            qblk,
            pl.BlockSpec((None, n_e, LANES), lambda bi, i: (bi, 0, 0)),
            pl.BlockSpec((None, LANES, n_e), lambda bi, i: (bi, 0, 0)),
        ],
        out_specs=[
            qblk,
            pl.BlockSpec((None, N_KV_HEADS, n_sel, Q_BLOCK), lambda bi, i: (bi, 0, 0, i)),
        ],
        out_shape=[jax.ShapeDtypeStruct((b, Q_PAD, t), F32),
                   jax.ShapeDtypeStruct((b, N_KV_HEADS, n_sel, t), BF16)],
        scratch_shapes=[pltpu.VMEM((n_e + 8, Q_BLOCK), F32), pltpu.VMEM((n_e + 8, Q_BLOCK), F32)],
        compiler_params=_cparams(("parallel", "arbitrary")),
        name="cmp_topk_prompt",
    )(qc_t, kc, vc_t)


def _selwin_prompt_kernel(qr_ref, mneg_ref, ks_ref, vs_ref, kw_ref, vw_ref, ocmp_ref, gt_ref, o_ref,
                          qa_ref, sa_ref, sb_ref, pa_ref, pb_ref):
    i = pl.program_id(1)
    n_sel = mneg_ref.shape[1]
    cols = N_HEADS * Q_BLOCK
    kv_cols = GROUP * Q_BLOCK
    for h in range(N_HEADS):
        c = slice(h * Q_BLOCK, (h + 1) * Q_BLOCK)
        qa_ref[0:PAD_HEAD, c] = qr_ref[h * PAD_HEAD:(h + 1) * PAD_HEAD, :]
        qa_ref[PAD_HEAD:PAD_HEAD + n_sel, c] = mneg_ref[h // GROUP]
    qa = qa_ref[...]
    q0 = i * Q_BLOCK
    qpos = q0 + (lax.broadcasted_iota(I32, (1, cols), 1) & (Q_BLOCK - 1))

    tile = SEL_TILE
    key_col = lax.broadcasted_iota(I32, (tile, 1), 0)

    def scores(t):
        return _dot(ks_ref[pl.ds(pl.multiple_of(t * tile, tile), tile), :], qa)

    def weighted_values(v_ref, first_tile, n_tiles, p):
        outs = []
        for kv in range(N_KV_HEADS):
            vk = v_ref[kv, pl.ds(first_tile, n_tiles)]
            vt = jnp.concatenate([vk[c] for c in range(n_tiles)], axis=1)
            outs.append(_dot(vt, p[:, kv * kv_cols:(kv + 1) * kv_cols]))
        return jnp.concatenate(outs, axis=1)

    def normalise(acc):
        sums = [acc[SUM_ROWS[kv]:SUM_ROWS[kv] + 1, kv * kv_cols:(kv + 1) * kv_cols] for kv in range(N_KV_HEADS)]
        return acc / jnp.concatenate(sums, axis=1)

    start = pl.multiple_of(jnp.maximum(q0 + Q_BLOCK - WIN_KEYS, 0), Q_BLOCK)
    s = _dot(kw_ref[pl.ds(start, WIN_KEYS), :], qa[0:PAD_HEAD, :])
    dpos = qpos - (start + lax.broadcasted_iota(I32, (WIN_KEYS, 1), 0))
    s = jnp.where((dpos >= 0) & (dpos <= WINDOW), s, -jnp.inf)
    p = jnp.exp2(s - jnp.max(s, axis=0, keepdims=True))
    o_win = normalise(weighted_values(vw_ref, start // LANES, WIN_KEYS // LANES, p.astype(BF16)))

    sub = tile // LANES

    def stage(t, cur, nxt, carry, masked, prefetch):
        m, acc, alpha_prev = carry
        if prefetch:
            s_refs[nxt][...] = scores(t + 1)
        acc = alpha_prev * acc + weighted_values(vs_ref, jnp.maximum(t - 1, 0) * sub, sub, p_refs[nxt][...])
        s = s_refs[cur][...]
        if masked:
            s = jnp.where(t * tile + key_col <= qpos, s, MASK_NEG)
        m_new = jnp.maximum(m, jnp.max(s, axis=0, keepdims=True))
        alpha = jnp.exp2(m - m_new)
        p_refs[cur][...] = jnp.exp2(s - m_new).astype(BF16)
        return m_new, acc, alpha

    s_refs = (sa_ref, sb_ref)
    p_refs = (pa_ref, pb_ref)
    sa_ref[...] = scores(0)
    pb_ref[...] = jnp.zeros(pb_ref.shape, BF16)
    init = (jnp.full((1, cols), MASK_NEG, F32), jnp.zeros((LANES, cols), F32), jnp.ones((1, cols), F32))

    def pair(u, carry):
        carry = stage(2 * u, 0, 1, carry, False, True)
        return stage(2 * u + 1, 1, 0, carry, False, True)

    u_diag = q0 // (2 * tile)
    carry = lax.fori_loop(0, u_diag, pair, init)
    carry = stage(2 * u_diag, 0, 1, carry, True, True)

    def finish_second(c):
        _, acc, alpha = stage(2 * u_diag + 1, 1, 0, c, True, False)
        return alpha * acc + weighted_values(vs_ref, (2 * u_diag + 1) * sub, sub, pb_ref[...])

    def finish_first(c):
        _, acc, alpha = c
        return alpha * acc + weighted_values(vs_ref, 2 * u_diag * sub, sub, pa_ref[...])

    o_sel = normalise(lax.cond(q0 - 2 * u_diag * tile >= tile, finish_second, finish_first, carry))

    for h in range(N_HEADS):
        c = slice(h * Q_BLOCK, (h + 1) * Q_BLOCK)
        rows = slice(h * PAD_HEAD, (h + 1) * PAD_HEAD)
        g0 = (h // GROUP) * LANES + 3 * (h % GROUP)
        o = (gt_ref[g0:g0 + 1, :] * ocmp_ref[rows, :] + gt_ref[g0 + 1:g0 + 2, :] * o_sel[:, c]
             + gt_ref[g0 + 2:g0 + 3, :] * o_win[:, c])
        o_ref[rows, :] = o.astype(BF16)


def _selwin_prompt(qr_t, mneg, ks, vs_t, kw, vw_t, ocmp_t, gates_t):
    b, _, t = qr_t.shape
    n_sel = mneg.shape[2]
    assert t % (2 * SEL_TILE) == 0 and t >= WIN_KEYS
    cols = N_HEADS * Q_BLOCK
    per_q = lambda r: pl.BlockSpec((None, r, Q_BLOCK), lambda bi, i: (bi, 0, i))
    per_b = lambda *shape: pl.BlockSpec((None,) + shape, lambda bi, i: (bi,) + (0,) * len(shape),
                                        pipeline_mode=pl.Buffered(1))
    vals = per_b(N_KV_HEADS, t // LANES, LANES, LANES)
    return pl.pallas_call(
        _selwin_prompt_kernel,
        grid=(b, t // Q_BLOCK),
        in_specs=[
            per_q(Q_PAD),
            pl.BlockSpec((None, N_KV_HEADS, n_sel, Q_BLOCK), lambda bi, i: (bi, 0, 0, i)),
            per_b(t, LANES + n_sel), vals, per_b(t, LANES), vals,
            per_q(Q_PAD), per_q(N_KV_HEADS * LANES),
        ],
        out_specs=per_q(Q_PAD),
        out_shape=jax.ShapeDtypeStruct((b, Q_PAD, t), BF16),
        scratch_shapes=[pltpu.VMEM((PAD_HEAD + n_sel, cols), BF16),
                        pltpu.VMEM((SEL_TILE, cols), F32),
                        pltpu.VMEM((SEL_TILE, cols), F32),
                        pltpu.VMEM((SEL_TILE, cols), BF16),
                        pltpu.VMEM((SEL_TILE, cols), BF16)],
        compiler_params=_cparams(("parallel", "arbitrary")),
        name="selwin_prompt",
    )(qr_t, mneg, ks, vs_t, kw, vw_t, ocmp_t, gates_t)


PAGES_PER_STEP = 16
S_COLS = LANES


def _cmp_topk_sample_kernel(n_t, pt_ref, *refs):
    page_refs = refs[:PAGES_PER_STEP]
    (q_ref, pe_ref, w1_ref, b1_ref, w2_ref, b2_ref,
     ocmp_ref, idx_ref, kcv_ref, pk_ref, carry_ref, xk_ref, xv_ref) = refs[PAGES_PER_STEP:]
    s = pl.program_id(1)
    n_steps = pl.num_programs(1)
    n_e = kcv_ref.shape[0]
    step_e = PAGES_PER_STEP * PAGE_SIZE // CMP_STRIDE

    @pl.when(s == 0)
    def _():
        carry_ref[...] = jnp.zeros(carry_ref.shape, F32)

    for k, page in enumerate(page_refs):
        rows = slice(k * PAGE_SIZE, (k + 1) * PAGE_SIZE)
        xk_ref[rows, :] = page[0:LANES, :].T
        xv_ref[rows, :] = page[LANES:KV_ROW, :].T
    out = _compress_rows([(xk_ref, xv_ref)], pe_ref, w1_ref, b1_ref, w2_ref, b2_ref, carry_ref)
    kcv_ref[pl.ds(pl.multiple_of(s * step_e, step_e), step_e), :] = out.astype(BF16)

    @pl.when(s == n_steps - 1)
    def _():
        n_sel = n_e // (SEL_BLOCK // CMP_STRIDE) + 1
        n_sel_rows = pk_ref.shape[0] // (SEL_BLOCK // CMP_STRIDE) - 2
        kc = kcv_ref[:, 0:LANES]
        vc = kcv_ref[:, LANES:KV_ROW]
        e_idx = lax.broadcasted_iota(I32, (n_e, S_COLS), 0)
        p = _masked_softmax_rows(_dot_nt(kc, q_ref[...]), e_idx >= 1)
        ocmp_ref[...] = _dot_tn(p.astype(BF16), vc)
        pkv = p
        for hh in range(1, GROUP):
            pkv = pkv + pltpu.roll(p, S_COLS - n_t * hh, 1)
        pk_ref[0:n_e, :] = pkv
        pk_ref[n_e:, :] = jnp.zeros((pk_ref.shape[0] - n_e, S_COLS), F32)
        slc = _selection_scores(pk_ref, n_sel_rows)
        j = lax.broadcasted_iota(I32, (n_sel_rows, S_COLS), 0)
        qp = PAST_LEN + (lax.broadcasted_iota(I32, (n_sel_rows, S_COLS), 1) & (n_t - 1))
        qblk = jnp.right_shift(qp, 6)
        forced = (j == 0) | (j == qblk) | (j == qblk - 1)
        in_range = j < n_sel
        score = jnp.where(forced & in_range, jnp.inf,
                          jnp.where((j * SEL_BLOCK <= qp) & in_range, slc, -jnp.inf))
        _, picks = _topk_rounds(score, n_sel_rows)
        for r, mi in enumerate(picks):
            idx_ref[r:r + 1, :] = mi.astype(I32)


def _cmp_topk_sample(page_table, cache_cmp_t, layer, q_cols, cw, n_t):
    assert n_t & (n_t - 1) == 0 and N_HEADS * n_t <= S_COLS
    b, n_pages = page_table.shape
    n_e = n_pages * PAGE_SIZE // CMP_STRIDE
    n_sel_rows = ((n_e // 4 + 1) + 7) // 8 * 8
    pk_rows = 4 * (n_sel_rows + 2)
    n_steps = n_pages // PAGES_PER_STEP
    step_rows = PAGES_PER_STEP * PAGE_SIZE

    def page_spec(k):
        return pl.BlockSpec((None, None, KV_ROW, PAGE_SIZE),
                            lambda bi, s, pt: (layer, pt[bi, s * PAGES_PER_STEP + k], 0, 0))

    per_b = lambda rows, w: pl.BlockSpec((None, rows, w), lambda bi, s, pt: (bi, 0, 0))
    grid_spec = pltpu.PrefetchScalarGridSpec(
        num_scalar_prefetch=1,
        grid=(b, n_steps),
        in_specs=[page_spec(k) for k in range(PAGES_PER_STEP)]
        + [per_b(S_COLS, LANES)] + _cmp_weight_specs(),
        out_specs=[per_b(S_COLS, LANES), per_b(N_SEL, S_COLS)],
        scratch_shapes=[pltpu.VMEM((n_e, KV_ROW), BF16),
                        pltpu.VMEM((pk_rows, S_COLS), F32),
                        pltpu.VMEM((4, 8, CMP_HIDDEN), F32),
                        pltpu.VMEM((step_rows, LANES), F32),
                        pltpu.VMEM((step_rows, LANES), F32)],
    )
    return pl.pallas_call(
        functools.partial(_cmp_topk_sample_kernel, n_t),
        grid_spec=grid_spec,
        out_shape=[jax.ShapeDtypeStruct((b, S_COLS, LANES), F32),
                   jax.ShapeDtypeStruct((b, N_SEL, S_COLS), I32)],
        compiler_params=_cparams(("parallel", "arbitrary")),
        name="cmp_topk_sample",
    )(page_table, *([cache_cmp_t] * PAGES_PER_STEP), q_cols, *_cmp_weight_args(cw))


HROWS = 8


def _selwin_sample_kernel(idx_ref, pt_ref, *refs):
    page_refs = refs[:N_SEL]
    (q_ref, new_s_ref, win_ref, new_w_ref, ocmp_ref, gt_ref, o_ref, k_ref, v_ref) = refs[N_SEL:]
    bi = pl.program_id(0)
    tq = pl.program_id(1)
    kh = pl.program_id(2)
    n_t = pl.num_programs(1)
    qpos = PAST_LEN + tq
    q = q_ref[...]
    base = ((bi * n_t + tq) * N_KV_HEADS + kh) * N_SEL
    new_block = PAST_LEN // SEL_BLOCK

    valid_parts = []
    lane = lax.broadcasted_iota(I32, (1, PAGE_SIZE), 1)
    for r in range(N_SEL):
        j = idx_ref[base + r]
        is_new = j == new_block
        cols = slice(r * PAGE_SIZE, (r + 1) * PAGE_SIZE)

        @pl.when(is_new)
        def _():
            k_ref[:, cols] = new_s_ref[0:LANES, :].astype(BF16)
            v_ref[:, cols] = new_s_ref[LANES:KV_ROW, :].astype(BF16)

        @pl.when(jnp.logical_not(is_new))
        def _():
            k_ref[:, cols] = page_refs[r][0:LANES, :].astype(BF16)
            v_ref[:, cols] = page_refs[r][LANES:KV_ROW, :].astype(BF16)

        kpos = jnp.right_shift(j, 1) * PAGE_SIZE + lane
        valid_parts.append((jnp.right_shift(kpos, 6) == j) & (kpos <= qpos))
    valid = jnp.concatenate(valid_parts, axis=1)
    s = jnp.where(valid, _dot(q, k_ref[...]), -jnp.inf)
    m = jnp.max(s, axis=-1, keepdims=True)
    p = jnp.exp(s - m)
    o_sel = _dot_nt(p.astype(BF16), v_ref[...]) / jnp.sum(p, axis=-1, keepdims=True)

    wb = win_ref.shape[1]
    s_old = _dot(q, win_ref[0:LANES, :].astype(BF16))
    d_old = qpos - (PAST_LEN - wb + lax.broadcasted_iota(I32, (1, wb), 1))
    s_old = jnp.where((d_old >= 0) & (d_old <= WINDOW), s_old, -jnp.inf)
    s_new = _dot(q, new_w_ref[0:LANES, :].astype(BF16))
    d_new = tq - lax.broadcasted_iota(I32, (1, new_w_ref.shape[1]), 1)
    s_new = jnp.where((d_new >= 0) & (d_new <= WINDOW), s_new, -jnp.inf)
    m = jnp.maximum(jnp.max(s_old, axis=-1, keepdims=True), jnp.max(s_new, axis=-1, keepdims=True))
    p_old = jnp.exp(s_old - m)
    p_new = jnp.exp(s_new - m)
    den = jnp.sum(p_old, axis=-1, keepdims=True) + jnp.sum(p_new, axis=-1, keepdims=True)
    o_win = (_dot_nt(p_old.astype(BF16), win_ref[LANES:KV_ROW, :].astype(BF16))
             + _dot_nt(p_new.astype(BF16), new_w_ref[LANES:KV_ROW, :].astype(BF16))) / den

    g = gt_ref[...]
    o = g[:, 0:1] * ocmp_ref[...] + g[:, 1:2] * o_sel + g[:, 2:3] * o_win
    o_ref[...] = o.astype(BF16)


def _selwin_sample(idx_flat, page_table, cache_sel_t, win_t, layer, q_rows, new_s_t, new_w_t, ocmp_rows, gate_rows):
    b, n_t = q_rows.shape[:2]
    wb = win_t.shape[-1]
    n_t_static = n_t
    last_page = page_table.shape[1] - 1

    def page_spec(r):
        def imap(bi, tq, kh, idx, pt):
            j = idx[((bi * n_t_static + tq) * N_KV_HEADS + kh) * N_SEL + r]
            return (layer, pt[bi, jnp.minimum(jnp.right_shift(j, 1), last_page)], 0, 0)
        return pl.BlockSpec((None, None, KV_ROW, PAGE_SIZE), imap)

    row5 = pl.BlockSpec((None, None, None, HROWS, LANES), lambda bi, tq, kh, idx, pt: (bi, tq, kh, 0, 0))
    new_rows = pl.BlockSpec((None, KV_ROW, LANES), lambda bi, tq, kh, idx, pt: (bi, 0, 0))
    grid_spec = pltpu.PrefetchScalarGridSpec(
        num_scalar_prefetch=2,
        grid=(b, n_t, N_KV_HEADS),
        in_specs=[page_spec(r) for r in range(N_SEL)]
        + [row5, new_rows,
           pl.BlockSpec((None, None, KV_ROW, wb), lambda bi, tq, kh, idx, pt: (layer, bi, 0, 0)),
           new_rows, row5, row5],
        out_specs=row5,
        scratch_shapes=[pltpu.VMEM((LANES, N_SEL * PAGE_SIZE), BF16),
                        pltpu.VMEM((LANES, N_SEL * PAGE_SIZE), BF16)],
    )
    return pl.pallas_call(
        _selwin_sample_kernel,
        grid_spec=grid_spec,
        out_shape=jax.ShapeDtypeStruct((b, n_t, N_KV_HEADS, HROWS, LANES), BF16),
        compiler_params=_cparams(("arbitrary", "arbitrary", "arbitrary")),
        name="selwin_sample",
    )(idx_flat, page_table, *([cache_sel_t] * N_SEL), q_rows, new_s_t, win_t, new_w_t, ocmp_rows, gate_rows)


def _outproj_ln_kernel(attn_transposed, x_ref, yc_ref, yl_ref, ya_ref, w_ref, g_ref, b_ref,
                       wg_ref, wu_ref, wo_ref, g2_ref, b2_ref, o_ref):
    y = _dot(yc_ref[...], w_ref[0:D_CONV, :])
    y = y + _dot(yl_ref[...], w_ref[D_CONV:D_CONV + D_LRU, :])
    w_attn = w_ref[D_CONV + D_LRU:, :]
    y = y + (_dot_tn(ya_ref[...], w_attn) if attn_transposed else _dot(ya_ref[...], w_attn))
    x = _layernorm(ALPHA * x_ref[...] + y, g_ref[...], b_ref[...])
    o_ref[...] = _ffn_ln_apply(x, wg_ref, wu_ref, wo_ref, g2_ref, b2_ref)


def _outproj_ffn(x, yc, yl, ya, ow, fw, tm):
    n = x.shape[0]
    tok = lambda w: pl.BlockSpec((tm, w), lambda i: (i, 0))
    attn_transposed = ya.ndim == 3
    if attn_transposed:
        tps = ya.shape[2] // tm
        ya_spec = pl.BlockSpec((None, Q_PAD, tm), lambda i: (i // tps, 0, i % tps))
    else:
        ya_spec = tok(Q_PAD)
    return pl.pallas_call(
        functools.partial(_outproj_ln_kernel, attn_transposed),
        grid=(n // tm,),
        in_specs=[tok(D_MODEL), tok(D_CONV), tok(D_LRU), ya_spec,
                  _const_spec((D_CONV + D_LRU + Q_PAD, D_MODEL)),
                  _const_spec((1, D_MODEL)), _const_spec((1, D_MODEL))] + _ffn_weight_specs(),
        out_specs=tok(D_MODEL),
        out_shape=jax.ShapeDtypeStruct((n, D_MODEL), F32),
        compiler_params=_cparams(("parallel",)),
        name="outproj_ffn",
    )(x, yc, yl, ya, ow["w"], ow["g"], ow["b"], *_ffn_weight_args(fw))


def _rope_tables(pos):
    half = ROPE_DIM // 2
    inv = ROPE_THETA ** (-jnp.arange(half, dtype=F32) / half)
    ang = pos.astype(F32)[:, None] * inv[None, :]
    cos, sin = jnp.cos(ang), jnp.sin(ang)
    n = pos.shape[0]
    rest = HEAD_DIM - ROPE_DIM
    zeros8 = jnp.zeros((n, half), F32)
    c = jnp.concatenate([cos, cos, jnp.ones((n, rest), F32)], axis=1)
    s1 = jnp.concatenate([zeros8, sin, jnp.zeros((n, rest), F32)], axis=1)
    s2 = jnp.concatenate([-sin, zeros8, jnp.zeros((n, rest), F32)], axis=1)
    rep = LANES // HEAD_DIM
    return tuple(jnp.tile(a, (1, rep)) for a in (c, s1, s2))


def _head_pad_index():
    h = np.arange(D_ATTN) // HEAD_DIM
    d = np.arange(D_ATTN) % HEAD_DIM
    return h * PAD_HEAD + (h // GROUP) * HEAD_DIM + d


def _prep_layer(l, ln_g, ln_b, ffn_w_in, ffn_w_out, w_in, conv_w, conv_b, conv_ln_g, conv_ln_b,
                lru_conv_w, lru_conv_b, lru_w_gate, lru_b_gate, lru_lambda,
                cmp_pe, cmp_w1, cmp_b1, cmp_w2, cmp_b2, w_out):
    row = lambda v: v.reshape(1, -1).astype(F32)
    ffn = []
    for f, ln_i in ((0, 0), (1, 2)):
        wi = ffn_w_in[l, f]
        wg = wi[:, :D_FF].reshape(D_MODEL, N_FF_CHUNKS, FF_CHUNK).transpose(1, 0, 2).astype(BF16)
        wu = wi[:, D_FF:].reshape(D_MODEL, N_FF_CHUNKS, FF_CHUNK).transpose(1, 0, 2).astype(BF16)
        wo = ffn_w_out[l, f].reshape(N_FF_CHUNKS, FF_CHUNK, D_MODEL).astype(BF16)
        ffn.append({"wg": wg, "wu": wu, "wo": wo, "g": row(ln_g[l, ln_i]), "b": row(ln_b[l, ln_i])})

    wl = w_in[l]
    o_q = 2 * D_CONV + 2 * D_LRU
    o_kv = o_q + D_ATTN
    o_g = o_kv + 3 * KV_ROW
    pad_idx = _head_pad_index()
    wq = jnp.zeros((D_MODEL, Q_PAD), F32).at[:, pad_idx].set(wl[:, o_q:o_kv])
    hh = np.arange(3 * N_HEADS) // 3
    gate_idx = (hh // GROUP) * LANES + (hh % GROUP) * 3 + np.arange(3 * N_HEADS) % 3
    wgt = jnp.zeros((D_MODEL, N_KV_HEADS * LANES), F32).at[:, gate_idx].set(wl[:, o_g:])
    w_all = jnp.concatenate([wl[:, :o_q], wq, wl[:, o_kv:o_g], wgt], axis=1).astype(BF16)
    w_t = jnp.concatenate([w_all[:, C_Q:C_KVC], w_all[:, C_KVS:C_GATE], w_all[:, C_KVC:C_KVS],
                           w_all[:, C_GATE:]], axis=1).T

    conv = {"w": jnp.pad(conv_w[l], ((0, CONV_HALO - CONV_WIDTH), (0, 0))),
            "b": row(conv_b[l]), "ln_g": row(conv_ln_g[l]), "ln_b": row(conv_ln_b[l])}

    def blockdiag(w):
        out = jnp.zeros((D_LRU, D_LRU), F32)
        for n in range(LRU_BLOCKS):
            out = out.at[n * LRU_BW:(n + 1) * LRU_BW, n * LRU_BW:(n + 1) * LRU_BW].set(w[n])
        return out.astype(BF16)

    lru = {"cw": jnp.pad(lru_conv_w[l], ((0, LRU_HALO - LRU_CONV_WIDTH), (0, 0))),
           "cb": row(lru_conv_b[l]),
           "wr": blockdiag(lru_w_gate[l, 0]), "wi": blockdiag(lru_w_gate[l, 1]),
           "bg": lru_b_gate[l].astype(F32), "lam": row(lru_lambda[l])}

    pe = cmp_pe[l].reshape(2, 2, CMP_STRIDE * HEAD_DIM)
    pe_rows = jnp.stack([jnp.stack([pe[g // 2, j] for g in range(4)]) for j in range(2)])
    w2e = jnp.zeros((4, CMP_HIDDEN, KV_ROW), F32)
    for g in range(4):
        w2e = w2e.at[g, :, g * HEAD_DIM:(g + 1) * HEAD_DIM].set(cmp_w2[l, g // 2])
    cmp = {"pe": pe_rows.astype(F32),
           "w1": cmp_w1[l].reshape(2, 2, CMP_STRIDE * HEAD_DIM, CMP_HIDDEN).astype(BF16),
           "b1": cmp_b1[l].astype(F32),
           "w2": w2e.astype(BF16),
           "b2": jnp.concatenate([cmp_b2[l, 0], cmp_b2[l, 0], cmp_b2[l, 1], cmp_b2[l, 1]]).reshape(1, -1)}

    wo = w_out[l]
    wo_attn = jnp.zeros((Q_PAD, D_MODEL), F32).at[pad_idx, :].set(wo[D_CONV + D_LRU:])
    out = {"w": jnp.concatenate([wo[:D_CONV + D_LRU], wo_attn], axis=0).astype(BF16),
           "g": row(ln_g[l, 1]), "b": row(ln_b[l, 1])}
    return {"ffn": ffn, "w_all": w_all, "w_t": w_t, "conv": conv, "lru": lru, "cmp": cmp, "out": out}


def _pad_front(a, rows):
    return jnp.pad(a, ((0, 0), (rows - a.shape[1], 0), (0, 0)))


def _kv6(a, lead):
    return a.reshape(lead + (2, N_KV_HEADS, HEAD_DIM))


TM_PROMPT = 512
TC_PROMPT = 512
TL_PROMPT = 256


def _layer_prompt(x, bsz, t, lw, tabs):
    n = bsz * t
    tabs_t = tuple(a.T for a in tabs)
    (x, u, lx, lg, kvc, ks, kw, kvc_t, kvs_t, kvw_t, qc_t, qr_t, vs_t, vw_t, gates_t) = _ffn_inproj_t(
        x, lw["ffn"][0], lw["w_all"], lw["w_t"], tabs, tabs_t, bsz, t, TM_PROMPT)
    s3 = lambda a: a.reshape(bsz, t, a.shape[-1])
    u3, lx3 = s3(u), s3(lx)
    yc = _conv_group(u3, jnp.zeros((bsz, CONV_HALO, D_CONV), F32), lw["conv"], TC_PROMPT)
    yl, h_last = _lru_group(lx3, s3(lg), jnp.zeros((bsz, LRU_HALO, D_LRU), F32),
                            jnp.zeros((bsz, LRU_HALO, D_LRU), F32), lw["lru"], TL_PROMPT, TL_PROMPT - 1)
    kc, vc_t = _compress_prompt(s3(kvc), lw["cmp"])
    ocmp_t, mneg = _cmp_topk_prompt(qc_t, kc, vc_t)
    ya_t = _selwin_prompt(qr_t, mneg, s3(ks), vs_t, s3(kw), vw_t, ocmp_t, gates_t)
    x = _outproj_ffn(x, yc.reshape(n, D_CONV), yl.reshape(n, D_LRU), ya_t, lw["out"], lw["ffn"][1], TM_PROMPT)
    leaf = lambda a: a.reshape(bsz, 2, N_KV_HEADS, HEAD_DIM, a.shape[-1]).transpose(0, 4, 1, 2, 3)
    state = (u3[:, t - (CONV_WIDTH - 1):], lx3[:, t - (LRU_CONV_WIDTH - 1):], h_last[:, 0],
             leaf(kvc_t), leaf(kvs_t), leaf(kvw_t[:, :, t - min(WINDOW, t):]))
    return x, state


T_PAD = 8


def _layer_sample(x, bsz, t, lw, tabs, conv_buf, lru_buf, lru_h, cache_cmp_t, cache_sel_t, win_t, layer, page_table):
    n = bsz * t
    x = _ffn_ln(x, lw["ffn"][0], n)
    (u, lx, lg, qc, qr, kvc, kvs, kvs_b, kvw, kvw_b, gates) = _inproj(x, lw["w_all"], tabs, n, 1)
    s3 = lambda a: a.reshape(bsz, t, a.shape[-1])
    padt = lambda a: jnp.pad(s3(a), ((0, 0), (0, T_PAD - t), (0, 0)))
    yc = _conv_group(padt(u), _pad_front(conv_buf, CONV_HALO), lw["conv"], T_PAD)[:, :t]
    h0 = jnp.broadcast_to(lru_h[:, None, :], (bsz, LRU_HALO, D_LRU))
    yl, h_all = _lru_group(padt(lx), padt(lg), _pad_front(lru_buf, LRU_HALO), h0, lw["lru"], T_PAD, t - 1)
    yl = yl[:, :t]

    def head_cols(a):
        a = a.reshape(bsz, t, N_HEADS, PAD_HEAD).transpose(0, 2, 1, 3).reshape(bsz, N_HEADS * t, PAD_HEAD)
        return jnp.pad(a, ((0, 0), (0, S_COLS - N_HEADS * t), (0, 0)))

    ocmp_cols, picks = _cmp_topk_sample(page_table, cache_cmp_t, layer, head_cols(qc), lw["cmp"], t)
    pk = picks[:, :, :N_HEADS * t].reshape(bsz, N_SEL, N_KV_HEADS, GROUP, t)[:, :, :, 0, :]
    idx_flat = pk.transpose(0, 3, 2, 1).reshape(-1).astype(I32)

    def head_rows(a, dt):
        a = a.reshape(bsz, N_KV_HEADS, GROUP, t, PAD_HEAD).transpose(0, 3, 1, 2, 4)
        return jnp.pad(a, ((0, 0), (0, 0), (0, 0), (0, HROWS - GROUP), (0, 0))).astype(dt)

    q_rows = head_rows(qr.reshape(bsz, t, N_HEADS, PAD_HEAD).transpose(0, 2, 1, 3).reshape(bsz, N_HEADS * t, PAD_HEAD), BF16)
    ocmp_rows = head_rows(ocmp_cols[:, :N_HEADS * t], F32)
    g3 = gates.reshape(bsz, t, N_KV_HEADS, LANES)[..., :3 * GROUP].reshape(bsz, t, N_KV_HEADS, GROUP, 3)
    gate_rows = jnp.pad(g3, ((0, 0), (0, 0), (0, 0), (0, HROWS - GROUP), (0, LANES - 3)))
    pos_last = lambda a: s3(a).transpose(0, 2, 1)
    pad_cols = lambda a: jnp.pad(a, ((0, 0), (0, 0), (0, LANES - t)))
    kvs_t, kvw_t = pos_last(kvs), pos_last(kvw)
    ya_rows = _selwin_sample(idx_flat, page_table, cache_sel_t, win_t, layer, q_rows, pad_cols(kvs_t),
                             pad_cols(kvw_t), ocmp_rows, gate_rows)
    ya = ya_rows[:, :, :, :GROUP].reshape(bsz, t, Q_PAD)

    x = _outproj_ffn(x, yc.reshape(n, D_CONV), yl.reshape(n, D_LRU), ya.reshape(n, Q_PAD), lw["out"], lw["ffn"][1], n)
    new_conv = jnp.concatenate([conv_buf, s3(u)], axis=1)[:, t:]
    new_lru = jnp.concatenate([lru_buf, s3(lx)], axis=1)[:, t:]
    win_all = jnp.concatenate([win_t[layer], kvw_t], axis=-1)
    n_win = min(WINDOW, win_all.shape[-1])
    new_win = win_all[..., win_all.shape[-1] - n_win:]
    new_win = new_win.reshape(bsz, 2, N_KV_HEADS, HEAD_DIM, n_win).transpose(0, 4, 1, 2, 3)
    state = (new_conv, new_lru, h_all[:, 0], _kv6(kvc, (bsz, t)), _kv6(kvs, (bsz, t)), new_win)
    return x, state


def kernel(x_prompt, x_sample, state_conv, state_lru_conv, state_lru_h, cache_cmp_kv, cache_sel_kv, cache_win_kv,
           page_table, ln_g, ln_b, ffn_w_in, ffn_w_out, w_in, conv_w, conv_b, conv_ln_g, conv_ln_b,
           lru_conv_w, lru_conv_b, lru_w_gate, lru_b_gate, lru_lambda, cmp_pe, cmp_w1, cmp_b1, cmp_w2, cmp_b2, w_out):
    bp, tp, _ = x_prompt.shape
    bs, ts, _ = x_sample.shape
    depth = ln_g.shape[0]
    past = page_table.shape[1] * PAGE_SIZE
    assert past == PAST_LEN and past % SEL_BLOCK == 0
    tabs_p = _rope_tables(jnp.arange(tp))
    tabs_s = _rope_tables(jnp.tile(past + jnp.arange(ts), bs))
    xp = x_prompt.reshape(bp * tp, D_MODEL)
    xs = x_sample.reshape(bs * ts, D_MODEL)
    n_pool = cache_cmp_kv.shape[1]
    pos_last = lambda c: jnp.transpose(c, (0, 1, 3, 4, 5, 2)).reshape(c.shape[:2] + (KV_ROW, c.shape[2]))
    cache_cmp_t, cache_sel_t, win_t = pos_last(cache_cmp_kv), pos_last(cache_sel_kv), pos_last(cache_win_kv)
    st_p, st_s = [], []
    for l in range(depth):
        lw = _prep_layer(l, ln_g, ln_b, ffn_w_in, ffn_w_out, w_in, conv_w, conv_b, conv_ln_g, conv_ln_b,
                         lru_conv_w, lru_conv_b, lru_w_gate, lru_b_gate, lru_lambda,
                         cmp_pe, cmp_w1, cmp_b1, cmp_w2, cmp_b2, w_out)
        xp, sp = _layer_prompt(xp, bp, tp, lw, tabs_p)
        xs, ss = _layer_sample(
            xs, bs, ts, lw, tabs_s, state_conv[l], state_lru_conv[l], state_lru_h[l],
            cache_cmp_t, cache_sel_t, win_t, l, page_table)
        st_p.append(sp)
        st_s.append(ss)
    outs = [xp.reshape(bp, tp, D_MODEL), xs.reshape(bs, ts, D_MODEL)]
    for k in range(6):
        outs.append(jnp.stack([s[k] for s in st_p]))
        outs.append(jnp.stack([s[k] for s in st_s]))
    return tuple(outs)
```

```python
import functools

import numpy as np
import jax
import jax.numpy as jnp
from jax import lax
from jax.experimental import pallas as pl
from jax.experimental.pallas import tpu as pltpu

F32 = jnp.float32
BF16 = jnp.bfloat16
I32 = jnp.int32

D_MODEL = 1024
DEPTH = 2
PAST_LEN = 16384
PAGE_SIZE = 128
D_CONV = 256
CONV_WIDTH = 31
D_LRU = 256
LRU_BLOCKS = 4
LRU_BW = D_LRU // LRU_BLOCKS
LRU_CONV_WIDTH = 4
LRU_C = 8.0
D_ATTN = 512
N_HEADS = 8
HEAD_DIM = 64
N_KV_HEADS = 2
GROUP = N_HEADS // N_KV_HEADS
KV_ROW = 2 * N_KV_HEADS * HEAD_DIM
ROPE_DIM = 16
ROPE_THETA = 500000.0
CMP_BLOCK = 32
CMP_STRIDE = 16
CMP_HIDDEN = 256
SEL_BLOCK = 64
N_SEL = 16
WINDOW = 512
Q_BLOCK = 128
D_FF = 2816
ALPHA = (2 * DEPTH) ** 0.25
LN_EPS = 1e-5
SCALE = HEAD_DIM ** -0.5

LANES = 128
SUBLANES = 8
VMEM_LIMIT = 56 * 1024 * 1024
FF_CHUNK = 256
N_FF_CHUNKS = D_FF // FF_CHUNK
PAD_HEAD = 128
Q_PAD = N_HEADS * PAD_HEAD
KV_GROUP_LANES = GROUP * PAD_HEAD
MASK_NEG = -(2.0 ** 60)
SEL_TILE = 512
SELWIN_QUERIES = 256

C_GLU = 0
C_LRUX = 512
C_LRUG = 768
C_Q = 1024
C_KVC = C_Q + Q_PAD
C_KVS = C_KVC + KV_ROW
C_KVW = C_KVS + KV_ROW
C_GATE = C_KVW + KV_ROW
N_PROJ = C_GATE + N_KV_HEADS * LANES


def _cparams(sem):
    return pltpu.CompilerParams(dimension_semantics=sem, vmem_limit_bytes=VMEM_LIMIT)


def _const_spec(shape):
    nd = len(shape)
    return pl.BlockSpec(shape, lambda *_: (0,) * nd, pipeline_mode=pl.Buffered(1))


def _layernorm(y, g, b):
    mu = jnp.mean(y, axis=-1, keepdims=True)
    d = y - mu
    var = jnp.mean(d * d, axis=-1, keepdims=True)
    return d * lax.rsqrt(var + LN_EPS) * g + b


def _dot(a, b):
    return jnp.dot(a, b, preferred_element_type=F32)


def _dot_nt(a, b):
    return lax.dot_general(a, b, (((1,), (1,)), ((), ())), preferred_element_type=F32)


def _dot_tn(a, b):
    return lax.dot_general(a, b, (((0,), (0,)), ((), ())), preferred_element_type=F32)


def _ffn_ln_apply(x, wg_ref, wu_ref, wo_ref, g_ref, b_ref):
    xb = x.astype(BF16)
    acc = jnp.zeros(x.shape, F32)
    for c in range(N_FF_CHUNKS):
        gate = _dot(xb, wg_ref[c])
        up = _dot(xb, wu_ref[c])
        h = (gate * jax.nn.sigmoid(gate)) * up
        acc = acc + _dot(h.astype(BF16), wo_ref[c])
    y = ALPHA * x + 0.5 * acc
    return _layernorm(y, g_ref[...], b_ref[...])


def _ffn_ln_kernel(x_ref, wg_ref, wu_ref, wo_ref, g_ref, b_ref, o_ref):
    o_ref[...] = _ffn_ln_apply(x_ref[...], wg_ref, wu_ref, wo_ref, g_ref, b_ref)


def _ffn_weight_specs():
    return [
        _const_spec((N_FF_CHUNKS, D_MODEL, FF_CHUNK)),
        _const_spec((N_FF_CHUNKS, D_MODEL, FF_CHUNK)),
        _const_spec((N_FF_CHUNKS, FF_CHUNK, D_MODEL)),
        _const_spec((1, D_MODEL)),
        _const_spec((1, D_MODEL)),
    ]


def _ffn_weight_args(fw):
    return (fw["wg"], fw["wu"], fw["wo"], fw["g"], fw["b"])


def _ffn_ln(x, fw, tm):
    n = x.shape[0]
    return pl.pallas_call(
        _ffn_ln_kernel,
        grid=(n // tm,),
        in_specs=[pl.BlockSpec((tm, D_MODEL), lambda i: (i, 0))] + _ffn_weight_specs(),
        out_specs=pl.BlockSpec((tm, D_MODEL), lambda i: (i, 0)),
        out_shape=jax.ShapeDtypeStruct((n, D_MODEL), F32),
        compiler_params=_cparams(("parallel",)),
        name="ffn_ln",
    )(x, *_ffn_weight_args(fw))


def _rope(v, cos, s1, s2):
    return v * cos + pltpu.roll(v, 8, 1) * s1 + pltpu.roll(v, LANES - 8, 1) * s2


def _inproj_kernel(x_ref, w_ref, cos_ref, s1_ref, s2_ref,
                   u_ref, lx_ref, lg_ref, qc_ref, qr_ref,
                   kvc_ref, kvs_ref, kvsb_ref, kvw_ref, kvwb_ref, gt_ref):
    xb = x_ref[...].astype(BF16)

    def mm(lo, hi):
        return _dot(xb, w_ref[:, lo:hi])

    glu = mm(C_GLU, C_GLU + 2 * D_CONV)
    u_ref[...] = glu[:, :D_CONV] * jax.nn.sigmoid(glu[:, D_CONV:])
    lx_ref[...] = mm(C_LRUX, C_LRUX + D_LRU)
    lg_ref[...] = mm(C_LRUG, C_LRUG + D_LRU)
    cos = cos_ref[...]
    s1 = s1_ref[...]
    s2 = s2_ref[...]
    for h in range(N_HEADS):
        qh = mm(C_Q + h * PAD_HEAD, C_Q + (h + 1) * PAD_HEAD) * SCALE
        qc_ref[:, h * PAD_HEAD:(h + 1) * PAD_HEAD] = qh.astype(BF16)
        qr_ref[:, h * PAD_HEAD:(h + 1) * PAD_HEAD] = _rope(qh, cos, s1, s2).astype(BF16)
    kvc_ref[...] = mm(C_KVC, C_KVC + KV_ROW)
    for c0, f_ref, b_ref in ((C_KVS, kvs_ref, kvsb_ref), (C_KVW, kvw_ref, kvwb_ref)):
        kv = mm(c0, c0 + KV_ROW)
        k = _rope(kv[:, :LANES], cos, s1, s2)
        v = kv[:, LANES:]
        f_ref[:, 0:LANES] = k
        f_ref[:, LANES:KV_ROW] = v
        b_ref[:, 0:LANES] = k.astype(BF16)
        b_ref[:, LANES:KV_ROW] = v.astype(BF16)
    gt_ref[...] = jax.nn.sigmoid(mm(C_GATE, N_PROJ))


def _rope_t(v, cos, s1, s2):
    return v * cos + pltpu.roll(v, 8, 0) * s1 + pltpu.roll(v, PAD_HEAD - 8, 0) * s2


R_Q = 0
R_KVS = Q_PAD
R_KVW = R_KVS + KV_ROW
R_KVC = R_KVW + KV_ROW
R_GATE = R_KVC + KV_ROW
N_PROJ_T = R_GATE + N_KV_HEADS * LANES


LOG2E = 1.4426950408889634
SUM_ROWS = (HEAD_DIM, 0)


def _inproj_t_kernel(x_ref, w_ref, wt_ref, cos_ref, s1_ref, s2_ref, cos_t_ref, s1_t_ref, s2_t_ref, blk_ref,
                     u_ref, lx_ref, lg_ref, kvc_ref, ks_ref, kw_ref,
                     kvct_ref, kvst_ref, kvwt_ref, qc_ref, qr_ref, vs_ref, vw_ref, gt_ref):
    xb = x_ref[...].astype(BF16)
    tm = xb.shape[0]

    def mm(lo, hi):
        return _dot(xb, w_ref[:, lo:hi])

    def mm_t(lo, hi):
        return _dot_nt(wt_ref[lo:hi, :], xb)

    glu = mm(C_GLU, C_GLU + 2 * D_CONV)
    u_ref[...] = glu[:, :D_CONV] * jax.nn.sigmoid(glu[:, D_CONV:])
    lx_ref[...] = mm(C_LRUX, C_LRUX + D_LRU)
    lg_ref[...] = mm(C_LRUG, C_LRUG + D_LRU)
    kvc_ref[...] = mm(C_KVC, C_KVC + KV_ROW)
    cos = cos_ref[...]
    s1 = s1_ref[...]
    s2 = s2_ref[...]
    for c0, b_ref in ((C_KVS, ks_ref), (C_KVW, kw_ref)):
        b_ref[:, 0:LANES] = _rope(mm(c0, c0 + LANES), cos, s1, s2).astype(BF16)
    ks_ref[:, LANES:] = blk_ref[...]
    cos_t = cos_t_ref[...]
    s1_t = s1_t_ref[...]
    s2_t = s2_t_ref[...]
    for h in range(N_HEADS):
        rows = slice(h * PAD_HEAD, (h + 1) * PAD_HEAD)
        qh = mm_t(R_Q + h * PAD_HEAD, R_Q + (h + 1) * PAD_HEAD)
        qc_ref[rows, :] = (qh * SCALE).astype(BF16)
        qr_ref[rows, :] = (_rope_t(qh, cos_t, s1_t, s2_t) * (SCALE * LOG2E)).astype(BF16)
    row = lax.broadcasted_iota(I32, (LANES, tm), 0)
    kvct_ref[...] = mm_t(R_KVC, R_KVC + KV_ROW)
    for r0, leaf_ref, o_ref in ((R_KVS, kvst_ref, vs_ref), (R_KVW, kvwt_ref, vw_ref)):
        leaf_ref[0:LANES, :] = _rope_t(mm_t(r0, r0 + LANES), cos_t, s1_t, s2_t)
        vt = mm_t(r0 + LANES, r0 + KV_ROW)
        leaf_ref[LANES:KV_ROW, :] = vt
        per_head = (jnp.where(row < HEAD_DIM, vt, jnp.where(row == SUM_ROWS[0], 1.0, 0.0)),
                    jnp.where(row >= HEAD_DIM, vt, jnp.where(row == SUM_ROWS[1], 1.0, 0.0)))
        for kv, vk in enumerate(per_head):
            vk = vk.astype(BF16)
            for c in range(tm // LANES):
                o_ref[kv, c] = vk[:, c * LANES:(c + 1) * LANES]
    gt_ref[...] = jax.nn.sigmoid(mm_t(R_GATE, N_PROJ_T))


def _inproj_t(x, w_all, w_t, rope_tabs, rope_tabs_t, bsz, t, tm):
    n = x.shape[0]
    tps = t // tm
    tok = lambda w: pl.BlockSpec((tm, w), lambda i: (i, 0))
    tab = pl.BlockSpec((tm, LANES), lambda i: (i % tps, 0))
    tab_t = pl.BlockSpec((PAD_HEAD, tm), lambda i: (0, i % tps))
    feat_t = lambda r: pl.BlockSpec((None, r, tm), lambda i: (i // tps, 0, i % tps))
    vt_spec = pl.BlockSpec((None, N_KV_HEADS, tm // LANES, LANES, LANES), lambda i: (i // tps, 0, i % tps, 0, 0))
    n_sel = t // SEL_BLOCK
    row_outs = [(D_CONV, F32), (D_LRU, F32), (D_LRU, F32), (KV_ROW, F32), (LANES + n_sel, BF16), (LANES, BF16)]
    leaf_t = jax.ShapeDtypeStruct((bsz, KV_ROW, t), F32)
    vt_shape = jax.ShapeDtypeStruct((bsz, N_KV_HEADS, t // LANES, LANES, LANES), BF16)
    block_id = (jnp.arange(t)[:, None] // SEL_BLOCK == jnp.arange(n_sel)[None, :]).astype(BF16)
    return pl.pallas_call(
        _inproj_t_kernel,
        grid=(n // tm,),
        in_specs=[tok(D_MODEL), _const_spec((D_MODEL, N_PROJ)), _const_spec((N_PROJ_T, D_MODEL)),
                  tab, tab, tab, tab_t, tab_t, tab_t,
                  pl.BlockSpec((tm, n_sel), lambda i: (i % tps, 0))],
        out_specs=[tok(w) for w, _ in row_outs]
        + [feat_t(KV_ROW)] * 3
        + [feat_t(Q_PAD), feat_t(Q_PAD), vt_spec, vt_spec, feat_t(N_KV_HEADS * LANES)],
        out_shape=[jax.ShapeDtypeStruct((n, w), dt) for w, dt in row_outs]
        + [leaf_t] * 3
        + [jax.ShapeDtypeStruct((bsz, Q_PAD, t), BF16), jax.ShapeDtypeStruct((bsz, Q_PAD, t), BF16),
           vt_shape, vt_shape, jax.ShapeDtypeStruct((bsz, N_KV_HEADS * LANES, t), F32)],
        compiler_params=_cparams(("parallel",)),
        name="inproj_t",
    )(x, w_all, w_t, *rope_tabs, *rope_tabs_t, block_id)


def _inproj(x, w_all, rope_tabs, tm, tiles_per_seq):
    n = x.shape[0]
    cos, s1, s2 = rope_tabs
    tok = lambda w: pl.BlockSpec((tm, w), lambda i: (i, 0))
    tab = pl.BlockSpec((tm, LANES), lambda i: (i % tiles_per_seq, 0))
    outs = [
        (D_CONV, F32), (D_LRU, F32), (D_LRU, F32), (Q_PAD, BF16), (Q_PAD, BF16),
        (KV_ROW, F32), (KV_ROW, F32), (KV_ROW, BF16), (KV_ROW, F32), (KV_ROW, BF16),
        (N_KV_HEADS * LANES, F32),
    ]
    return pl.pallas_call(
        _inproj_kernel,
        grid=(n // tm,),
        in_specs=[tok(D_MODEL), _const_spec((D_MODEL, N_PROJ)), tab, tab, tab],
        out_specs=[tok(w) for w, _ in outs],
        out_shape=[jax.ShapeDtypeStruct((n, w), dt) for w, dt in outs],
        compiler_params=_cparams(("parallel",)),
        name="inproj",
    )(x, w_all, cos, s1, s2)


CONV_HALO = 32
CONV_ROWS = 64


def _conv_kernel(u_ref, buf_ref, w_ref, b_ref, g_ref, bb_ref, o_ref, ext_ref, sh_ref):
    t = pl.program_id(1)
    tc = u_ref.shape[0]

    @pl.when(t == 0)
    def _():
        ext_ref[0:CONV_HALO, :] = buf_ref[...]

    ext_ref[CONV_HALO:CONV_HALO + tc, :] = u_ref[...]
    off = CONV_HALO - (CONV_WIDTH - 1)
    span = tc + CONV_HALO - SUBLANES
    for s in range(1, SUBLANES):
        sh_ref[s - 1, 0:span, :] = ext_ref[s:s + span, :]
    rows = min(CONV_ROWS, tc)
    for r0 in range(0, tc, rows):
        acc = jnp.zeros((rows, D_CONV), F32)
        for k in range(CONV_WIDTH):
            a, s = divmod(off + k, SUBLANES)
            src = ext_ref if s == 0 else sh_ref.at[s - 1]
            acc = acc + src[r0 + SUBLANES * a:r0 + SUBLANES * a + rows, :] * w_ref[k:k + 1, :]
        y = acc + b_ref[...]
        y = _layernorm(y, g_ref[...], bb_ref[...])
        o_ref[r0:r0 + rows, :] = (y * jax.nn.sigmoid(y)).astype(BF16)
    ext_ref[0:CONV_HALO, :] = ext_ref[tc:tc + CONV_HALO, :]


def _conv_group(u, buf, cw, tc):
    b, t, _ = u.shape
    return pl.pallas_call(
        _conv_kernel,
        grid=(b, t // tc),
        in_specs=[
            pl.BlockSpec((None, tc, D_CONV), lambda i, j: (i, j, 0)),
            pl.BlockSpec((None, CONV_HALO, D_CONV), lambda i, j: (i, 0, 0)),
            _const_spec((CONV_HALO, D_CONV)),
            _const_spec((1, D_CONV)), _const_spec((1, D_CONV)), _const_spec((1, D_CONV)),
        ],
        out_specs=pl.BlockSpec((None, tc, D_CONV), lambda i, j: (i, j, 0)),
        out_shape=jax.ShapeDtypeStruct((b, t, D_CONV), BF16),
        scratch_shapes=[pltpu.VMEM((tc + CONV_HALO, D_CONV), F32),
                        pltpu.VMEM((SUBLANES - 1, tc + CONV_HALO - SUBLANES, D_CONV), F32)],
        compiler_params=_cparams(("parallel", "arbitrary")),
        name="conv_group",
    )(u, buf, cw["w"], cw["b"], cw["ln_g"], cw["ln_b"])


LRU_HALO = 8


def _lru_kernel(last_row, x_ref, gate_ref, buf_ref, h0_ref, cw_ref, cb_ref, wr_ref, wi_ref,
                bg_ref, lam_ref, y_ref, hl_ref, ext_ref, hc_ref):
    t = pl.program_id(1)
    tl = x_ref.shape[0]

    @pl.when(t == 0)
    def _():
        ext_ref[0:LRU_HALO, :] = buf_ref[...]
        hc_ref[...] = h0_ref[...]

    ext_ref[LRU_HALO:LRU_HALO + tl, :] = x_ref[...]
    off = LRU_HALO - (LRU_CONV_WIDTH - 1)
    xl = jnp.zeros((tl, D_LRU), F32)
    for k in range(LRU_CONV_WIDTH):
        xl = xl + ext_ref[off + k:off + k + tl, :] * cw_ref[k:k + 1, :]
    xl = xl + cb_ref[...]
    ext_ref[0:LRU_HALO, :] = ext_ref[tl:tl + LRU_HALO, :]

    xb = xl.astype(BF16)
    r_gate = jax.nn.sigmoid(_dot(xb, wr_ref[...]) + bg_ref[0:1, :])
    i_gate = jax.nn.sigmoid(_dot(xb, wi_ref[...]) + bg_ref[1:2, :])
    log_a = LRU_C * r_gate * jax.nn.log_sigmoid(lam_ref[...])
    a = jnp.exp(log_a)
    bv = jnp.sqrt(-jnp.tanh(log_a) * (a * a + 1.0)) * (i_gate * xl)

    row = lax.broadcasted_iota(I32, (tl, D_LRU), 0)
    s = 1
    while s < tl:
        keep = row >= s
        a_sh = jnp.where(keep, pltpu.roll(a, s, 0), 1.0)
        b_sh = jnp.where(keep, pltpu.roll(bv, s, 0), 0.0)
        bv = a * b_sh + bv
        a = a * a_sh
        s *= 2
    h = a * hc_ref[0:1, :] + bv
    hc_ref[...] = jnp.broadcast_to(h[last_row:last_row + 1, :], hc_ref.shape)
    hl_ref[...] = hc_ref[...]
    y_ref[...] = (h * jax.nn.gelu(gate_ref[...])).astype(BF16)


def _lru_group(x, gate, buf, h0, lw, tl, last_row):
    b, t, _ = x.shape
    assert last_row == tl - 1 or t == tl
    seq = pl.BlockSpec((None, tl, D_LRU), lambda i, j: (i, j, 0))
    per_b = pl.BlockSpec((None, LRU_HALO, D_LRU), lambda i, j: (i, 0, 0))
    return pl.pallas_call(
        functools.partial(_lru_kernel, last_row),
        grid=(b, t // tl),
        in_specs=[seq, seq, per_b, per_b,
                  _const_spec((LRU_HALO, D_LRU)), _const_spec((1, D_LRU)),
                  _const_spec((D_LRU, D_LRU)), _const_spec((D_LRU, D_LRU)),
                  _const_spec((2, D_LRU)), _const_spec((1, D_LRU))],
        out_specs=[seq, per_b],
        out_shape=[jax.ShapeDtypeStruct((b, t, D_LRU), BF16),
                   jax.ShapeDtypeStruct((b, LRU_HALO, D_LRU), F32)],
        scratch_shapes=[pltpu.VMEM((tl + LRU_HALO, D_LRU), F32),
                        pltpu.VMEM((LRU_HALO, D_LRU), F32)],
        compiler_params=_cparams(("parallel", "arbitrary")),
        name="lru_group",
    )(x, gate, buf, h0, lw["cw"], lw["cb"], lw["wr"], lw["wi"], lw["bg"], lw["lam"])


def _compress_rows(row_refs, pe_ref, w1_ref, b1_ref, w2_ref, b2_ref, carry_ref):
    lo = lax.broadcasted_iota(I32, (1, LANES), 1) < HEAD_DIM
    xs = [[] for _ in range(4)]
    for halves in row_refs:
        n = halves[0].shape[0] // CMP_STRIDE
        cols = [[] for _ in range(4)]
        for rp in range(CMP_STRIDE // 2):
            for half, ref in enumerate(halves):
                pa = ref[pl.ds(2 * rp, n, stride=CMP_STRIDE), :]
                pb = ref[pl.ds(2 * rp + 1, n, stride=CMP_STRIDE), :]
                cols[2 * half].append(jnp.where(lo, pa, pltpu.roll(pb, HEAD_DIM, 1)))
                cols[2 * half + 1].append(jnp.where(lo, pltpu.roll(pa, HEAD_DIM, 1), pb))
        for g in range(4):
            xs[g].append(jnp.concatenate(cols[g], axis=1))
    out = None
    for g in range(4):
        sidx = g // 2
        x = jnp.concatenate(xs[g], axis=0)
        p0 = _dot((x + pe_ref[0, g:g + 1, :]).astype(BF16), w1_ref[sidx, 0])
        p1 = _dot((x + pe_ref[1, g:g + 1, :]).astype(BF16), w1_ref[sidx, 1])
        n_tot = p0.shape[0]
        row = lax.broadcasted_iota(I32, p0.shape, 0)
        p0s = jnp.where(row == 0, carry_ref[g, 0:1, :], pltpu.roll(p0, 1, 0))
        carry_ref[g, 0:1, :] = p0[n_tot - 1:n_tot, :]
        h = (b1_ref[sidx:sidx + 1, :] + p0s) + p1
        part = _dot(jax.nn.gelu(h).astype(BF16), w2_ref[g])
        out = part if out is None else out + part
    return out + b2_ref[...]


def _cmp_weight_specs():
    return [
        _const_spec((2, 4, CMP_STRIDE * HEAD_DIM)),
        _const_spec((2, 2, CMP_STRIDE * HEAD_DIM, CMP_HIDDEN)),
        _const_spec((2, CMP_HIDDEN)),
        _const_spec((4, CMP_HIDDEN, KV_ROW)),
        _const_spec((1, KV_ROW)),
    ]


def _cmp_weight_args(cw):
    return (cw["pe"], cw["w1"], cw["b1"], cw["w2"], cw["b2"])


CMP_TILE_ROWS = 2048


def _compress_prompt_kernel(k_ref, v_ref, pe_ref, w1_ref, b1_ref, w2_ref, b2_ref, kc_ref, vct_ref, carry_ref):
    @pl.when(pl.program_id(1) == 0)
    def _():
        carry_ref[...] = jnp.zeros(carry_ref.shape, F32)

    out = _compress_rows([(k_ref, v_ref)], pe_ref, w1_ref, b1_ref, w2_ref, b2_ref, carry_ref)
    kc_ref[...] = out[:, 0:LANES].astype(BF16)
    vct_ref[...] = out[:, LANES:KV_ROW].T.astype(BF16)


def _compress_prompt(kvc, cw):
    b, t, _ = kvc.shape
    n_e = CMP_TILE_ROWS // CMP_STRIDE
    return pl.pallas_call(
        _compress_prompt_kernel,
        grid=(b, t // CMP_TILE_ROWS),
        in_specs=[pl.BlockSpec((None, CMP_TILE_ROWS, LANES), lambda i, j: (i, j, 0)),
                  pl.BlockSpec((None, CMP_TILE_ROWS, LANES), lambda i, j: (i, j, 1))]
        + _cmp_weight_specs(),
        out_specs=[pl.BlockSpec((None, n_e, LANES), lambda i, j: (i, j, 0)),
                   pl.BlockSpec((None, LANES, n_e), lambda i, j: (i, 0, j))],
        out_shape=[jax.ShapeDtypeStruct((b, t // CMP_STRIDE, LANES), BF16),
                   jax.ShapeDtypeStruct((b, LANES, t // CMP_STRIDE), BF16)],
        scratch_shapes=[pltpu.VMEM((4, 8, CMP_HIDDEN), F32)],
        compiler_params=_cparams(("parallel", "arbitrary")),
        name="compress_prompt",
    )(kvc, kvc, *_cmp_weight_args(cw))


def _selection_scores(pk_ref, n_sel):
    ratio = SEL_BLOCK // CMP_STRIDE
    slc = pk_ref[pl.ds(0, n_sel, stride=ratio), :]
    for o in range(1, ratio):
        slc = slc + 2.0 * pk_ref[pl.ds(o, n_sel, stride=ratio), :]
    return slc + pk_ref[pl.ds(ratio, n_sel, stride=ratio), :]


def _topk_rounds(score, n_rows):
    j = lax.broadcasted_iota(I32, score.shape, 0).astype(F32)
    sel = jnp.zeros(score.shape, F32)
    picks = []
    for _ in range(N_SEL):
        cm = jnp.max(score, axis=0, keepdims=True)
        mi = jnp.min(jnp.where(score == cm, j, float(n_rows)), axis=0, keepdims=True)
        hit = j == mi
        sel = jnp.where(hit, 1.0, sel)
        score = jnp.where(hit, -jnp.inf, score)
        picks.append(mi)
    return sel, picks


def _masked_softmax_rows(s_t, valid):
    s_t = jnp.where(valid, s_t, -jnp.inf)
    m = jnp.max(s_t, axis=0, keepdims=True)
    m = jnp.where(m > -jnp.inf, m, 0.0)
    e = jnp.exp(s_t - m)
    d = jnp.sum(e, axis=0, keepdims=True)
    return e / jnp.where(d > 0, d, 1.0)


CMP_ENTRY_CHUNK = 128


def _cmp_topk_prompt_kernel(qc_ref, kc_ref, vct_ref, ocmp_ref, mneg_ref, pk0_ref, pk1_ref):
    i = pl.program_id(1)
    n_e = kc_ref.shape[0]
    n_sel = n_e // (SEL_BLOCK // CMP_STRIDE)
    pk_refs = (pk0_ref, pk1_ref)

    def attend(n_use):
        kc = kc_ref[0:n_use, :]
        vct = vct_ref[:, 0:n_use]
        e_idx = lax.broadcasted_iota(I32, (n_use, Q_BLOCK), 0)
        qpos = i * Q_BLOCK + lax.broadcasted_iota(I32, (n_use, Q_BLOCK), 1)
        valid = (e_idx >= 1) & (CMP_STRIDE * e_idx + (CMP_STRIDE - 1) <= qpos)
        for kv in range(N_KV_HEADS):
            pkv = jnp.zeros((n_use, Q_BLOCK), F32)
            for h in range(kv * GROUP, (kv + 1) * GROUP):
                rows = slice(h * PAD_HEAD, (h + 1) * PAD_HEAD)
                p = _masked_softmax_rows(_dot(kc, qc_ref[rows, :]), valid)
                pkv = pkv + p
                ocmp_ref[rows, :] = _dot(vct, p.astype(BF16))
            pk_refs[kv][0:n_use, :] = pkv
            pk_refs[kv][n_use:, :] = jnp.zeros((n_e + 8 - n_use, Q_BLOCK), F32)

    n_var = n_e // CMP_ENTRY_CHUNK
    variant = jnp.minimum((8 * i + 7) // CMP_ENTRY_CHUNK, n_var - 1)
    for k in range(n_var):
        pl.when(variant == k)(functools.partial(attend, CMP_ENTRY_CHUNK * (k + 1)))

    j = lax.broadcasted_iota(I32, (n_sel, Q_BLOCK), 0)
    qp = i * Q_BLOCK + lax.broadcasted_iota(I32, (n_sel, Q_BLOCK), 1)
    qblk = jnp.right_shift(qp, 6)
    forced = (j == 0) | (j == qblk) | (j == qblk - 1)
    scores = [jnp.where(forced, jnp.inf, jnp.where(j * SEL_BLOCK <= qp, _selection_scores(r, n_sel), -jnp.inf))
              for r in pk_refs]
    sel, _ = _topk_rounds(jnp.concatenate(scores, axis=1), n_sel)
    mneg = jnp.where(sel > 0, 0.0, MASK_NEG).astype(BF16)
    for kv in range(N_KV_HEADS):
        mneg_ref[kv] = mneg[:, kv * Q_BLOCK:(kv + 1) * Q_BLOCK]


def _cmp_topk_prompt(qc_t, kc, vc_t):
    b, _, t = qc_t.shape
    n_e = kc.shape[1]
    n_sel = t // SEL_BLOCK
    assert n_e % CMP_ENTRY_CHUNK == 0
    qblk = pl.BlockSpec((None, Q_PAD, Q_BLOCK), lambda bi, i: (bi, 0, i))
    return pl.pallas_call(
        _cmp_topk_prompt_kernel,
        grid=(b, t // Q_BLOCK),
        in_specs=[
            qblk,
            pl.BlockSpec((None, n_e, LANES), lambda bi, i: (bi, 0, 0)),
            pl.BlockSpec((None, LANES, n_e), lambda bi, i: (bi, 0, 0)),
        ],
        out_specs=[
            qblk,
            pl.BlockSpec((None, N_KV_HEADS, n_sel, Q_BLOCK), lambda bi, i: (bi, 0, 0, i)),
        ],
        out_shape=[jax.ShapeDtypeStruct((b, Q_PAD, t), F32),
                   jax.ShapeDtypeStruct((b, N_KV_HEADS, n_sel, t), BF16)],
        scratch_shapes=[pltpu.VMEM((n_e + 8, Q_BLOCK), F32), pltpu.VMEM((n_e + 8, Q_BLOCK), F32)],
        compiler_params=_cparams(("parallel", "arbitrary")),
        name="cmp_topk_prompt",
    )(qc_t, kc, vc_t)


def _selwin_prompt_kernel(qr_ref, mneg_ref, ks_ref, vs_ref, kw_ref, vw_ref, ocmp_ref, gt_ref, o_ref,
                          qa_ref, sa_ref, sb_ref, pa_ref, pb_ref):
    i = pl.program_id(1)
    n_sel = mneg_ref.shape[1]
    qb = qr_ref.shape[1]
    win_keys = WINDOW + qb
    cols = N_HEADS * qb
    kv_cols = GROUP * qb
    for h in range(N_HEADS):
        c = slice(h * qb, (h + 1) * qb)
        qa_ref[0:PAD_HEAD, c] = qr_ref[h * PAD_HEAD:(h + 1) * PAD_HEAD, :]
        qa_ref[PAD_HEAD:PAD_HEAD + n_sel, c] = mneg_ref[h // GROUP]
    qa = qa_ref[...]
    q0 = i * qb
    qpos = q0 + (lax.broadcasted_iota(I32, (1, cols), 1) & (qb - 1))

    tile = SEL_TILE
    key_col = lax.broadcasted_iota(I32, (tile, 1), 0)

    def scores(t):
        return _dot(ks_ref[pl.ds(pl.multiple_of(t * tile, tile), tile), :], qa)

    def weighted_values(v_ref, first_tile, n_tiles, p):
        outs = []
        for kv in range(N_KV_HEADS):
            vk = v_ref[kv, pl.ds(first_tile, n_tiles)]
            vt = jnp.concatenate([vk[c] for c in range(n_tiles)], axis=1)
            outs.append(_dot(vt, p[:, kv * kv_cols:(kv + 1) * kv_cols]))
        return jnp.concatenate(outs, axis=1)

    def normalise(acc):
        sums = [acc[SUM_ROWS[kv]:SUM_ROWS[kv] + 1, kv * kv_cols:(kv + 1) * kv_cols] for kv in range(N_KV_HEADS)]
        return acc / jnp.concatenate(sums, axis=1)

    start = pl.multiple_of(jnp.maximum(q0 - WINDOW, 0), LANES)
    s = _dot(kw_ref[pl.ds(start, win_keys), :], qa[0:PAD_HEAD, :])
    dpos = qpos - (start + lax.broadcasted_iota(I32, (win_keys, 1), 0))
    s = jnp.where(lax.bitcast_convert_type(dpos, jnp.uint32) <= jnp.uint32(WINDOW), s, -jnp.inf)
    p = jnp.exp2(s - jnp.max(s, axis=0, keepdims=True))
    o_win = normalise(weighted_values(vw_ref, start // LANES, win_keys // LANES, p.astype(BF16)))

    sub = tile // LANES

    def stage(t, cur, nxt, carry, masked, prefetch):
        m, acc, alpha_prev = carry
        if prefetch:
            s_refs[nxt][...] = scores(t + 1)
        acc = alpha_prev * acc + weighted_values(vs_ref, jnp.maximum(t - 1, 0) * sub, sub, p_refs[nxt][...])
        s = s_refs[cur][...]
        if masked:
            s = jnp.where(t * tile + key_col <= qpos, s, MASK_NEG)
        m_new = jnp.maximum(m, jnp.max(s, axis=0, keepdims=True))
        alpha = jnp.exp2(m - m_new)
        p_refs[cur][...] = jnp.exp2(s - m_new).astype(BF16)
        return m_new, acc, alpha

    s_refs = (sa_ref, sb_ref)
    p_refs = (pa_ref, pb_ref)
    sa_ref[...] = scores(0)
    pb_ref[...] = jnp.zeros(pb_ref.shape, BF16)
    init = (jnp.full((1, cols), MASK_NEG, F32), jnp.zeros((LANES, cols), F32), jnp.ones((1, cols), F32))

    def pair(u, carry):
        carry = stage(2 * u, 0, 1, carry, False, True)
        return stage(2 * u + 1, 1, 0, carry, False, True)

    u_diag = q0 // (2 * tile)
    carry = lax.fori_loop(0, u_diag, pair, init)
    carry = stage(2 * u_diag, 0, 1, carry, True, True)

    def finish_second(c):
        _, acc, alpha = stage(2 * u_diag + 1, 1, 0, c, True, False)
        return alpha * acc + weighted_values(vs_ref, (2 * u_diag + 1) * sub, sub, pb_ref[...])

    def finish_first(c):
        _, acc, alpha = c
        return alpha * acc + weighted_values(vs_ref, 2 * u_diag * sub, sub, pa_ref[...])

    o_sel = normalise(lax.cond(q0 - 2 * u_diag * tile >= tile, finish_second, finish_first, carry))

    for h in range(N_HEADS):
        c = slice(h * qb, (h + 1) * qb)
        rows = slice(h * PAD_HEAD, (h + 1) * PAD_HEAD)
        g0 = (h // GROUP) * LANES + 3 * (h % GROUP)
        o = (gt_ref[g0:g0 + 1, :] * ocmp_ref[rows, :] + gt_ref[g0 + 1:g0 + 2, :] * o_sel[:, c]
             + gt_ref[g0 + 2:g0 + 3, :] * o_win[:, c])
        o_ref[rows, :] = o.astype(BF16)


def _selwin_prompt(qr_t, mneg, ks, vs_t, kw, vw_t, ocmp_t, gates_t):
    b, _, t = qr_t.shape
    n_sel = mneg.shape[2]
    qb = SELWIN_QUERIES
    assert t % (2 * SEL_TILE) == 0 and SEL_TILE % qb == 0 and t >= WINDOW + qb
    cols = N_HEADS * qb
    per_q = lambda r: pl.BlockSpec((None, r, qb), lambda bi, i: (bi, 0, i))
    per_b = lambda *shape: pl.BlockSpec((None,) + shape, lambda bi, i: (bi,) + (0,) * len(shape),
                                        pipeline_mode=pl.Buffered(1))
    vals = per_b(N_KV_HEADS, t // LANES, LANES, LANES)
    return pl.pallas_call(
        _selwin_prompt_kernel,
        grid=(b, t // qb),
        in_specs=[
            per_q(Q_PAD),
            pl.BlockSpec((None, N_KV_HEADS, n_sel, qb), lambda bi, i: (bi, 0, 0, i)),
            per_b(t, LANES + n_sel), vals, per_b(t, LANES), vals,
            per_q(Q_PAD), per_q(N_KV_HEADS * LANES),
        ],
        out_specs=per_q(Q_PAD),
        out_shape=jax.ShapeDtypeStruct((b, Q_PAD, t), BF16),
        scratch_shapes=[pltpu.VMEM((PAD_HEAD + n_sel, cols), BF16),
                        pltpu.VMEM((SEL_TILE, cols), F32),
                        pltpu.VMEM((SEL_TILE, cols), F32),
                        pltpu.VMEM((SEL_TILE, cols), BF16),
                        pltpu.VMEM((SEL_TILE, cols), BF16)],
        compiler_params=_cparams(("parallel", "arbitrary")),
        name="selwin_prompt",
    )(qr_t, mneg, ks, vs_t, kw, vw_t, ocmp_t, gates_t)


PAGES_PER_STEP = 16
S_COLS = LANES


def _cmp_topk_sample_kernel(n_t, pt_ref, *refs):
    page_refs = refs[:PAGES_PER_STEP]
    (q_ref, pe_ref, w1_ref, b1_ref, w2_ref, b2_ref,
     ocmp_ref, idx_ref, kcv_ref, pk_ref, carry_ref, xk_ref, xv_ref) = refs[PAGES_PER_STEP:]
    s = pl.program_id(1)
    n_steps = pl.num_programs(1)
    n_e = kcv_ref.shape[0]
    step_e = PAGES_PER_STEP * PAGE_SIZE // CMP_STRIDE

    @pl.when(s == 0)
    def _():
        carry_ref[...] = jnp.zeros(carry_ref.shape, F32)

    for k, page in enumerate(page_refs):
        rows = slice(k * PAGE_SIZE, (k + 1) * PAGE_SIZE)
        xk_ref[rows, :] = page[0:LANES, :].T
        xv_ref[rows, :] = page[LANES:KV_ROW, :].T
    out = _compress_rows([(xk_ref, xv_ref)], pe_ref, w1_ref, b1_ref, w2_ref, b2_ref, carry_ref)
    kcv_ref[pl.ds(pl.multiple_of(s * step_e, step_e), step_e), :] = out.astype(BF16)

    @pl.when(s == n_steps - 1)
    def _():
        n_sel = n_e // (SEL_BLOCK // CMP_STRIDE) + 1
        n_sel_rows = pk_ref.shape[0] // (SEL_BLOCK // CMP_STRIDE) - 2
        kc = kcv_ref[:, 0:LANES]
        vc = kcv_ref[:, LANES:KV_ROW]
        e_idx = lax.broadcasted_iota(I32, (n_e, S_COLS), 0)
        p = _masked_softmax_rows(_dot_nt(kc, q_ref[...]), e_idx >= 1)
        ocmp_ref[...] = _dot_tn(p.astype(BF16), vc)
        pkv = p
        for hh in range(1, GROUP):
            pkv = pkv + pltpu.roll(p, S_COLS - n_t * hh, 1)
        pk_ref[0:n_e, :] = pkv
        pk_ref[n_e:, :] = jnp.zeros((pk_ref.shape[0] - n_e, S_COLS), F32)
        slc = _selection_scores(pk_ref, n_sel_rows)
        j = lax.broadcasted_iota(I32, (n_sel_rows, S_COLS), 0)
        qp = PAST_LEN + (lax.broadcasted_iota(I32, (n_sel_rows, S_COLS), 1) & (n_t - 1))
        qblk = jnp.right_shift(qp, 6)
        forced = (j == 0) | (j == qblk) | (j == qblk - 1)
        in_range = j < n_sel
        score = jnp.where(forced & in_range, jnp.inf,
                          jnp.where((j * SEL_BLOCK <= qp) & in_range, slc, -jnp.inf))
        _, picks = _topk_rounds(score, n_sel_rows)
        for r, mi in enumerate(picks):
            idx_ref[r:r + 1, :] = mi.astype(I32)


def _cmp_topk_sample(page_table, cache_cmp_t, layer, q_cols, cw, n_t):
    assert n_t & (n_t - 1) == 0 and N_HEADS * n_t <= S_COLS
    b, n_pages = page_table.shape
    n_e = n_pages * PAGE_SIZE // CMP_STRIDE
    n_sel_rows = ((n_e // 4 + 1) + 7) // 8 * 8
    pk_rows = 4 * (n_sel_rows + 2)
    n_steps = n_pages // PAGES_PER_STEP
    step_rows = PAGES_PER_STEP * PAGE_SIZE

    def page_spec(k):
        return pl.BlockSpec((None, None, KV_ROW, PAGE_SIZE),
                            lambda bi, s, pt: (layer, pt[bi, s * PAGES_PER_STEP + k], 0, 0))

    per_b = lambda rows, w: pl.BlockSpec((None, rows, w), lambda bi, s, pt: (bi, 0, 0))
    grid_spec = pltpu.PrefetchScalarGridSpec(
        num_scalar_prefetch=1,
        grid=(b, n_steps),
        in_specs=[page_spec(k) for k in range(PAGES_PER_STEP)]
        + [per_b(S_COLS, LANES)] + _cmp_weight_specs(),
        out_specs=[per_b(S_COLS, LANES), per_b(N_SEL, S_COLS)],
        scratch_shapes=[pltpu.VMEM((n_e, KV_ROW), BF16),
                        pltpu.VMEM((pk_rows, S_COLS), F32),
                        pltpu.VMEM((4, 8, CMP_HIDDEN), F32),
                        pltpu.VMEM((step_rows, LANES), F32),
                        pltpu.VMEM((step_rows, LANES), F32)],
    )
    return pl.pallas_call(
        functools.partial(_cmp_topk_sample_kernel, n_t),
        grid_spec=grid_spec,
        out_shape=[jax.ShapeDtypeStruct((b, S_COLS, LANES), F32),
                   jax.ShapeDtypeStruct((b, N_SEL, S_COLS), I32)],
        compiler_params=_cparams(("parallel", "arbitrary")),
        name="cmp_topk_sample",
    )(page_table, *([cache_cmp_t] * PAGES_PER_STEP), q_cols, *_cmp_weight_args(cw))


HROWS = 8


def _selwin_sample_kernel(idx_ref, pt_ref, *refs):
    page_refs = refs[:N_SEL]
    (q_ref, new_s_ref, win_ref, new_w_ref, ocmp_ref, gt_ref, o_ref, k_ref, v_ref) = refs[N_SEL:]
    bi = pl.program_id(0)
    tq = pl.program_id(1)
    kh = pl.program_id(2)
    n_t = pl.num_programs(1)
    qpos = PAST_LEN + tq
    q = q_ref[...]
    base = ((bi * n_t + tq) * N_KV_HEADS + kh) * N_SEL
    new_block = PAST_LEN // SEL_BLOCK

    valid_parts = []
    lane = lax.broadcasted_iota(I32, (1, PAGE_SIZE), 1)
    for r in range(N_SEL):
        j = idx_ref[base + r]
        is_new = j == new_block
        cols = slice(r * PAGE_SIZE, (r + 1) * PAGE_SIZE)

        @pl.when(is_new)
        def _():
            k_ref[:, cols] = new_s_ref[0:LANES, :].astype(BF16)
            v_ref[:, cols] = new_s_ref[LANES:KV_ROW, :].astype(BF16)

        @pl.when(jnp.logical_not(is_new))
        def _():
            k_ref[:, cols] = page_refs[r][0:LANES, :].astype(BF16)
            v_ref[:, cols] = page_refs[r][LANES:KV_ROW, :].astype(BF16)

        kpos = jnp.right_shift(j, 1) * PAGE_SIZE + lane
        valid_parts.append((jnp.right_shift(kpos, 6) == j) & (kpos <= qpos))
    valid = jnp.concatenate(valid_parts, axis=1)
    s = jnp.where(valid, _dot(q, k_ref[...]), -jnp.inf)
    m = jnp.max(s, axis=-1, keepdims=True)
    p = jnp.exp(s - m)
    o_sel = _dot_nt(p.astype(BF16), v_ref[...]) / jnp.sum(p, axis=-1, keepdims=True)

    wb = win_ref.shape[1]
    s_old = _dot(q, win_ref[0:LANES, :].astype(BF16))
    d_old = qpos - (PAST_LEN - wb + lax.broadcasted_iota(I32, (1, wb), 1))
    s_old = jnp.where((d_old >= 0) & (d_old <= WINDOW), s_old, -jnp.inf)
    s_new = _dot(q, new_w_ref[0:LANES, :].astype(BF16))
    d_new = tq - lax.broadcasted_iota(I32, (1, new_w_ref.shape[1]), 1)
    s_new = jnp.where((d_new >= 0) & (d_new <= WINDOW), s_new, -jnp.inf)
    m = jnp.maximum(jnp.max(s_old, axis=-1, keepdims=True), jnp.max(s_new, axis=-1, keepdims=True))
    p_old = jnp.exp(s_old - m)
    p_new = jnp.exp(s_new - m)
    den = jnp.sum(p_old, axis=-1, keepdims=True) + jnp.sum(p_new, axis=-1, keepdims=True)
    o_win = (_dot_nt(p_old.astype(BF16), win_ref[LANES:KV_ROW, :].astype(BF16))
             + _dot_nt(p_new.astype(BF16), new_w_ref[LANES:KV_ROW, :].astype(BF16))) / den

    g = gt_ref[...]
    o = g[:, 0:1] * ocmp_ref[...] + g[:, 1:2] * o_sel + g[:, 2:3] * o_win
    o_ref[...] = o.astype(BF16)


def _selwin_sample(idx_flat, page_table, cache_sel_t, win_t, layer, q_rows, new_s_t, new_w_t, ocmp_rows, gate_rows):
    b, n_t = q_rows.shape[:2]
    wb = win_t.shape[-1]
    n_t_static = n_t
    last_page = page_table.shape[1] - 1

    def page_spec(r):
        def imap(bi, tq, kh, idx, pt):
            j = idx[((bi * n_t_static + tq) * N_KV_HEADS + kh) * N_SEL + r]
            return (layer, pt[bi, jnp.minimum(jnp.right_shift(j, 1), last_page)], 0, 0)
        return pl.BlockSpec((None, None, KV_ROW, PAGE_SIZE), imap)

    row5 = pl.BlockSpec((None, None, None, HROWS, LANES), lambda bi, tq, kh, idx, pt: (bi, tq, kh, 0, 0))
    new_rows = pl.BlockSpec((None, KV_ROW, LANES), lambda bi, tq, kh, idx, pt: (bi, 0, 0))
    grid_spec = pltpu.PrefetchScalarGridSpec(
        num_scalar_prefetch=2,
        grid=(b, n_t, N_KV_HEADS),
        in_specs=[page_spec(r) for r in range(N_SEL)]
        + [row5, new_rows,
           pl.BlockSpec((None, None, KV_ROW, wb), lambda bi, tq, kh, idx, pt: (layer, bi, 0, 0)),
           new_rows, row5, row5],
        out_specs=row5,
        scratch_shapes=[pltpu.VMEM((LANES, N_SEL * PAGE_SIZE), BF16),
                        pltpu.VMEM((LANES, N_SEL * PAGE_SIZE), BF16)],
    )
    return pl.pallas_call(
        _selwin_sample_kernel,
        grid_spec=grid_spec,
        out_shape=jax.ShapeDtypeStruct((b, n_t, N_KV_HEADS, HROWS, LANES), BF16),
        compiler_params=_cparams(("arbitrary", "arbitrary", "arbitrary")),
        name="selwin_sample",
    )(idx_flat, page_table, *([cache_sel_t] * N_SEL), q_rows, new_s_t, win_t, new_w_t, ocmp_rows, gate_rows)


def _outproj_ln_kernel(attn_transposed, x_ref, yc_ref, yl_ref, ya_ref, w_ref, g_ref, b_ref,
                       wg_ref, wu_ref, wo_ref, g2_ref, b2_ref, o_ref):
    y = _dot(yc_ref[...], w_ref[0:D_CONV, :])
    y = y + _dot(yl_ref[...], w_ref[D_CONV:D_CONV + D_LRU, :])
    w_attn = w_ref[D_CONV + D_LRU:, :]
    y = y + (_dot_tn(ya_ref[...], w_attn) if attn_transposed else _dot(ya_ref[...], w_attn))
    x = _layernorm(ALPHA * x_ref[...] + y, g_ref[...], b_ref[...])
    o_ref[...] = _ffn_ln_apply(x, wg_ref, wu_ref, wo_ref, g2_ref, b2_ref)


def _outproj_ffn(x, yc, yl, ya, ow, fw, tm):
    n = x.shape[0]
    tok = lambda w: pl.BlockSpec((tm, w), lambda i: (i, 0))
    attn_transposed = ya.ndim == 3
    if attn_transposed:
        tps = ya.shape[2] // tm
        ya_spec = pl.BlockSpec((None, Q_PAD, tm), lambda i: (i // tps, 0, i % tps))
    else:
        ya_spec = tok(Q_PAD)
    return pl.pallas_call(
        functools.partial(_outproj_ln_kernel, attn_transposed),
        grid=(n // tm,),
        in_specs=[tok(D_MODEL), tok(D_CONV), tok(D_LRU), ya_spec,
                  _const_spec((D_CONV + D_LRU + Q_PAD, D_MODEL)),
                  _const_spec((1, D_MODEL)), _const_spec((1, D_MODEL))] + _ffn_weight_specs(),
        out_specs=tok(D_MODEL),
        out_shape=jax.ShapeDtypeStruct((n, D_MODEL), F32),
        compiler_params=_cparams(("parallel",)),
        name="outproj_ffn",
    )(x, yc, yl, ya, ow["w"], ow["g"], ow["b"], *_ffn_weight_args(fw))


def _rope_tables(pos):
    half = ROPE_DIM // 2
    inv = ROPE_THETA ** (-jnp.arange(half, dtype=F32) / half)
    ang = pos.astype(F32)[:, None] * inv[None, :]
    cos, sin = jnp.cos(ang), jnp.sin(ang)
    n = pos.shape[0]
    rest = HEAD_DIM - ROPE_DIM
    zeros8 = jnp.zeros((n, half), F32)
    c = jnp.concatenate([cos, cos, jnp.ones((n, rest), F32)], axis=1)
    s1 = jnp.concatenate([zeros8, sin, jnp.zeros((n, rest), F32)], axis=1)
    s2 = jnp.concatenate([-sin, zeros8, jnp.zeros((n, rest), F32)], axis=1)
    rep = LANES // HEAD_DIM
    return tuple(jnp.tile(a, (1, rep)) for a in (c, s1, s2))


def _head_pad_index():
    h = np.arange(D_ATTN) // HEAD_DIM
    d = np.arange(D_ATTN) % HEAD_DIM
    return h * PAD_HEAD + (h // GROUP) * HEAD_DIM + d


def _prep_layer(l, ln_g, ln_b, ffn_w_in, ffn_w_out, w_in, conv_w, conv_b, conv_ln_g, conv_ln_b,
                lru_conv_w, lru_conv_b, lru_w_gate, lru_b_gate, lru_lambda,
                cmp_pe, cmp_w1, cmp_b1, cmp_w2, cmp_b2, w_out):
    row = lambda v: v.reshape(1, -1).astype(F32)
    ffn = []
    for f, ln_i in ((0, 0), (1, 2)):
        wi = ffn_w_in[l, f]
        wg = wi[:, :D_FF].reshape(D_MODEL, N_FF_CHUNKS, FF_CHUNK).transpose(1, 0, 2).astype(BF16)
        wu = wi[:, D_FF:].reshape(D_MODEL, N_FF_CHUNKS, FF_CHUNK).transpose(1, 0, 2).astype(BF16)
        wo = ffn_w_out[l, f].reshape(N_FF_CHUNKS, FF_CHUNK, D_MODEL).astype(BF16)
        ffn.append({"wg": wg, "wu": wu, "wo": wo, "g": row(ln_g[l, ln_i]), "b": row(ln_b[l, ln_i])})

    wl = w_in[l]
    o_q = 2 * D_CONV + 2 * D_LRU
    o_kv = o_q + D_ATTN
    o_g = o_kv + 3 * KV_ROW
    pad_idx = _head_pad_index()
    wq = jnp.zeros((D_MODEL, Q_PAD), F32).at[:, pad_idx].set(wl[:, o_q:o_kv])
    hh = np.arange(3 * N_HEADS) // 3
    gate_idx = (hh // GROUP) * LANES + (hh % GROUP) * 3 + np.arange(3 * N_HEADS) % 3
    wgt = jnp.zeros((D_MODEL, N_KV_HEADS * LANES), F32).at[:, gate_idx].set(wl[:, o_g:])
    w_all = jnp.concatenate([wl[:, :o_q], wq, wl[:, o_kv:o_g], wgt], axis=1).astype(BF16)
    w_t = jnp.concatenate([w_all[:, C_Q:C_KVC], w_all[:, C_KVS:C_GATE], w_all[:, C_KVC:C_KVS],
                           w_all[:, C_GATE:]], axis=1).T

    conv = {"w": jnp.pad(conv_w[l], ((0, CONV_HALO - CONV_WIDTH), (0, 0))),
            "b": row(conv_b[l]), "ln_g": row(conv_ln_g[l]), "ln_b": row(conv_ln_b[l])}

    def blockdiag(w):
        out = jnp.zeros((D_LRU, D_LRU), F32)
        for n in range(LRU_BLOCKS):
            out = out.at[n * LRU_BW:(n + 1) * LRU_BW, n * LRU_BW:(n + 1) * LRU_BW].set(w[n])
        return out.astype(BF16)

    lru = {"cw": jnp.pad(lru_conv_w[l], ((0, LRU_HALO - LRU_CONV_WIDTH), (0, 0))),
           "cb": row(lru_conv_b[l]),
           "wr": blockdiag(lru_w_gate[l, 0]), "wi": blockdiag(lru_w_gate[l, 1]),
           "bg": lru_b_gate[l].astype(F32), "lam": row(lru_lambda[l])}

    pe = cmp_pe[l].reshape(2, 2, CMP_STRIDE * HEAD_DIM)
    pe_rows = jnp.stack([jnp.stack([pe[g // 2, j] for g in range(4)]) for j in range(2)])
    w2e = jnp.zeros((4, CMP_HIDDEN, KV_ROW), F32)
    for g in range(4):
        w2e = w2e.at[g, :, g * HEAD_DIM:(g + 1) * HEAD_DIM].set(cmp_w2[l, g // 2])
    cmp = {"pe": pe_rows.astype(F32),
           "w1": cmp_w1[l].reshape(2, 2, CMP_STRIDE * HEAD_DIM, CMP_HIDDEN).astype(BF16),
           "b1": cmp_b1[l].astype(F32),
           "w2": w2e.astype(BF16),
           "b2": jnp.concatenate([cmp_b2[l, 0], cmp_b2[l, 0], cmp_b2[l, 1], cmp_b2[l, 1]]).reshape(1, -1)}

    wo = w_out[l]
    wo_attn = jnp.zeros((Q_PAD, D_MODEL), F32).at[pad_idx, :].set(wo[D_CONV + D_LRU:])
    out = {"w": jnp.concatenate([wo[:D_CONV + D_LRU], wo_attn], axis=0).astype(BF16),
           "g": row(ln_g[l, 1]), "b": row(ln_b[l, 1])}
    return {"ffn": ffn, "w_all": w_all, "w_t": w_t, "conv": conv, "lru": lru, "cmp": cmp, "out": out}


def _pad_front(a, rows):
    return jnp.pad(a, ((0, 0), (rows - a.shape[1], 0), (0, 0)))


def _kv6(a, lead):
    return a.reshape(lead + (2, N_KV_HEADS, HEAD_DIM))


TM_PROMPT = 512
TC_PROMPT = 512
TL_PROMPT = 256


def _layer_prompt(x, bsz, t, lw, tabs):
    n = bsz * t
    x = _ffn_ln(x, lw["ffn"][0], TM_PROMPT)
    tabs_t = tuple(a.T for a in tabs)
    (u, lx, lg, kvc, ks, kw, kvc_t, kvs_t, kvw_t, qc_t, qr_t, vs_t, vw_t, gates_t) = _inproj_t(
        x, lw["w_all"], lw["w_t"], tabs, tabs_t, bsz, t, TM_PROMPT)
    s3 = lambda a: a.reshape(bsz, t, a.shape[-1])
    u3, lx3 = s3(u), s3(lx)
    yc = _conv_group(u3, jnp.zeros((bsz, CONV_HALO, D_CONV), F32), lw["conv"], TC_PROMPT)
    yl, h_last = _lru_group(lx3, s3(lg), jnp.zeros((bsz, LRU_HALO, D_LRU), F32),
                            jnp.zeros((bsz, LRU_HALO, D_LRU), F32), lw["lru"], TL_PROMPT, TL_PROMPT - 1)
    kc, vc_t = _compress_prompt(s3(kvc), lw["cmp"])
    ocmp_t, mneg = _cmp_topk_prompt(qc_t, kc, vc_t)
    ya_t = _selwin_prompt(qr_t, mneg, s3(ks), vs_t, s3(kw), vw_t, ocmp_t, gates_t)
    x = _outproj_ffn(x, yc.reshape(n, D_CONV), yl.reshape(n, D_LRU), ya_t, lw["out"], lw["ffn"][1], TM_PROMPT)
    leaf = lambda a: a.reshape(bsz, 2, N_KV_HEADS, HEAD_DIM, a.shape[-1]).transpose(0, 4, 1, 2, 3)
    state = (u3[:, t - (CONV_WIDTH - 1):], lx3[:, t - (LRU_CONV_WIDTH - 1):], h_last[:, 0],
             leaf(kvc_t), leaf(kvs_t), leaf(kvw_t[:, :, t - min(WINDOW, t):]))
    return x, state


T_PAD = 8


def _layer_sample(x, bsz, t, lw, tabs, conv_buf, lru_buf, lru_h, cache_cmp_t, cache_sel_t, win_t, layer, page_table):
    n = bsz * t
    x = _ffn_ln(x, lw["ffn"][0], n)
    (u, lx, lg, qc, qr, kvc, kvs, kvs_b, kvw, kvw_b, gates) = _inproj(x, lw["w_all"], tabs, n, 1)
    s3 = lambda a: a.reshape(bsz, t, a.shape[-1])
    padt = lambda a: jnp.pad(s3(a), ((0, 0), (0, T_PAD - t), (0, 0)))
    yc = _conv_group(padt(u), _pad_front(conv_buf, CONV_HALO), lw["conv"], T_PAD)[:, :t]
    h0 = jnp.broadcast_to(lru_h[:, None, :], (bsz, LRU_HALO, D_LRU))
    yl, h_all = _lru_group(padt(lx), padt(lg), _pad_front(lru_buf, LRU_HALO), h0, lw["lru"], T_PAD, t - 1)
    yl = yl[:, :t]

    def head_cols(a):
        a = a.reshape(bsz, t, N_HEADS, PAD_HEAD).transpose(0, 2, 1, 3).reshape(bsz, N_HEADS * t, PAD_HEAD)
        return jnp.pad(a, ((0, 0), (0, S_COLS - N_HEADS * t), (0, 0)))

    ocmp_cols, picks = _cmp_topk_sample(page_table, cache_cmp_t, layer, head_cols(qc), lw["cmp"], t)
    pk = picks[:, :, :N_HEADS * t].reshape(bsz, N_SEL, N_KV_HEADS, GROUP, t)[:, :, :, 0, :]
    idx_flat = pk.transpose(0, 3, 2, 1).reshape(-1).astype(I32)

    def head_rows(a, dt):
        a = a.reshape(bsz, N_KV_HEADS, GROUP, t, PAD_HEAD).transpose(0, 3, 1, 2, 4)
        return jnp.pad(a, ((0, 0), (0, 0), (0, 0), (0, HROWS - GROUP), (0, 0))).astype(dt)

    q_rows = head_rows(qr.reshape(bsz, t, N_HEADS, PAD_HEAD).transpose(0, 2, 1, 3).reshape(bsz, N_HEADS * t, PAD_HEAD), BF16)
    ocmp_rows = head_rows(ocmp_cols[:, :N_HEADS * t], F32)
    g3 = gates.reshape(bsz, t, N_KV_HEADS, LANES)[..., :3 * GROUP].reshape(bsz, t, N_KV_HEADS, GROUP, 3)
    gate_rows = jnp.pad(g3, ((0, 0), (0, 0), (0, 0), (0, HROWS - GROUP), (0, LANES - 3)))
    pos_last = lambda a: s3(a).transpose(0, 2, 1)
    pad_cols = lambda a: jnp.pad(a, ((0, 0), (0, 0), (0, LANES - t)))
    kvs_t, kvw_t = pos_last(kvs), pos_last(kvw)
    ya_rows = _selwin_sample(idx_flat, page_table, cache_sel_t, win_t, layer, q_rows, pad_cols(kvs_t),
                             pad_cols(kvw_t), ocmp_rows, gate_rows)
    ya = ya_rows[:, :, :, :GROUP].reshape(bsz, t, Q_PAD)

    x = _outproj_ffn(x, yc.reshape(n, D_CONV), yl.reshape(n, D_LRU), ya.reshape(n, Q_PAD), lw["out"], lw["ffn"][1], n)
    new_conv = jnp.concatenate([conv_buf, s3(u)], axis=1)[:, t:]
    new_lru = jnp.concatenate([lru_buf, s3(lx)], axis=1)[:, t:]
    win_all = jnp.concatenate([win_t[layer], kvw_t], axis=-1)
    n_win = min(WINDOW, win_all.shape[-1])
    new_win = win_all[..., win_all.shape[-1] - n_win:]
    new_win = new_win.reshape(bsz, 2, N_KV_HEADS, HEAD_DIM, n_win).transpose(0, 4, 1, 2, 3)
    state = (new_conv, new_lru, h_all[:, 0], _kv6(kvc, (bsz, t)), _kv6(kvs, (bsz, t)), new_win)
    return x, state


def kernel(x_prompt, x_sample, state_conv, state_lru_conv, state_lru_h, cache_cmp_kv, cache_sel_kv, cache_win_kv,
           page_table, ln_g, ln_b, ffn_w_in, ffn_w_out, w_in, conv_w, conv_b, conv_ln_g, conv_ln_b,
           lru_conv_w, lru_conv_b, lru_w_gate, lru_b_gate, lru_lambda, cmp_pe, cmp_w1, cmp_b1, cmp_w2, cmp_b2, w_out):
    bp, tp, _ = x_prompt.shape
    bs, ts, _ = x_sample.shape
    depth = ln_g.shape[0]
    past = page_table.shape[1] * PAGE_SIZE
    assert past == PAST_LEN and past % SEL_BLOCK == 0
    tabs_p = _rope_tables(jnp.arange(tp))
    tabs_s = _rope_tables(jnp.tile(past + jnp.arange(ts), bs))
    xp = x_prompt.reshape(bp * tp, D_MODEL)
    xs = x_sample.reshape(bs * ts, D_MODEL)
    n_pool = cache_cmp_kv.shape[1]
    pos_last = lambda c: jnp.transpose(c, (0, 1, 3, 4, 5, 2)).reshape(c.shape[:2] + (KV_ROW, c.shape[2]))
    cache_cmp_t, cache_sel_t, win_t = pos_last(cache_cmp_kv), pos_last(cache_sel_kv), pos_last(cache_win_kv)
    st_p, st_s = [], []
    for l in range(depth):
        lw = _prep_layer(l, ln_g, ln_b, ffn_w_in, ffn_w_out, w_in, conv_w, conv_b, conv_ln_g, conv_ln_b,
                         lru_conv_w, lru_conv_b, lru_w_gate, lru_b_gate, lru_lambda,
                         cmp_pe, cmp_w1, cmp_b1, cmp_w2, cmp_b2, w_out)
        xp, sp = _layer_prompt(xp, bp, tp, lw, tabs_p)
        xs, ss = _layer_sample(
            xs, bs, ts, lw, tabs_s, state_conv[l], state_lru_conv[l], state_lru_h[l],
            cache_cmp_t, cache_sel_t, win_t, l, page_table)
        st_p.append(sp)
        st_s.append(ss)
    outs = [xp.reshape(bp, tp, D_MODEL), xs.reshape(bs, ts, D_MODEL)]
    for k in range(6):
        outs.append(jnp.stack([s[k] for s in st_p]))
        outs.append(jnp.stack([s[k] for s in st_s]))
    return tuple(outs)
```

```python
import functools

import numpy as np
import jax
import jax.numpy as jnp
from jax import lax
from jax.experimental import pallas as pl
from jax.experimental.pallas import tpu as pltpu

F32 = jnp.float32
BF16 = jnp.bfloat16
I32 = jnp.int32

D_MODEL = 1024
DEPTH = 2
PAST_LEN = 16384
PAGE_SIZE = 128
D_CONV = 256
CONV_WIDTH = 31
D_LRU = 256
LRU_BLOCKS = 4
LRU_BW = D_LRU // LRU_BLOCKS
LRU_CONV_WIDTH = 4
LRU_C = 8.0
D_ATTN = 512
N_HEADS = 8
HEAD_DIM = 64
N_KV_HEADS = 2
GROUP = N_HEADS // N_KV_HEADS
KV_ROW = 2 * N_KV_HEADS * HEAD_DIM
ROPE_DIM = 16
ROPE_THETA = 500000.0
CMP_BLOCK = 32
CMP_STRIDE = 16
CMP_HIDDEN = 256
SEL_BLOCK = 64
N_SEL = 16
WINDOW = 512
Q_BLOCK = 128
D_FF = 2816
ALPHA = (2 * DEPTH) ** 0.25
LN_EPS = 1e-5
SCALE = HEAD_DIM ** -0.5

LANES = 128
SUBLANES = 8
VMEM_LIMIT = 56 * 1024 * 1024
FF_CHUNK = 256
N_FF_CHUNKS = D_FF // FF_CHUNK
PAD_HEAD = 128
Q_PAD = N_HEADS * PAD_HEAD
KV_GROUP_LANES = GROUP * PAD_HEAD
MASK_NEG = -(2.0 ** 60)
SEL_TILE = 512
SELWIN_QUERIES = 256

C_GLU = 0
C_LRUX = 512
C_LRUG = 768
C_Q = 1024
C_KVC = C_Q + Q_PAD
C_KVS = C_KVC + KV_ROW
C_KVW = C_KVS + KV_ROW
C_GATE = C_KVW + KV_ROW
N_PROJ = C_GATE + N_KV_HEADS * LANES


def _cparams(sem):
    return pltpu.CompilerParams(dimension_semantics=sem, vmem_limit_bytes=VMEM_LIMIT)


def _const_spec(shape):
    nd = len(shape)
    return pl.BlockSpec(shape, lambda *_: (0,) * nd, pipeline_mode=pl.Buffered(1))


def _layernorm(y, g, b):
    mu = jnp.mean(y, axis=-1, keepdims=True)
    d = y - mu
    var = jnp.mean(d * d, axis=-1, keepdims=True)
    return d * lax.rsqrt(var + LN_EPS) * g + b


def _dot(a, b):
    return jnp.dot(a, b, preferred_element_type=F32)


def _dot_nt(a, b):
    return lax.dot_general(a, b, (((1,), (1,)), ((), ())), preferred_element_type=F32)


def _dot_tn(a, b):
    return lax.dot_general(a, b, (((0,), (0,)), ((), ())), preferred_element_type=F32)


def _ffn_ln_apply(x, wg_ref, wu_ref, wo_ref, g_ref, b_ref):
    xb = x.astype(BF16)
    acc = jnp.zeros(x.shape, F32)
    for c in range(N_FF_CHUNKS):
        gate = _dot(xb, wg_ref[c])
        up = _dot(xb, wu_ref[c])
        h = (gate * jax.nn.sigmoid(gate)) * up
        acc = acc + _dot(h.astype(BF16), wo_ref[c])
    y = ALPHA * x + 0.5 * acc
    return _layernorm(y, g_ref[...], b_ref[...])


def _ffn_ln_kernel(x_ref, wg_ref, wu_ref, wo_ref, g_ref, b_ref, o_ref):
    o_ref[...] = _ffn_ln_apply(x_ref[...], wg_ref, wu_ref, wo_ref, g_ref, b_ref)


def _ffn_weight_specs():
    return [
        _const_spec((N_FF_CHUNKS, D_MODEL, FF_CHUNK)),
        _const_spec((N_FF_CHUNKS, D_MODEL, FF_CHUNK)),
        _const_spec((N_FF_CHUNKS, FF_CHUNK, D_MODEL)),
        _const_spec((1, D_MODEL)),
        _const_spec((1, D_MODEL)),
    ]


def _ffn_weight_args(fw):
    return (fw["wg"], fw["wu"], fw["wo"], fw["g"], fw["b"])


def _ffn_ln(x, fw, tm):
    n = x.shape[0]
    return pl.pallas_call(
        _ffn_ln_kernel,
        grid=(n // tm,),
        in_specs=[pl.BlockSpec((tm, D_MODEL), lambda i: (i, 0))] + _ffn_weight_specs(),
        out_specs=pl.BlockSpec((tm, D_MODEL), lambda i: (i, 0)),
        out_shape=jax.ShapeDtypeStruct((n, D_MODEL), F32),
        compiler_params=_cparams(("parallel",)),
        name="ffn_ln",
    )(x, *_ffn_weight_args(fw))


def _rope(v, cos, s1, s2):
    return v * cos + pltpu.roll(v, 8, 1) * s1 + pltpu.roll(v, LANES - 8, 1) * s2


def _inproj_kernel(x_ref, w_ref, cos_ref, s1_ref, s2_ref,
                   u_ref, lx_ref, lg_ref, qc_ref, qr_ref,
                   kvc_ref, kvs_ref, kvsb_ref, kvw_ref, kvwb_ref, gt_ref):
    xb = x_ref[...].astype(BF16)

    def mm(lo, hi):
        return _dot(xb, w_ref[:, lo:hi])

    glu = mm(C_GLU, C_GLU + 2 * D_CONV)
    u_ref[...] = glu[:, :D_CONV] * jax.nn.sigmoid(glu[:, D_CONV:])
    lx_ref[...] = mm(C_LRUX, C_LRUX + D_LRU)
    lg_ref[...] = mm(C_LRUG, C_LRUG + D_LRU)
    cos = cos_ref[...]
    s1 = s1_ref[...]
    s2 = s2_ref[...]
    for h in range(N_HEADS):
        qh = mm(C_Q + h * PAD_HEAD, C_Q + (h + 1) * PAD_HEAD) * SCALE
        qc_ref[:, h * PAD_HEAD:(h + 1) * PAD_HEAD] = qh.astype(BF16)
        qr_ref[:, h * PAD_HEAD:(h + 1) * PAD_HEAD] = _rope(qh, cos, s1, s2).astype(BF16)
    kvc_ref[...] = mm(C_KVC, C_KVC + KV_ROW)
    for c0, f_ref, b_ref in ((C_KVS, kvs_ref, kvsb_ref), (C_KVW, kvw_ref, kvwb_ref)):
        kv = mm(c0, c0 + KV_ROW)
        k = _rope(kv[:, :LANES], cos, s1, s2)
        v = kv[:, LANES:]
        f_ref[:, 0:LANES] = k
        f_ref[:, LANES:KV_ROW] = v
        b_ref[:, 0:LANES] = k.astype(BF16)
        b_ref[:, LANES:KV_ROW] = v.astype(BF16)
    gt_ref[...] = jax.nn.sigmoid(mm(C_GATE, N_PROJ))


def _rope_t(v, cos, s1, s2):
    return v * cos + pltpu.roll(v, 8, 0) * s1 + pltpu.roll(v, v.shape[0] - 8, 0) * s2


R_Q = 0
R_VS = D_ATTN
R_VW = R_VS + LANES
R_GATE = R_VW + LANES
N_PROJ_T = R_GATE + N_KV_HEADS * LANES


LOG2E = 1.4426950408889634
SUM_ROWS = (HEAD_DIM, 0)


def _inproj_t_kernel(x_ref, w_ref, wt_ref, cos_ref, s1_ref, s2_ref, cos_t_ref, s1_t_ref, s2_t_ref, blk_ref,
                     u_ref, lx_ref, lg_ref, kvc_ref, ks_ref, kw_ref,
                     kvct_ref, kvst_ref, kvwt_ref, qc_ref, qr_ref, vs_ref, vw_ref, gt_ref):
    xb = x_ref[...].astype(BF16)
    tm = xb.shape[0]

    def mm(lo, hi):
        return _dot(xb, w_ref[:, lo:hi])

    glu = mm(C_GLU, C_GLU + 2 * D_CONV)
    u_ref[...] = glu[:, :D_CONV] * jax.nn.sigmoid(glu[:, D_CONV:])
    lx_ref[...] = mm(C_LRUX, C_LRUX + D_LRU)
    lg_ref[...] = mm(C_LRUG, C_LRUG + D_LRU)
    kvc = mm(C_KVC, C_KVC + KV_ROW)
    kvc_ref[...] = kvc
    kvct_ref[...] = kvc.T
    cos = cos_ref[...]
    s1 = s1_ref[...]
    s2 = s2_ref[...]
    for c0, b_ref, leaf_ref in ((C_KVS, ks_ref, kvst_ref), (C_KVW, kw_ref, kvwt_ref)):
        k = _rope(mm(c0, c0 + LANES), cos, s1, s2)
        b_ref[:, 0:LANES] = k.astype(BF16)
        leaf_ref[0:LANES, :] = k.T
    ks_ref[:, LANES:] = blk_ref[...]

    all_t = _dot_nt(wt_ref[...], xb)
    cos_t = cos_t_ref[0:HEAD_DIM, :]
    s1_t = s1_t_ref[0:HEAD_DIM, :]
    s2_t = s2_t_ref[0:HEAD_DIM, :]
    no_rows = jnp.zeros((HEAD_DIM, tm), BF16)
    for h in range(N_HEADS):
        qh = all_t[R_Q + h * HEAD_DIM:R_Q + (h + 1) * HEAD_DIM, :]
        own = h * PAD_HEAD + (h // GROUP) * HEAD_DIM
        other = h * PAD_HEAD + (1 - h // GROUP) * HEAD_DIM
        qc_ref[own:own + HEAD_DIM, :] = (qh * SCALE).astype(BF16)
        qr_ref[own:own + HEAD_DIM, :] = (_rope_t(qh, cos_t, s1_t, s2_t) * (SCALE * LOG2E)).astype(BF16)
        qc_ref[other:other + HEAD_DIM, :] = no_rows
        qr_ref[other:other + HEAD_DIM, :] = no_rows
    row = lax.broadcasted_iota(I32, (LANES, tm), 0)
    for r0, leaf_ref, o_ref in ((R_VS, kvst_ref, vs_ref), (R_VW, kvwt_ref, vw_ref)):
        vt = all_t[r0:r0 + LANES, :]
        leaf_ref[LANES:KV_ROW, :] = vt
        per_head = (jnp.where(row < HEAD_DIM, vt, jnp.where(row == SUM_ROWS[0], 1.0, 0.0)),
                    jnp.where(row >= HEAD_DIM, vt, jnp.where(row == SUM_ROWS[1], 1.0, 0.0)))
        for kv, vk in enumerate(per_head):
            vk = vk.astype(BF16)
            for c in range(tm // LANES):
                o_ref[kv, c] = vk[:, c * LANES:(c + 1) * LANES]
    gt_ref[...] = jax.nn.sigmoid(all_t[R_GATE:N_PROJ_T, :])


def _inproj_t(x, w_all, w_t, rope_tabs, rope_tabs_t, bsz, t, tm):
    n = x.shape[0]
    tps = t // tm
    tok = lambda w: pl.BlockSpec((tm, w), lambda i: (i, 0))
    tab = pl.BlockSpec((tm, LANES), lambda i: (i % tps, 0))
    tab_t = pl.BlockSpec((PAD_HEAD, tm), lambda i: (0, i % tps))
    feat_t = lambda r: pl.BlockSpec((None, r, tm), lambda i: (i // tps, 0, i % tps))
    vt_spec = pl.BlockSpec((None, N_KV_HEADS, tm // LANES, LANES, LANES), lambda i: (i // tps, 0, i % tps, 0, 0))
    n_sel = t // SEL_BLOCK
    row_outs = [(D_CONV, F32), (D_LRU, F32), (D_LRU, F32), (KV_ROW, F32), (LANES + n_sel, BF16), (LANES, BF16)]
    leaf_t = jax.ShapeDtypeStruct((bsz, KV_ROW, t), F32)
    vt_shape = jax.ShapeDtypeStruct((bsz, N_KV_HEADS, t // LANES, LANES, LANES), BF16)
    block_id = (jnp.arange(t)[:, None] // SEL_BLOCK == jnp.arange(n_sel)[None, :]).astype(BF16)
    return pl.pallas_call(
        _inproj_t_kernel,
        grid=(n // tm,),
        in_specs=[tok(D_MODEL), _const_spec((D_MODEL, N_PROJ)), _const_spec((N_PROJ_T, D_MODEL)),
                  tab, tab, tab, tab_t, tab_t, tab_t,
                  pl.BlockSpec((tm, n_sel), lambda i: (i % tps, 0))],
        out_specs=[tok(w) for w, _ in row_outs]
        + [feat_t(KV_ROW)] * 3
        + [feat_t(Q_PAD), feat_t(Q_PAD), vt_spec, vt_spec, feat_t(N_KV_HEADS * LANES)],
        out_shape=[jax.ShapeDtypeStruct((n, w), dt) for w, dt in row_outs]
        + [leaf_t] * 3
        + [jax.ShapeDtypeStruct((bsz, Q_PAD, t), BF16), jax.ShapeDtypeStruct((bsz, Q_PAD, t), BF16),
           vt_shape, vt_shape, jax.ShapeDtypeStruct((bsz, N_KV_HEADS * LANES, t), F32)],
        compiler_params=_cparams(("parallel",)),
        name="inproj_t",
    )(x, w_all, w_t, *rope_tabs, *rope_tabs_t, block_id)


def _inproj(x, w_all, rope_tabs, tm, tiles_per_seq):
    n = x.shape[0]
    cos, s1, s2 = rope_tabs
    tok = lambda w: pl.BlockSpec((tm, w), lambda i: (i, 0))
    tab = pl.BlockSpec((tm, LANES), lambda i: (i % tiles_per_seq, 0))
    outs = [
        (D_CONV, F32), (D_LRU, F32), (D_LRU, F32), (Q_PAD, BF16), (Q_PAD, BF16),
        (KV_ROW, F32), (KV_ROW, F32), (KV_ROW, BF16), (KV_ROW, F32), (KV_ROW, BF16),
        (N_KV_HEADS * LANES, F32),
    ]
    return pl.pallas_call(
        _inproj_kernel,
        grid=(n // tm,),
        in_specs=[tok(D_MODEL), _const_spec((D_MODEL, N_PROJ)), tab, tab, tab],
        out_specs=[tok(w) for w, _ in outs],
        out_shape=[jax.ShapeDtypeStruct((n, w), dt) for w, dt in outs],
        compiler_params=_cparams(("parallel",)),
        name="inproj",
    )(x, w_all, cos, s1, s2)


CONV_HALO = 32
CONV_ROWS = 64


def _conv_kernel(u_ref, buf_ref, w_ref, b_ref, g_ref, bb_ref, o_ref, ext_ref, sh_ref):
    t = pl.program_id(1)
    tc = u_ref.shape[0]

    @pl.when(t == 0)
    def _():
        ext_ref[0:CONV_HALO, :] = buf_ref[...]

    ext_ref[CONV_HALO:CONV_HALO + tc, :] = u_ref[...]
    off = CONV_HALO - (CONV_WIDTH - 1)
    span = tc + CONV_HALO - SUBLANES
    for s in range(1, SUBLANES):
        sh_ref[s - 1, 0:span, :] = ext_ref[s:s + span, :]
    rows = min(CONV_ROWS, tc)
    for r0 in range(0, tc, rows):
        acc = jnp.zeros((rows, D_CONV), F32)
        for k in range(CONV_WIDTH):
            a, s = divmod(off + k, SUBLANES)
            src = ext_ref if s == 0 else sh_ref.at[s - 1]
            acc = acc + src[r0 + SUBLANES * a:r0 + SUBLANES * a + rows, :] * w_ref[k:k + 1, :]
        y = acc + b_ref[...]
        y = _layernorm(y, g_ref[...], bb_ref[...])
        o_ref[r0:r0 + rows, :] = (y * jax.nn.sigmoid(y)).astype(BF16)
    ext_ref[0:CONV_HALO, :] = ext_ref[tc:tc + CONV_HALO, :]


def _conv_group(u, buf, cw, tc):
    b, t, _ = u.shape
    return pl.pallas_call(
        _conv_kernel,
        grid=(b, t // tc),
        in_specs=[
            pl.BlockSpec((None, tc, D_CONV), lambda i, j: (i, j, 0)),
            pl.BlockSpec((None, CONV_HALO, D_CONV), lambda i, j: (i, 0, 0)),
            _const_spec((CONV_HALO, D_CONV)),
            _const_spec((1, D_CONV)), _const_spec((1, D_CONV)), _const_spec((1, D_CONV)),
        ],
        out_specs=pl.BlockSpec((None, tc, D_CONV), lambda i, j: (i, j, 0)),
        out_shape=jax.ShapeDtypeStruct((b, t, D_CONV), BF16),
        scratch_shapes=[pltpu.VMEM((tc + CONV_HALO, D_CONV), F32),
                        pltpu.VMEM((SUBLANES - 1, tc + CONV_HALO - SUBLANES, D_CONV), F32)],
        compiler_params=_cparams(("parallel", "arbitrary")),
        name="conv_group",
    )(u, buf, cw["w"], cw["b"], cw["ln_g"], cw["ln_b"])


LRU_HALO = 8


def _lru_kernel(last_row, x_ref, gate_ref, buf_ref, h0_ref, cw_ref, cb_ref, wr_ref, wi_ref,
                bg_ref, lam_ref, y_ref, hl_ref, ext_ref, hc_ref):
    t = pl.program_id(1)
    tl = x_ref.shape[0]

    @pl.when(t == 0)
    def _():
        ext_ref[0:LRU_HALO, :] = buf_ref[...]
        hc_ref[...] = h0_ref[...]

    ext_ref[LRU_HALO:LRU_HALO + tl, :] = x_ref[...]
    off = LRU_HALO - (LRU_CONV_WIDTH - 1)
    xl = jnp.zeros((tl, D_LRU), F32)
    for k in range(LRU_CONV_WIDTH):
        xl = xl + ext_ref[off + k:off + k + tl, :] * cw_ref[k:k + 1, :]
    xl = xl + cb_ref[...]
    ext_ref[0:LRU_HALO, :] = ext_ref[tl:tl + LRU_HALO, :]

    xb = xl.astype(BF16)
    r_gate = jax.nn.sigmoid(_dot(xb, wr_ref[...]) + bg_ref[0:1, :])
    i_gate = jax.nn.sigmoid(_dot(xb, wi_ref[...]) + bg_ref[1:2, :])
    log_a = LRU_C * r_gate * jax.nn.log_sigmoid(lam_ref[...])
    a = jnp.exp(log_a)
    bv = jnp.sqrt(-jnp.tanh(log_a) * (a * a + 1.0)) * (i_gate * xl)

    row = lax.broadcasted_iota(I32, (tl, D_LRU), 0)
    s = 1
    while s < tl:
        keep = row >= s
        a_sh = jnp.where(keep, pltpu.roll(a, s, 0), 1.0)
        b_sh = jnp.where(keep, pltpu.roll(bv, s, 0), 0.0)
        bv = a * b_sh + bv
        a = a * a_sh
        s *= 2
    h = a * hc_ref[0:1, :] + bv
    hc_ref[...] = jnp.broadcast_to(h[last_row:last_row + 1, :], hc_ref.shape)
    hl_ref[...] = hc_ref[...]
    y_ref[...] = (h * jax.nn.gelu(gate_ref[...])).astype(BF16)


def _lru_group(x, gate, buf, h0, lw, tl, last_row):
    b, t, _ = x.shape
    assert last_row == tl - 1 or t == tl
    seq = pl.BlockSpec((None, tl, D_LRU), lambda i, j: (i, j, 0))
    per_b = pl.BlockSpec((None, LRU_HALO, D_LRU), lambda i, j: (i, 0, 0))
    return pl.pallas_call(
        functools.partial(_lru_kernel, last_row),
        grid=(b, t // tl),
        in_specs=[seq, seq, per_b, per_b,
                  _const_spec((LRU_HALO, D_LRU)), _const_spec((1, D_LRU)),
                  _const_spec((D_LRU, D_LRU)), _const_spec((D_LRU, D_LRU)),
                  _const_spec((2, D_LRU)), _const_spec((1, D_LRU))],
        out_specs=[seq, per_b],
        out_shape=[jax.ShapeDtypeStruct((b, t, D_LRU), BF16),
                   jax.ShapeDtypeStruct((b, LRU_HALO, D_LRU), F32)],
        scratch_shapes=[pltpu.VMEM((tl + LRU_HALO, D_LRU), F32),
                        pltpu.VMEM((LRU_HALO, D_LRU), F32)],
        compiler_params=_cparams(("parallel", "arbitrary")),
        name="lru_group",
    )(x, gate, buf, h0, lw["cw"], lw["cb"], lw["wr"], lw["wi"], lw["bg"], lw["lam"])


def _compress_rows(row_refs, pe_ref, w1_ref, b1_ref, w2_ref, b2_ref, carry_ref):
    lo = lax.broadcasted_iota(I32, (1, LANES), 1) < HEAD_DIM
    xs = [[] for _ in range(4)]
    for halves in row_refs:
        n = halves[0].shape[0] // CMP_STRIDE
        cols = [[] for _ in range(4)]
        for rp in range(CMP_STRIDE // 2):
            for half, ref in enumerate(halves):
                pa = ref[pl.ds(2 * rp, n, stride=CMP_STRIDE), :]
                pb = ref[pl.ds(2 * rp + 1, n, stride=CMP_STRIDE), :]
                cols[2 * half].append(jnp.where(lo, pa, pltpu.roll(pb, HEAD_DIM, 1)))
                cols[2 * half + 1].append(jnp.where(lo, pltpu.roll(pa, HEAD_DIM, 1), pb))
        for g in range(4):
            xs[g].append(jnp.concatenate(cols[g], axis=1))
    out = None
    for g in range(4):
        sidx = g // 2
        x = jnp.concatenate(xs[g], axis=0)
        p0 = _dot((x + pe_ref[0, g:g + 1, :]).astype(BF16), w1_ref[sidx, 0])
        p1 = _dot((x + pe_ref[1, g:g + 1, :]).astype(BF16), w1_ref[sidx, 1])
        n_tot = p0.shape[0]
        row = lax.broadcasted_iota(I32, p0.shape, 0)
        p0s = jnp.where(row == 0, carry_ref[g, 0:1, :], pltpu.roll(p0, 1, 0))
        carry_ref[g, 0:1, :] = p0[n_tot - 1:n_tot, :]
        h = (b1_ref[sidx:sidx + 1, :] + p0s) + p1
        part = _dot(jax.nn.gelu(h).astype(BF16), w2_ref[g])
        out = part if out is None else out + part
    return out + b2_ref[...]


def _cmp_weight_specs():
    return [
        _const_spec((2, 4, CMP_STRIDE * HEAD_DIM)),
        _const_spec((2, 2, CMP_STRIDE * HEAD_DIM, CMP_HIDDEN)),
        _const_spec((2, CMP_HIDDEN)),
        _const_spec((4, CMP_HIDDEN, KV_ROW)),
        _const_spec((1, KV_ROW)),
    ]


def _cmp_weight_args(cw):
    return (cw["pe"], cw["w1"], cw["b1"], cw["w2"], cw["b2"])


CMP_TILE_ROWS = 2048


def _compress_prompt_kernel(k_ref, v_ref, pe_ref, w1_ref, b1_ref, w2_ref, b2_ref, kc_ref, vct_ref, carry_ref):
    @pl.when(pl.program_id(1) == 0)
    def _():
        carry_ref[...] = jnp.zeros(carry_ref.shape, F32)

    out = _compress_rows([(k_ref, v_ref)], pe_ref, w1_ref, b1_ref, w2_ref, b2_ref, carry_ref)
    kc_ref[...] = out[:, 0:LANES].astype(BF16)
    vct_ref[...] = out[:, LANES:KV_ROW].T.astype(BF16)


def _compress_prompt(kvc, cw):
    b, t, _ = kvc.shape
    n_e = CMP_TILE_ROWS // CMP_STRIDE
    return pl.pallas_call(
        _compress_prompt_kernel,
        grid=(b, t // CMP_TILE_ROWS),
        in_specs=[pl.BlockSpec((None, CMP_TILE_ROWS, LANES), lambda i, j: (i, j, 0)),
                  pl.BlockSpec((None, CMP_TILE_ROWS, LANES), lambda i, j: (i, j, 1))]
        + _cmp_weight_specs(),
        out_specs=[pl.BlockSpec((None, n_e, LANES), lambda i, j: (i, j, 0)),
                   pl.BlockSpec((None, LANES, n_e), lambda i, j: (i, 0, j))],
        out_shape=[jax.ShapeDtypeStruct((b, t // CMP_STRIDE, LANES), BF16),
                   jax.ShapeDtypeStruct((b, LANES, t // CMP_STRIDE), BF16)],
        scratch_shapes=[pltpu.VMEM((4, 8, CMP_HIDDEN), F32)],
        compiler_params=_cparams(("parallel", "arbitrary")),
        name="compress_prompt",
    )(kvc, kvc, *_cmp_weight_args(cw))


def _selection_scores(pk_ref, n_sel):
    ratio = SEL_BLOCK // CMP_STRIDE
    slc = pk_ref[pl.ds(0, n_sel, stride=ratio), :]
    for o in range(1, ratio):
        slc = slc + 2.0 * pk_ref[pl.ds(o, n_sel, stride=ratio), :]
    return slc + pk_ref[pl.ds(ratio, n_sel, stride=ratio), :]


def _topk_rounds(score, n_rows):
    j = lax.broadcasted_iota(I32, score.shape, 0).astype(F32)
    sel = jnp.zeros(score.shape, F32)
    picks = []
    for _ in range(N_SEL):
        cm = jnp.max(score, axis=0, keepdims=True)
        mi = jnp.min(jnp.where(score == cm, j, float(n_rows)), axis=0, keepdims=True)
        hit = j == mi
        sel = jnp.where(hit, 1.0, sel)
        score = jnp.where(hit, -jnp.inf, score)
        picks.append(mi)
    return sel, picks


def _masked_softmax_rows(s_t, valid):
    s_t = jnp.where(valid, s_t, -jnp.inf)
    m = jnp.max(s_t, axis=0, keepdims=True)
    m = jnp.where(m > -jnp.inf, m, 0.0)
    e = jnp.exp(s_t - m)
    d = jnp.sum(e, axis=0, keepdims=True)
    return e / jnp.where(d > 0, d, 1.0)


CMP_ENTRY_CHUNK = 128


def _cmp_topk_prompt_kernel(qc_ref, kc_ref, vct_ref, ocmp_ref, mneg_ref, pk0_ref, pk1_ref):
    i = pl.program_id(1)
    n_e = kc_ref.shape[0]
    n_sel = n_e // (SEL_BLOCK // CMP_STRIDE)
    pk_refs = (pk0_ref, pk1_ref)

    def attend(n_use):
        kc = kc_ref[0:n_use, :]
        vct = vct_ref[:, 0:n_use]
        e_idx = lax.broadcasted_iota(I32, (n_use, Q_BLOCK), 0)
        qpos = i * Q_BLOCK + lax.broadcasted_iota(I32, (n_use, Q_BLOCK), 1)
        valid = (e_idx >= 1) & (CMP_STRIDE * e_idx + (CMP_STRIDE - 1) <= qpos)
        for kv in range(N_KV_HEADS):
            pkv = jnp.zeros((n_use, Q_BLOCK), F32)
            for h in range(kv * GROUP, (kv + 1) * GROUP):
                rows = slice(h * PAD_HEAD, (h + 1) * PAD_HEAD)
                p = _masked_softmax_rows(_dot(kc, qc_ref[rows, :]), valid)
                pkv = pkv + p
                ocmp_ref[rows, :] = _dot(vct, p.astype(BF16))
            pk_refs[kv][0:n_use, :] = pkv
            pk_refs[kv][n_use:, :] = jnp.zeros((n_e + 8 - n_use, Q_BLOCK), F32)

    n_var = n_e // CMP_ENTRY_CHUNK
    variant = jnp.minimum((8 * i + 7) // CMP_ENTRY_CHUNK, n_var - 1)
    for k in range(n_var):
        pl.when(variant == k)(functools.partial(attend, CMP_ENTRY_CHUNK * (k + 1)))

    j = lax.broadcasted_iota(I32, (n_sel, Q_BLOCK), 0)
    qp = i * Q_BLOCK + lax.broadcasted_iota(I32, (n_sel, Q_BLOCK), 1)
    qblk = jnp.right_shift(qp, 6)
    forced = (j == 0) | (j == qblk) | (j == qblk - 1)
    scores = [jnp.where(forced, jnp.inf, jnp.where(j * SEL_BLOCK <= qp, _selection_scores(r, n_sel), -jnp.inf))
              for r in pk_refs]
    sel, _ = _topk_rounds(jnp.concatenate(scores, axis=1), n_sel)
    mneg = jnp.where(sel > 0, 0.0, MASK_NEG).astype(BF16)
    for kv in range(N_KV_HEADS):
        mneg_ref[kv] = mneg[:, kv * Q_BLOCK:(kv + 1) * Q_BLOCK]


def _cmp_topk_prompt(qc_t, kc, vc_t):
    b, _, t = qc_t.shape
    n_e = kc.shape[1]
    n_sel = t // SEL_BLOCK
    assert n_e % CMP_ENTRY_CHUNK == 0
    qblk = pl.BlockSpec((None, Q_PAD, Q_BLOCK), lambda bi, i: (bi, 0, i))
    return pl.pallas_call(
        _cmp_topk_prompt_kernel,
        grid=(b, t // Q_BLOCK),
        in_specs=[
            qblk,
            pl.BlockSpec((None, n_e, LANES), lambda bi, i: (bi, 0, 0)),
            pl.BlockSpec((None, LANES, n_e), lambda bi, i: (bi, 0, 0)),
        ],
        out_specs=[
            qblk,
            pl.BlockSpec((None, N_KV_HEADS, n_sel, Q_BLOCK), lambda bi, i: (bi, 0, 0, i)),
        ],
        out_shape=[jax.ShapeDtypeStruct((b, Q_PAD, t), F32),
                   jax.ShapeDtypeStruct((b, N_KV_HEADS, n_sel, t), BF16)],
        scratch_shapes=[pltpu.VMEM((n_e + 8, Q_BLOCK), F32), pltpu.VMEM((n_e + 8, Q_BLOCK), F32)],
        compiler_params=_cparams(("parallel", "arbitrary")),
        name="cmp_topk_prompt",
    )(qc_t, kc, vc_t)


def _selwin_prompt_kernel(qr_ref, mneg_ref, ks_ref, vs_ref, kw_ref, vw_ref, ocmp_ref, gt_ref, o_ref,
                          qa_ref, sa_ref, sb_ref, pa_ref, pb_ref):
    i = pl.program_id(1)
    n_sel = mneg_ref.shape[1]
    qb = qr_ref.shape[1]
    win_keys = WINDOW + qb
    cols = N_HEADS * qb
    kv_cols = GROUP * qb
    for h in range(N_HEADS):
        c = slice(h * qb, (h + 1) * qb)
        qa_ref[0:PAD_HEAD, c] = qr_ref[h * PAD_HEAD:(h + 1) * PAD_HEAD, :]
        qa_ref[PAD_HEAD:PAD_HEAD + n_sel, c] = mneg_ref[h // GROUP]
    qa = qa_ref[...]
    q0 = i * qb
    qpos = q0 + (lax.broadcasted_iota(I32, (1, cols), 1) & (qb - 1))

    tile = SEL_TILE
    key_col = lax.broadcasted_iota(I32, (tile, 1), 0)

    def scores(t):
        return _dot(ks_ref[pl.ds(pl.multiple_of(t * tile, tile), tile), :], qa)

    def weighted_values(v_ref, first_tile, n_tiles, p):
        outs = []
        for kv in range(N_KV_HEADS):
            vk = v_ref[kv, pl.ds(first_tile, n_tiles)]
            vt = jnp.concatenate([vk[c] for c in range(n_tiles)], axis=1)
            outs.append(_dot(vt, p[:, kv * kv_cols:(kv + 1) * kv_cols]))
        return jnp.concatenate(outs, axis=1)

    def normalise(acc):
        sums = [acc[SUM_ROWS[kv]:SUM_ROWS[kv] + 1, kv * kv_cols:(kv + 1) * kv_cols] for kv in range(N_KV_HEADS)]
        return acc / jnp.concatenate(sums, axis=1)

    start = pl.multiple_of(jnp.maximum(q0 - WINDOW, 0), LANES)
    s = _dot(kw_ref[pl.ds(start, win_keys), :], qa[0:PAD_HEAD, :])
    dpos = qpos - (start + lax.broadcasted_iota(I32, (win_keys, 1), 0))
    s = jnp.where(lax.bitcast_convert_type(dpos, jnp.uint32) <= jnp.uint32(WINDOW), s, -jnp.inf)
    p = jnp.exp2(s - jnp.max(s, axis=0, keepdims=True))
    o_win = normalise(weighted_values(vw_ref, start // LANES, win_keys // LANES, p.astype(BF16)))

    sub = tile // LANES

    def stage(t, cur, nxt, carry, masked, prefetch):
        m, acc, alpha_prev = carry
        if prefetch:
            s_refs[nxt][...] = scores(t + 1)
        acc = alpha_prev * acc + weighted_values(vs_ref, jnp.maximum(t - 1, 0) * sub, sub, p_refs[nxt][...])
        s = s_refs[cur][...]
        if masked:
            s = jnp.where(t * tile + key_col <= qpos, s, MASK_NEG)
        m_new = jnp.maximum(m, jnp.max(s, axis=0, keepdims=True))
        alpha = jnp.exp2(m - m_new)
        p_refs[cur][...] = jnp.exp2(s - m_new).astype(BF16)
        return m_new, acc, alpha

    s_refs = (sa_ref, sb_ref)
    p_refs = (pa_ref, pb_ref)
    sa_ref[...] = scores(0)
    pb_ref[...] = jnp.zeros(pb_ref.shape, BF16)
    init = (jnp.full((1, cols), MASK_NEG, F32), jnp.zeros((LANES, cols), F32), jnp.ones((1, cols), F32))

    def pair(u, carry):
        carry = stage(2 * u, 0, 1, carry, False, True)
        return stage(2 * u + 1, 1, 0, carry, False, True)

    u_diag = q0 // (2 * tile)
    carry = lax.fori_loop(0, u_diag, pair, init)
    carry = stage(2 * u_diag, 0, 1, carry, True, True)

    def finish_second(c):
        _, acc, alpha = stage(2 * u_diag + 1, 1, 0, c, True, False)
        return alpha * acc + weighted_values(vs_ref, (2 * u_diag + 1) * sub, sub, pb_ref[...])

    def finish_first(c):
        _, acc, alpha = c
        return alpha * acc + weighted_values(vs_ref, 2 * u_diag * sub, sub, pa_ref[...])

    o_sel = normalise(lax.cond(q0 - 2 * u_diag * tile >= tile, finish_second, finish_first, carry))

    for h in range(N_HEADS):
        c = slice(h * qb, (h + 1) * qb)
        rows = slice(h * PAD_HEAD, (h + 1) * PAD_HEAD)
        g0 = (h // GROUP) * LANES + 3 * (h % GROUP)
        o = (gt_ref[g0:g0 + 1, :] * ocmp_ref[rows, :] + gt_ref[g0 + 1:g0 + 2, :] * o_sel[:, c]
             + gt_ref[g0 + 2:g0 + 3, :] * o_win[:, c])
        o_ref[rows, :] = o.astype(BF16)


def _selwin_prompt(qr_t, mneg, ks, vs_t, kw, vw_t, ocmp_t, gates_t):
    b, _, t = qr_t.shape
    n_sel = mneg.shape[2]
    qb = SELWIN_QUERIES
    assert t % (2 * SEL_TILE) == 0 and SEL_TILE % qb == 0 and t >= WINDOW + qb
    cols = N_HEADS * qb
    per_q = lambda r: pl.BlockSpec((None, r, qb), lambda bi, i: (bi, 0, i))
    per_b = lambda *shape: pl.BlockSpec((None,) + shape, lambda bi, i: (bi,) + (0,) * len(shape),
                                        pipeline_mode=pl.Buffered(1))
    vals = per_b(N_KV_HEADS, t // LANES, LANES, LANES)
    return pl.pallas_call(
        _selwin_prompt_kernel,
        grid=(b, t // qb),
        in_specs=[
            per_q(Q_PAD),
            pl.BlockSpec((None, N_KV_HEADS, n_sel, qb), lambda bi, i: (bi, 0, 0, i)),
            per_b(t, LANES + n_sel), vals, per_b(t, LANES), vals,
            per_q(Q_PAD), per_q(N_KV_HEADS * LANES),
        ],
        out_specs=per_q(Q_PAD),
        out_shape=jax.ShapeDtypeStruct((b, Q_PAD, t), BF16),
        scratch_shapes=[pltpu.VMEM((PAD_HEAD + n_sel, cols), BF16),
                        pltpu.VMEM((SEL_TILE, cols), F32),
                        pltpu.VMEM((SEL_TILE, cols), F32),
                        pltpu.VMEM((SEL_TILE, cols), BF16),
                        pltpu.VMEM((SEL_TILE, cols), BF16)],
        compiler_params=_cparams(("parallel", "arbitrary")),
        name="selwin_prompt",
    )(qr_t, mneg, ks, vs_t, kw, vw_t, ocmp_t, gates_t)


PAGES_PER_STEP = 16
S_COLS = LANES


def _cmp_topk_sample_kernel(n_t, pt_ref, *refs):
    page_refs = refs[:PAGES_PER_STEP]
    (q_ref, pe_ref, w1_ref, b1_ref, w2_ref, b2_ref,
     ocmp_ref, idx_ref, kcv_ref, pk_ref, carry_ref, xk_ref, xv_ref) = refs[PAGES_PER_STEP:]
    s = pl.program_id(1)
    n_steps = pl.num_programs(1)
    n_e = kcv_ref.shape[0]
    step_e = PAGES_PER_STEP * PAGE_SIZE // CMP_STRIDE

    @pl.when(s == 0)
    def _():
        carry_ref[...] = jnp.zeros(carry_ref.shape, F32)

    for k, page in enumerate(page_refs):
        rows = slice(k * PAGE_SIZE, (k + 1) * PAGE_SIZE)
        xk_ref[rows, :] = page[0:LANES, :].T
        xv_ref[rows, :] = page[LANES:KV_ROW, :].T
    out = _compress_rows([(xk_ref, xv_ref)], pe_ref, w1_ref, b1_ref, w2_ref, b2_ref, carry_ref)
    kcv_ref[pl.ds(pl.multiple_of(s * step_e, step_e), step_e), :] = out.astype(BF16)

    @pl.when(s == n_steps - 1)
    def _():
        n_sel = n_e // (SEL_BLOCK // CMP_STRIDE) + 1
        n_sel_rows = pk_ref.shape[0] // (SEL_BLOCK // CMP_STRIDE) - 2
        kc = kcv_ref[:, 0:LANES]
        vc = kcv_ref[:, LANES:KV_ROW]
        e_idx = lax.broadcasted_iota(I32, (n_e, S_COLS), 0)
        p = _masked_softmax_rows(_dot_nt(kc, q_ref[...]), e_idx >= 1)
        ocmp_ref[...] = _dot_tn(p.astype(BF16), vc)
        pkv = p
        for hh in range(1, GROUP):
            pkv = pkv + pltpu.roll(p, S_COLS - n_t * hh, 1)
        pk_ref[0:n_e, :] = pkv
        pk_ref[n_e:, :] = jnp.zeros((pk_ref.shape[0] - n_e, S_COLS), F32)
        slc = _selection_scores(pk_ref, n_sel_rows)
        j = lax.broadcasted_iota(I32, (n_sel_rows, S_COLS), 0)
        qp = PAST_LEN + (lax.broadcasted_iota(I32, (n_sel_rows, S_COLS), 1) & (n_t - 1))
        qblk = jnp.right_shift(qp, 6)
        forced = (j == 0) | (j == qblk) | (j == qblk - 1)
        in_range = j < n_sel
        score = jnp.where(forced & in_range, jnp.inf,
                          jnp.where((j * SEL_BLOCK <= qp) & in_range, slc, -jnp.inf))
        _, picks = _topk_rounds(score, n_sel_rows)
        for r, mi in enumerate(picks):
            idx_ref[r:r + 1, :] = mi.astype(I32)


def _cmp_topk_sample(page_table, cache_cmp_t, layer, q_cols, cw, n_t):
    assert n_t & (n_t - 1) == 0 and N_HEADS * n_t <= S_COLS
    b, n_pages = page_table.shape
    n_e = n_pages * PAGE_SIZE // CMP_STRIDE
    n_sel_rows = ((n_e // 4 + 1) + 7) // 8 * 8
    pk_rows = 4 * (n_sel_rows + 2)
    n_steps = n_pages // PAGES_PER_STEP
    step_rows = PAGES_PER_STEP * PAGE_SIZE

    def page_spec(k):
        return pl.BlockSpec((None, None, KV_ROW, PAGE_SIZE),
                            lambda bi, s, pt: (layer, pt[bi, s * PAGES_PER_STEP + k], 0, 0))

    per_b = lambda rows, w: pl.BlockSpec((None, rows, w), lambda bi, s, pt: (bi, 0, 0))
    grid_spec = pltpu.PrefetchScalarGridSpec(
        num_scalar_prefetch=1,
        grid=(b, n_steps),
        in_specs=[page_spec(k) for k in range(PAGES_PER_STEP)]
        + [per_b(S_COLS, LANES)] + _cmp_weight_specs(),
        out_specs=[per_b(S_COLS, LANES), per_b(N_SEL, S_COLS)],
        scratch_shapes=[pltpu.VMEM((n_e, KV_ROW), BF16),
                        pltpu.VMEM((pk_rows, S_COLS), F32),
                        pltpu.VMEM((4, 8, CMP_HIDDEN), F32),
                        pltpu.VMEM((step_rows, LANES), F32),
                        pltpu.VMEM((step_rows, LANES), F32)],
    )
    return pl.pallas_call(
        functools.partial(_cmp_topk_sample_kernel, n_t),
        grid_spec=grid_spec,
        out_shape=[jax.ShapeDtypeStruct((b, S_COLS, LANES), F32),
                   jax.ShapeDtypeStruct((b, N_SEL, S_COLS), I32)],
        compiler_params=_cparams(("parallel", "arbitrary")),
        name="cmp_topk_sample",
    )(page_table, *([cache_cmp_t] * PAGES_PER_STEP), q_cols, *_cmp_weight_args(cw))


HROWS = 8


def _selwin_sample_kernel(idx_ref, pt_ref, *refs):
    page_refs = refs[:N_SEL]
    (q_ref, new_s_ref, win_ref, new_w_ref, ocmp_ref, gt_ref, o_ref, k_ref, v_ref) = refs[N_SEL:]
    bi = pl.program_id(0)
    tq = pl.program_id(1)
    kh = pl.program_id(2)
    n_t = pl.num_programs(1)
    qpos = PAST_LEN + tq
    q = q_ref[...]
    base = ((bi * n_t + tq) * N_KV_HEADS + kh) * N_SEL
    new_block = PAST_LEN // SEL_BLOCK

    valid_parts = []
    lane = lax.broadcasted_iota(I32, (1, PAGE_SIZE), 1)
    picked_new = False
    for r in range(N_SEL):
        j = idx_ref[base + r]
        cols = slice(r * PAGE_SIZE, (r + 1) * PAGE_SIZE)
        k_ref[:, cols] = page_refs[r][0:LANES, :].astype(BF16)
        v_ref[:, cols] = page_refs[r][LANES:KV_ROW, :].astype(BF16)
        kpos = jnp.right_shift(j, 1) * PAGE_SIZE + lane
        valid_parts.append((jnp.right_shift(kpos, 6) == j) & (j != new_block))
        picked_new = jnp.logical_or(picked_new, j == new_block)
    cols = slice(N_SEL * PAGE_SIZE, (N_SEL + 1) * PAGE_SIZE)
    k_ref[:, cols] = new_s_ref[0:LANES, :].astype(BF16)
    v_ref[:, cols] = new_s_ref[LANES:KV_ROW, :].astype(BF16)
    new_pos = PAST_LEN + lane
    last_new = jnp.where(picked_new, qpos, -1)
    valid_parts.append((jnp.right_shift(new_pos, 6) == new_block) & (new_pos <= last_new))
    valid = jnp.concatenate(valid_parts, axis=1)
    s = jnp.where(valid, _dot(q, k_ref[...]), -jnp.inf)
    m = jnp.max(s, axis=-1, keepdims=True)
    p = jnp.exp(s - m)
    o_sel = _dot_nt(p.astype(BF16), v_ref[...]) / jnp.sum(p, axis=-1, keepdims=True)

    wb = win_ref.shape[1]
    s_old = _dot(q, win_ref[0:LANES, :].astype(BF16))
    d_old = qpos - (PAST_LEN - wb + lax.broadcasted_iota(I32, (1, wb), 1))
    s_old = jnp.where((d_old >= 0) & (d_old <= WINDOW), s_old, -jnp.inf)
    s_new = _dot(q, new_w_ref[0:LANES, :].astype(BF16))
    d_new = tq - lax.broadcasted_iota(I32, (1, new_w_ref.shape[1]), 1)
    s_new = jnp.where((d_new >= 0) & (d_new <= WINDOW), s_new, -jnp.inf)
    m = jnp.maximum(jnp.max(s_old, axis=-1, keepdims=True), jnp.max(s_new, axis=-1, keepdims=True))
    p_old = jnp.exp(s_old - m)
    p_new = jnp.exp(s_new - m)
    den = jnp.sum(p_old, axis=-1, keepdims=True) + jnp.sum(p_new, axis=-1, keepdims=True)
    o_win = (_dot_nt(p_old.astype(BF16), win_ref[LANES:KV_ROW, :].astype(BF16))
             + _dot_nt(p_new.astype(BF16), new_w_ref[LANES:KV_ROW, :].astype(BF16))) / den

    g = gt_ref[...]
    o = g[:, 0:1] * ocmp_ref[...] + g[:, 1:2] * o_sel + g[:, 2:3] * o_win
    o_ref[...] = o.astype(BF16)


def _selwin_sample(idx_flat, page_table, cache_sel_t, win_t, layer, q_rows, new_s_t, new_w_t, ocmp_rows, gate_rows):
    b, n_t = q_rows.shape[:2]
    wb = win_t.shape[-1]
    n_t_static = n_t
    last_page = page_table.shape[1] - 1

    def page_spec(r):
        def imap(bi, tq, kh, idx, pt):
            j = idx[((bi * n_t_static + tq) * N_KV_HEADS + kh) * N_SEL + r]
            return (layer, pt[bi, jnp.minimum(jnp.right_shift(j, 1), last_page)], 0, 0)
        return pl.BlockSpec((None, None, KV_ROW, PAGE_SIZE), imap)

    row5 = pl.BlockSpec((None, None, None, HROWS, LANES), lambda bi, tq, kh, idx, pt: (bi, tq, kh, 0, 0))
    new_rows = pl.BlockSpec((None, KV_ROW, LANES), lambda bi, tq, kh, idx, pt: (bi, 0, 0))
    grid_spec = pltpu.PrefetchScalarGridSpec(
        num_scalar_prefetch=2,
        grid=(b, n_t, N_KV_HEADS),
        in_specs=[page_spec(r) for r in range(N_SEL)]
        + [row5, new_rows,
           pl.BlockSpec((None, None, KV_ROW, wb), lambda bi, tq, kh, idx, pt: (layer, bi, 0, 0)),
           new_rows, row5, row5],
        out_specs=row5,
        scratch_shapes=[pltpu.VMEM((LANES, (N_SEL + 1) * PAGE_SIZE), BF16),
                        pltpu.VMEM((LANES, (N_SEL + 1) * PAGE_SIZE), BF16)],
    )
    return pl.pallas_call(
        _selwin_sample_kernel,
        grid_spec=grid_spec,
        out_shape=jax.ShapeDtypeStruct((b, n_t, N_KV_HEADS, HROWS, LANES), BF16),
        compiler_params=_cparams(("arbitrary", "arbitrary", "arbitrary")),
        name="selwin_sample",
    )(idx_flat, page_table, *([cache_sel_t] * N_SEL), q_rows, new_s_t, win_t, new_w_t, ocmp_rows, gate_rows)


def _outproj_ln_kernel(attn_transposed, x_ref, yc_ref, yl_ref, ya_ref, w_ref, g_ref, b_ref,
                       wg_ref, wu_ref, wo_ref, g2_ref, b2_ref, o_ref):
    y = _dot(yc_ref[...], w_ref[0:D_CONV, :])
    y = y + _dot(yl_ref[...], w_ref[D_CONV:D_CONV + D_LRU, :])
    w_attn = w_ref[D_CONV + D_LRU:, :]
    y = y + (_dot_tn(ya_ref[...], w_attn) if attn_transposed else _dot(ya_ref[...], w_attn))
    x = _layernorm(ALPHA * x_ref[...] + y, g_ref[...], b_ref[...])
    o_ref[...] = _ffn_ln_apply(x, wg_ref, wu_ref, wo_ref, g2_ref, b2_ref)


def _outproj_ffn(x, yc, yl, ya, ow, fw, tm):
    n = x.shape[0]
    tok = lambda w: pl.BlockSpec((tm, w), lambda i: (i, 0))
    attn_transposed = ya.ndim == 3
    if attn_transposed:
        tps = ya.shape[2] // tm
        ya_spec = pl.BlockSpec((None, Q_PAD, tm), lambda i: (i // tps, 0, i % tps))
    else:
        ya_spec = tok(Q_PAD)
    return pl.pallas_call(
        functools.partial(_outproj_ln_kernel, attn_transposed),
        grid=(n // tm,),
        in_specs=[tok(D_MODEL), tok(D_CONV), tok(D_LRU), ya_spec,
                  _const_spec((D_CONV + D_LRU + Q_PAD, D_MODEL)),
                  _const_spec((1, D_MODEL)), _const_spec((1, D_MODEL))] + _ffn_weight_specs(),
        out_specs=tok(D_MODEL),
        out_shape=jax.ShapeDtypeStruct((n, D_MODEL), F32),
        compiler_params=_cparams(("parallel",)),
        name="outproj_ffn",
    )(x, yc, yl, ya, ow["w"], ow["g"], ow["b"], *_ffn_weight_args(fw))


def _rope_tables(pos):
    half = ROPE_DIM // 2
    inv = ROPE_THETA ** (-jnp.arange(half, dtype=F32) / half)
    ang = pos.astype(F32)[:, None] * inv[None, :]
    cos, sin = jnp.cos(ang), jnp.sin(ang)
    n = pos.shape[0]
    rest = HEAD_DIM - ROPE_DIM
    zeros8 = jnp.zeros((n, half), F32)
    c = jnp.concatenate([cos, cos, jnp.ones((n, rest), F32)], axis=1)
    s1 = jnp.concatenate([zeros8, sin, jnp.zeros((n, rest), F32)], axis=1)
    s2 = jnp.concatenate([-sin, zeros8, jnp.zeros((n, rest), F32)], axis=1)
    rep = LANES // HEAD_DIM
    return tuple(jnp.tile(a, (1, rep)) for a in (c, s1, s2))


def _head_pad_index():
    h = np.arange(D_ATTN) // HEAD_DIM
    d = np.arange(D_ATTN) % HEAD_DIM
    return h * PAD_HEAD + (h // GROUP) * HEAD_DIM + d


def _prep_layer(l, ln_g, ln_b, ffn_w_in, ffn_w_out, w_in, conv_w, conv_b, conv_ln_g, conv_ln_b,
                lru_conv_w, lru_conv_b, lru_w_gate, lru_b_gate, lru_lambda,
                cmp_pe, cmp_w1, cmp_b1, cmp_w2, cmp_b2, w_out):
    row = lambda v: v.reshape(1, -1).astype(F32)
    ffn = []
    for f, ln_i in ((0, 0), (1, 2)):
        wi = ffn_w_in[l, f]
        wg = wi[:, :D_FF].reshape(D_MODEL, N_FF_CHUNKS, FF_CHUNK).transpose(1, 0, 2).astype(BF16)
        wu = wi[:, D_FF:].reshape(D_MODEL, N_FF_CHUNKS, FF_CHUNK).transpose(1, 0, 2).astype(BF16)
        wo = ffn_w_out[l, f].reshape(N_FF_CHUNKS, FF_CHUNK, D_MODEL).astype(BF16)
        ffn.append({"wg": wg, "wu": wu, "wo": wo, "g": row(ln_g[l, ln_i]), "b": row(ln_b[l, ln_i])})

    wl = w_in[l]
    o_q = 2 * D_CONV + 2 * D_LRU
    o_kv = o_q + D_ATTN
    o_g = o_kv + 3 * KV_ROW
    pad_idx = _head_pad_index()
    wq = jnp.zeros((D_MODEL, Q_PAD), F32).at[:, pad_idx].set(wl[:, o_q:o_kv])
    hh = np.arange(3 * N_HEADS) // 3
    gate_idx = (hh // GROUP) * LANES + (hh % GROUP) * 3 + np.arange(3 * N_HEADS) % 3
    wgt = jnp.zeros((D_MODEL, N_KV_HEADS * LANES), F32).at[:, gate_idx].set(wl[:, o_g:])
    w_all = jnp.concatenate([wl[:, :o_q], wq, wl[:, o_kv:o_g], wgt], axis=1).astype(BF16)
    w_t = jnp.concatenate([wl[:, o_q:o_kv].astype(BF16), w_all[:, C_KVS + LANES:C_KVW],
                           w_all[:, C_KVW + LANES:C_GATE], w_all[:, C_GATE:]], axis=1).T

    conv = {"w": jnp.pad(conv_w[l], ((0, CONV_HALO - CONV_WIDTH), (0, 0))),
            "b": row(conv_b[l]), "ln_g": row(conv_ln_g[l]), "ln_b": row(conv_ln_b[l])}

    def blockdiag(w):
        out = jnp.zeros((D_LRU, D_LRU), F32)
        for n in range(LRU_BLOCKS):
            out = out.at[n * LRU_BW:(n + 1) * LRU_BW, n * LRU_BW:(n + 1) * LRU_BW].set(w[n])
        return out.astype(BF16)

    lru = {"cw": jnp.pad(lru_conv_w[l], ((0, LRU_HALO - LRU_CONV_WIDTH), (0, 0))),
           "cb": row(lru_conv_b[l]),
           "wr": blockdiag(lru_w_gate[l, 0]), "wi": blockdiag(lru_w_gate[l, 1]),
           "bg": lru_b_gate[l].astype(F32), "lam": row(lru_lambda[l])}

    pe = cmp_pe[l].reshape(2, 2, CMP_STRIDE * HEAD_DIM)
    pe_rows = jnp.stack([jnp.stack([pe[g // 2, j] for g in range(4)]) for j in range(2)])
    w2e = jnp.zeros((4, CMP_HIDDEN, KV_ROW), F32)
    for g in range(4):
        w2e = w2e.at[g, :, g * HEAD_DIM:(g + 1) * HEAD_DIM].set(cmp_w2[l, g // 2])
    cmp = {"pe": pe_rows.astype(F32),
           "w1": cmp_w1[l].reshape(2, 2, CMP_STRIDE * HEAD_DIM, CMP_HIDDEN).astype(BF16),
           "b1": cmp_b1[l].astype(F32),
           "w2": w2e.astype(BF16),
           "b2": jnp.concatenate([cmp_b2[l, 0], cmp_b2[l, 0], cmp_b2[l, 1], cmp_b2[l, 1]]).reshape(1, -1)}

    wo = w_out[l]
    wo_attn = jnp.zeros((Q_PAD, D_MODEL), F32).at[pad_idx, :].set(wo[D_CONV + D_LRU:])
    out = {"w": jnp.concatenate([wo[:D_CONV + D_LRU], wo_attn], axis=0).astype(BF16),
           "g": row(ln_g[l, 1]), "b": row(ln_b[l, 1])}
    return {"ffn": ffn, "w_all": w_all, "w_t": w_t, "conv": conv, "lru": lru, "cmp": cmp, "out": out}


def _pad_front(a, rows):
    return jnp.pad(a, ((0, 0), (rows - a.shape[1], 0), (0, 0)))


def _kv6(a, lead):
    return a.reshape(lead + (2, N_KV_HEADS, HEAD_DIM))


TM_PROMPT = 512
TC_PROMPT = 512
TL_PROMPT = 256


def _layer_prompt(x, bsz, t, lw, tabs):
    n = bsz * t
    x = _ffn_ln(x, lw["ffn"][0], TM_PROMPT)
    tabs_t = tuple(a.T for a in tabs)
    (u, lx, lg, kvc, ks, kw, kvc_t, kvs_t, kvw_t, qc_t, qr_t, vs_t, vw_t, gates_t) = _inproj_t(
        x, lw["w_all"], lw["w_t"], tabs, tabs_t, bsz, t, TM_PROMPT)
    s3 = lambda a: a.reshape(bsz, t, a.shape[-1])
    u3, lx3 = s3(u), s3(lx)
    yc = _conv_group(u3, jnp.zeros((bsz, CONV_HALO, D_CONV), F32), lw["conv"], TC_PROMPT)
    yl, h_last = _lru_group(lx3, s3(lg), jnp.zeros((bsz, LRU_HALO, D_LRU), F32),
                            jnp.zeros((bsz, LRU_HALO, D_LRU), F32), lw["lru"], TL_PROMPT, TL_PROMPT - 1)
    kc, vc_t = _compress_prompt(s3(kvc), lw["cmp"])
    ocmp_t, mneg = _cmp_topk_prompt(qc_t, kc, vc_t)
    ya_t = _selwin_prompt(qr_t, mneg, s3(ks), vs_t, s3(kw), vw_t, ocmp_t, gates_t)
    x = _outproj_ffn(x, yc.reshape(n, D_CONV), yl.reshape(n, D_LRU), ya_t, lw["out"], lw["ffn"][1], TM_PROMPT)
    leaf = lambda a: a.reshape(bsz, 2, N_KV_HEADS, HEAD_DIM, a.shape[-1]).transpose(0, 4, 1, 2, 3)
    state = (u3[:, t - (CONV_WIDTH - 1):], lx3[:, t - (LRU_CONV_WIDTH - 1):], h_last[:, 0],
             leaf(kvc_t), leaf(kvs_t), leaf(kvw_t[:, :, t - min(WINDOW, t):]))
    return x, state


T_PAD = 8


def _layer_sample(x, bsz, t, lw, tabs, conv_buf, lru_buf, lru_h, cache_cmp_t, cache_sel_t, win_t, layer, page_table):
    n = bsz * t
    x = _ffn_ln(x, lw["ffn"][0], n)
    (u, lx, lg, qc, qr, kvc, kvs, kvs_b, kvw, kvw_b, gates) = _inproj(x, lw["w_all"], tabs, n, 1)
    s3 = lambda a: a.reshape(bsz, t, a.shape[-1])
    padt = lambda a: jnp.pad(s3(a), ((0, 0), (0, T_PAD - t), (0, 0)))
    yc = _conv_group(padt(u), _pad_front(conv_buf, CONV_HALO), lw["conv"], T_PAD)[:, :t]
    h0 = jnp.broadcast_to(lru_h[:, None, :], (bsz, LRU_HALO, D_LRU))
    yl, h_all = _lru_group(padt(lx), padt(lg), _pad_front(lru_buf, LRU_HALO), h0, lw["lru"], T_PAD, t - 1)
    yl = yl[:, :t]

    def head_cols(a):
        a = a.reshape(bsz, t, N_HEADS, PAD_HEAD).transpose(0, 2, 1, 3).reshape(bsz, N_HEADS * t, PAD_HEAD)
        return jnp.pad(a, ((0, 0), (0, S_COLS - N_HEADS * t), (0, 0)))

    ocmp_cols, picks = _cmp_topk_sample(page_table, cache_cmp_t, layer, head_cols(qc), lw["cmp"], t)
    pk = picks[:, :, :N_HEADS * t].reshape(bsz, N_SEL, N_KV_HEADS, GROUP, t)[:, :, :, 0, :]
    idx_flat = pk.transpose(0, 3, 2, 1).reshape(-1).astype(I32)

    def head_rows(a, dt):
        a = a.reshape(bsz, N_KV_HEADS, GROUP, t, PAD_HEAD).transpose(0, 3, 1, 2, 4)
        return jnp.pad(a, ((0, 0), (0, 0), (0, 0), (0, HROWS - GROUP), (0, 0))).astype(dt)

    q_rows = head_rows(qr.reshape(bsz, t, N_HEADS, PAD_HEAD).transpose(0, 2, 1, 3).reshape(bsz, N_HEADS * t, PAD_HEAD), BF16)
    ocmp_rows = head_rows(ocmp_cols[:, :N_HEADS * t], F32)
    g3 = gates.reshape(bsz, t, N_KV_HEADS, LANES)[..., :3 * GROUP].reshape(bsz, t, N_KV_HEADS, GROUP, 3)
    gate_rows = jnp.pad(g3, ((0, 0), (0, 0), (0, 0), (0, HROWS - GROUP), (0, LANES - 3)))
    pos_last = lambda a: s3(a).transpose(0, 2, 1)
    pad_cols = lambda a: jnp.pad(a, ((0, 0), (0, 0), (0, LANES - t)))
    kvs_t, kvw_t = pos_last(kvs), pos_last(kvw)
    ya_rows = _selwin_sample(idx_flat, page_table, cache_sel_t, win_t, layer, q_rows, pad_cols(kvs_t),
                             pad_cols(kvw_t), ocmp_rows, gate_rows)
    ya = ya_rows[:, :, :, :GROUP].reshape(bsz, t, Q_PAD)

    x = _outproj_ffn(x, yc.reshape(n, D_CONV), yl.reshape(n, D_LRU), ya.reshape(n, Q_PAD), lw["out"], lw["ffn"][1], n)
    new_conv = jnp.concatenate([conv_buf, s3(u)], axis=1)[:, t:]
    new_lru = jnp.concatenate([lru_buf, s3(lx)], axis=1)[:, t:]
    win_all = jnp.concatenate([win_t[layer], kvw_t], axis=-1)
    n_win = min(WINDOW, win_all.shape[-1])
    new_win = win_all[..., win_all.shape[-1] - n_win:]
    new_win = new_win.reshape(bsz, 2, N_KV_HEADS, HEAD_DIM, n_win).transpose(0, 4, 1, 2, 3)
    state = (new_conv, new_lru, h_all[:, 0], _kv6(kvc, (bsz, t)), _kv6(kvs, (bsz, t)), new_win)
    return x, state


def kernel(x_prompt, x_sample, state_conv, state_lru_conv, state_lru_h, cache_cmp_kv, cache_sel_kv, cache_win_kv,
           page_table, ln_g, ln_b, ffn_w_in, ffn_w_out, w_in, conv_w, conv_b, conv_ln_g, conv_ln_b,
           lru_conv_w, lru_conv_b, lru_w_gate, lru_b_gate, lru_lambda, cmp_pe, cmp_w1, cmp_b1, cmp_w2, cmp_b2, w_out):
    bp, tp, _ = x_prompt.shape
    bs, ts, _ = x_sample.shape
    depth = ln_g.shape[0]
    past = page_table.shape[1] * PAGE_SIZE
    assert past == PAST_LEN and past % SEL_BLOCK == 0
    tabs_p = _rope_tables(jnp.arange(tp))
    tabs_s = _rope_tables(jnp.tile(past + jnp.arange(ts), bs))
    xp = x_prompt.reshape(bp * tp, D_MODEL)
    xs = x_sample.reshape(bs * ts, D_MODEL)
    n_pool = cache_cmp_kv.shape[1]
    pos_last = lambda c: jnp.transpose(c, (0, 1, 3, 4, 5, 2)).reshape(c.shape[:2] + (KV_ROW, c.shape[2]))
    cache_cmp_t, cache_sel_t, win_t = pos_last(cache_cmp_kv), pos_last(cache_sel_kv), pos_last(cache_win_kv)
    st_p, st_s = [], []
    for l in range(depth):
        lw = _prep_layer(l, ln_g, ln_b, ffn_w_in, ffn_w_out, w_in, conv_w, conv_b, conv_ln_g, conv_ln_b,
                         lru_conv_w, lru_conv_b, lru_w_gate, lru_b_gate, lru_lambda,
                         cmp_pe, cmp_w1, cmp_b1, cmp_w2, cmp_b2, w_out)
        xp, sp = _layer_prompt(xp, bp, tp, lw, tabs_p)
        xs, ss = _layer_sample(
            xs, bs, ts, lw, tabs_s, state_conv[l], state_lru_conv[l], state_lru_h[l],
            cache_cmp_t, cache_sel_t, win_t, l, page_table)
        st_p.append(sp)
        st_s.append(ss)
    outs = [xp.reshape(bp, tp, D_MODEL), xs.reshape(bs, ts, D_MODEL)]
    for k in range(6):
        outs.append(jnp.stack([s[k] for s in st_p]))
        outs.append(jnp.stack([s[k] for s in st_s]))
    return tuple(outs)
```

```python
import functools

import numpy as np
import jax
import jax.numpy as jnp
from jax import lax
from jax.experimental import pallas as pl
from jax.experimental.pallas import tpu as pltpu

F32 = jnp.float32
BF16 = jnp.bfloat16
I32 = jnp.int32

D_MODEL = 1024
DEPTH = 2
PAST_LEN = 16384
PAGE_SIZE = 128
D_CONV = 256
CONV_WIDTH = 31
D_LRU = 256
LRU_BLOCKS = 4
LRU_BW = D_LRU // LRU_BLOCKS
LRU_CONV_WIDTH = 4
LRU_C = 8.0
D_ATTN = 512
N_HEADS = 8
HEAD_DIM = 64
N_KV_HEADS = 2
GROUP = N_HEADS // N_KV_HEADS
KV_ROW = 2 * N_KV_HEADS * HEAD_DIM
ROPE_DIM = 16
ROPE_THETA = 500000.0
CMP_BLOCK = 32
CMP_STRIDE = 16
CMP_HIDDEN = 256
SEL_BLOCK = 64
N_SEL = 16
WINDOW = 512
Q_BLOCK = 128
D_FF = 2816
ALPHA = (2 * DEPTH) ** 0.25
LN_EPS = 1e-5
SCALE = HEAD_DIM ** -0.5

LANES = 128
SUBLANES = 8
VMEM_LIMIT = 56 * 1024 * 1024
FF_CHUNK = 256
N_FF_CHUNKS = D_FF // FF_CHUNK
PAD_HEAD = 128
Q_PAD = N_HEADS * PAD_HEAD
KV_GROUP_LANES = GROUP * PAD_HEAD
MASK_NEG = -(2.0 ** 60)
SEL_TILE = 512
SELWIN_QUERIES = 256

C_GLU = 0
C_LRUX = 512
C_LRUG = 768
C_Q = 1024
C_KVC = C_Q + Q_PAD
C_KVS = C_KVC + KV_ROW
C_KVW = C_KVS + KV_ROW
C_GATE = C_KVW + KV_ROW
N_PROJ = C_GATE + N_KV_HEADS * LANES


def _cparams(sem):
    return pltpu.CompilerParams(dimension_semantics=sem, vmem_limit_bytes=VMEM_LIMIT)


def _const_spec(shape):
    nd = len(shape)
    return pl.BlockSpec(shape, lambda *_: (0,) * nd, pipeline_mode=pl.Buffered(1))


def _layernorm(y, g, b):
    mu = jnp.mean(y, axis=-1, keepdims=True)
    d = y - mu
    var = jnp.mean(d * d, axis=-1, keepdims=True)
    return d * lax.rsqrt(var + LN_EPS) * g + b


def _dot(a, b):
    return jnp.dot(a, b, preferred_element_type=F32)


def _dot_nt(a, b):
    return lax.dot_general(a, b, (((1,), (1,)), ((), ())), preferred_element_type=F32)


def _dot_tn(a, b):
    return lax.dot_general(a, b, (((0,), (0,)), ((), ())), preferred_element_type=F32)


def _ffn_ln_apply(x, wg_ref, wu_ref, wo_ref, g_ref, b_ref):
    xb = x.astype(BF16)
    acc = jnp.zeros(x.shape, F32)
    for c in range(N_FF_CHUNKS):
        gate = _dot(xb, wg_ref[c])
        up = _dot(xb, wu_ref[c])
        h = (gate * jax.nn.sigmoid(gate)) * up
        acc = acc + _dot(h.astype(BF16), wo_ref[c])
    y = ALPHA * x + 0.5 * acc
    return _layernorm(y, g_ref[...], b_ref[...])


def _ffn_ln_kernel(x_ref, wg_ref, wu_ref, wo_ref, g_ref, b_ref, o_ref):
    o_ref[...] = _ffn_ln_apply(x_ref[...], wg_ref, wu_ref, wo_ref, g_ref, b_ref)


def _ffn_weight_specs():
    return [
        _const_spec((N_FF_CHUNKS, D_MODEL, FF_CHUNK)),
        _const_spec((N_FF_CHUNKS, D_MODEL, FF_CHUNK)),
        _const_spec((N_FF_CHUNKS, FF_CHUNK, D_MODEL)),
        _const_spec((1, D_MODEL)),
        _const_spec((1, D_MODEL)),
    ]


def _ffn_weight_args(fw):
    return (fw["wg"], fw["wu"], fw["wo"], fw["g"], fw["b"])


def _ffn_ln(x, fw, tm):
    n = x.shape[0]
    return pl.pallas_call(
        _ffn_ln_kernel,
        grid=(n // tm,),
        in_specs=[pl.BlockSpec((tm, D_MODEL), lambda i: (i, 0))] + _ffn_weight_specs(),
        out_specs=pl.BlockSpec((tm, D_MODEL), lambda i: (i, 0)),
        out_shape=jax.ShapeDtypeStruct((n, D_MODEL), F32),
        compiler_params=_cparams(("parallel",)),
        name="ffn_ln",
    )(x, *_ffn_weight_args(fw))


def _rope(v, cos, s1, s2):
    return v * cos + pltpu.roll(v, 8, 1) * s1 + pltpu.roll(v, LANES - 8, 1) * s2


def _inproj_kernel(x_ref, w_ref, cos_ref, s1_ref, s2_ref,
                   u_ref, lx_ref, lg_ref, qc_ref, qr_ref,
                   kvc_ref, kvs_ref, kvsb_ref, kvw_ref, kvwb_ref, gt_ref):
    xb = x_ref[...].astype(BF16)

    def mm(lo, hi):
        return _dot(xb, w_ref[:, lo:hi])

    glu = mm(C_GLU, C_GLU + 2 * D_CONV)
    u_ref[...] = glu[:, :D_CONV] * jax.nn.sigmoid(glu[:, D_CONV:])
    lx_ref[...] = mm(C_LRUX, C_LRUX + D_LRU)
    lg_ref[...] = mm(C_LRUG, C_LRUG + D_LRU)
    cos = cos_ref[...]
    s1 = s1_ref[...]
    s2 = s2_ref[...]
    for h in range(N_HEADS):
        qh = mm(C_Q + h * PAD_HEAD, C_Q + (h + 1) * PAD_HEAD) * SCALE
        qc_ref[:, h * PAD_HEAD:(h + 1) * PAD_HEAD] = qh.astype(BF16)
        qr_ref[:, h * PAD_HEAD:(h + 1) * PAD_HEAD] = _rope(qh, cos, s1, s2).astype(BF16)
    kvc_ref[...] = mm(C_KVC, C_KVC + KV_ROW)
    for c0, f_ref, b_ref in ((C_KVS, kvs_ref, kvsb_ref), (C_KVW, kvw_ref, kvwb_ref)):
        kv = mm(c0, c0 + KV_ROW)
        k = _rope(kv[:, :LANES], cos, s1, s2)
        v = kv[:, LANES:]
        f_ref[:, 0:LANES] = k
        f_ref[:, LANES:KV_ROW] = v
        b_ref[:, 0:LANES] = k.astype(BF16)
        b_ref[:, LANES:KV_ROW] = v.astype(BF16)
    gt_ref[...] = jax.nn.sigmoid(mm(C_GATE, N_PROJ))


def _rope_t(v, cos, s1, s2):
    return v * cos + pltpu.roll(v, 8, 0) * s1 + pltpu.roll(v, v.shape[0] - 8, 0) * s2


R_Q = 0
R_VS = D_ATTN
R_VW = R_VS + LANES
R_GATE = R_VW + LANES
N_PROJ_T = R_GATE + N_KV_HEADS * LANES


LOG2E = 1.4426950408889634
SUM_ROWS = (HEAD_DIM, 0)


def _inproj_t_kernel(x_ref, w_ref, wt_ref, cos_ref, s1_ref, s2_ref, cos_t_ref, s1_t_ref, s2_t_ref, blk_ref,
                     u_ref, lx_ref, lg_ref, kvc_ref, ks_ref, kw_ref,
                     kvct_ref, kvst_ref, kvwt_ref, qc_ref, qr_ref, vs_ref, vw_ref, gt_ref):
    xb = x_ref[...].astype(BF16)
    tm = xb.shape[0]

    def mm(lo, hi):
        return _dot(xb, w_ref[:, lo:hi])

    glu = mm(C_GLU, C_GLU + 2 * D_CONV)
    u_ref[...] = glu[:, :D_CONV] * jax.nn.sigmoid(glu[:, D_CONV:])
    lx_ref[...] = mm(C_LRUX, C_LRUX + D_LRU)
    lg_ref[...] = mm(C_LRUG, C_LRUG + D_LRU)
    kvc = mm(C_KVC, C_KVC + KV_ROW)
    kvc_ref[...] = kvc
    kvct_ref[...] = kvc.T
    cos = cos_ref[...]
    s1 = s1_ref[...]
    s2 = s2_ref[...]
    for c0, b_ref, leaf_ref in ((C_KVS, ks_ref, kvst_ref), (C_KVW, kw_ref, kvwt_ref)):
        k = _rope(mm(c0, c0 + LANES), cos, s1, s2)
        b_ref[:, 0:LANES] = k.astype(BF16)
        leaf_ref[0:LANES, :] = k.T
    ks_ref[:, LANES:] = blk_ref[...]

    all_t = _dot_nt(wt_ref[...], xb)
    cos_t = cos_t_ref[0:HEAD_DIM, :]
    s1_t = s1_t_ref[0:HEAD_DIM, :]
    s2_t = s2_t_ref[0:HEAD_DIM, :]
    no_rows = jnp.zeros((HEAD_DIM, tm), BF16)
    for h in range(N_HEADS):
        qh = all_t[R_Q + h * HEAD_DIM:R_Q + (h + 1) * HEAD_DIM, :]
        own = h * PAD_HEAD + (h // GROUP) * HEAD_DIM
        other = h * PAD_HEAD + (1 - h // GROUP) * HEAD_DIM
        qc_ref[own:own + HEAD_DIM, :] = (qh * SCALE).astype(BF16)
        qr_ref[own:own + HEAD_DIM, :] = (_rope_t(qh, cos_t, s1_t, s2_t) * (SCALE * LOG2E)).astype(BF16)
        qc_ref[other:other + HEAD_DIM, :] = no_rows
        qr_ref[other:other + HEAD_DIM, :] = no_rows
    row = lax.broadcasted_iota(I32, (LANES, tm), 0)
    for r0, leaf_ref, o_ref in ((R_VS, kvst_ref, vs_ref), (R_VW, kvwt_ref, vw_ref)):
        vt = all_t[r0:r0 + LANES, :]
        leaf_ref[LANES:KV_ROW, :] = vt
        per_head = (jnp.where(row < HEAD_DIM, vt, jnp.where(row == SUM_ROWS[0], 1.0, 0.0)),
                    jnp.where(row >= HEAD_DIM, vt, jnp.where(row == SUM_ROWS[1], 1.0, 0.0)))
        for kv, vk in enumerate(per_head):
            vk = vk.astype(BF16)
            for c in range(tm // LANES):
                o_ref[kv, c] = vk[:, c * LANES:(c + 1) * LANES]
    gt_ref[...] = jax.nn.sigmoid(all_t[R_GATE:N_PROJ_T, :])


def _inproj_t(x, w_all, w_t, rope_tabs, rope_tabs_t, bsz, t, tm):
    n = x.shape[0]
    tps = t // tm
    tok = lambda w: pl.BlockSpec((tm, w), lambda i: (i, 0))
    tab = pl.BlockSpec((tm, LANES), lambda i: (i % tps, 0))
    tab_t = pl.BlockSpec((PAD_HEAD, tm), lambda i: (0, i % tps))
    feat_t = lambda r: pl.BlockSpec((None, r, tm), lambda i: (i // tps, 0, i % tps))
    vt_spec = pl.BlockSpec((None, N_KV_HEADS, tm // LANES, LANES, LANES), lambda i: (i // tps, 0, i % tps, 0, 0))
    n_sel = t // SEL_BLOCK
    row_outs = [(D_CONV, F32), (D_LRU, F32), (D_LRU, F32), (KV_ROW, F32), (LANES + n_sel, BF16), (LANES, BF16)]
    leaf_t = jax.ShapeDtypeStruct((bsz, KV_ROW, t), F32)
    vt_shape = jax.ShapeDtypeStruct((bsz, N_KV_HEADS, t // LANES, LANES, LANES), BF16)
    block_id = (jnp.arange(t)[:, None] // SEL_BLOCK == jnp.arange(n_sel)[None, :]).astype(BF16)
    return pl.pallas_call(
        _inproj_t_kernel,
        grid=(n // tm,),
        in_specs=[tok(D_MODEL), _const_spec((D_MODEL, N_PROJ)), _const_spec((N_PROJ_T, D_MODEL)),
                  tab, tab, tab, tab_t, tab_t, tab_t,
                  pl.BlockSpec((tm, n_sel), lambda i: (i % tps, 0))],
        out_specs=[tok(w) for w, _ in row_outs]
        + [feat_t(KV_ROW)] * 3
        + [feat_t(Q_PAD), feat_t(Q_PAD), vt_spec, vt_spec, feat_t(N_KV_HEADS * LANES)],
        out_shape=[jax.ShapeDtypeStruct((n, w), dt) for w, dt in row_outs]
        + [leaf_t] * 3
        + [jax.ShapeDtypeStruct((bsz, Q_PAD, t), BF16), jax.ShapeDtypeStruct((bsz, Q_PAD, t), BF16),
           vt_shape, vt_shape, jax.ShapeDtypeStruct((bsz, N_KV_HEADS * LANES, t), F32)],
        compiler_params=_cparams(("parallel",)),
        name="inproj_t",
    )(x, w_all, w_t, *rope_tabs, *rope_tabs_t, block_id)


def _inproj(x, w_all, rope_tabs, tm, tiles_per_seq):
    n = x.shape[0]
    cos, s1, s2 = rope_tabs
    tok = lambda w: pl.BlockSpec((tm, w), lambda i: (i, 0))
    tab = pl.BlockSpec((tm, LANES), lambda i: (i % tiles_per_seq, 0))
    outs = [
        (D_CONV, F32), (D_LRU, F32), (D_LRU, F32), (Q_PAD, BF16), (Q_PAD, BF16),
        (KV_ROW, F32), (KV_ROW, F32), (KV_ROW, BF16), (KV_ROW, F32), (KV_ROW, BF16),
        (N_KV_HEADS * LANES, F32),
    ]
    return pl.pallas_call(
        _inproj_kernel,
        grid=(n // tm,),
        in_specs=[tok(D_MODEL), _const_spec((D_MODEL, N_PROJ)), tab, tab, tab],
        out_specs=[tok(w) for w, _ in outs],
        out_shape=[jax.ShapeDtypeStruct((n, w), dt) for w, dt in outs],
        compiler_params=_cparams(("parallel",)),
        name="inproj",
    )(x, w_all, cos, s1, s2)


CONV_HALO = 32
CONV_ROWS = 64


def _conv_kernel(u_ref, buf_ref, w_ref, b_ref, g_ref, bb_ref, o_ref, ext_ref, sh_ref):
    t = pl.program_id(1)
    tc = u_ref.shape[0]

    @pl.when(t == 0)
    def _():
        ext_ref[0:CONV_HALO, :] = buf_ref[...]

    ext_ref[CONV_HALO:CONV_HALO + tc, :] = u_ref[...]
    off = CONV_HALO - (CONV_WIDTH - 1)
    span = tc + CONV_HALO - SUBLANES
    for s in range(1, SUBLANES):
        sh_ref[s - 1, 0:span, :] = ext_ref[s:s + span, :]
    rows = min(CONV_ROWS, tc)
    for r0 in range(0, tc, rows):
        acc = jnp.zeros((rows, D_CONV), F32)
        for k in range(CONV_WIDTH):
            a, s = divmod(off + k, SUBLANES)
            src = ext_ref if s == 0 else sh_ref.at[s - 1]
            acc = acc + src[r0 + SUBLANES * a:r0 + SUBLANES * a + rows, :] * w_ref[k:k + 1, :]
        y = acc + b_ref[...]
        y = _layernorm(y, g_ref[...], bb_ref[...])
        o_ref[r0:r0 + rows, :] = (y * jax.nn.sigmoid(y)).astype(BF16)
    ext_ref[0:CONV_HALO, :] = ext_ref[tc:tc + CONV_HALO, :]


def _conv_group(u, buf, cw, tc):
    b, t, _ = u.shape
    return pl.pallas_call(
        _conv_kernel,
        grid=(b, t // tc),
        in_specs=[
            pl.BlockSpec((None, tc, D_CONV), lambda i, j: (i, j, 0)),
            pl.BlockSpec((None, CONV_HALO, D_CONV), lambda i, j: (i, 0, 0)),
            _const_spec((CONV_HALO, D_CONV)),
            _const_spec((1, D_CONV)), _const_spec((1, D_CONV)), _const_spec((1, D_CONV)),
        ],
        out_specs=pl.BlockSpec((None, tc, D_CONV), lambda i, j: (i, j, 0)),
        out_shape=jax.ShapeDtypeStruct((b, t, D_CONV), BF16),
        scratch_shapes=[pltpu.VMEM((tc + CONV_HALO, D_CONV), F32),
                        pltpu.VMEM((SUBLANES - 1, tc + CONV_HALO - SUBLANES, D_CONV), F32)],
        compiler_params=_cparams(("parallel", "arbitrary")),
        name="conv_group",
    )(u, buf, cw["w"], cw["b"], cw["ln_g"], cw["ln_b"])


LRU_HALO = 8


def _lru_kernel(last_row, x_ref, gate_ref, buf_ref, h0_ref, cw_ref, cb_ref, wr_ref, wi_ref,
                bg_ref, lam_ref, y_ref, hl_ref, ext_ref, hc_ref):
    t = pl.program_id(1)
    tl = x_ref.shape[0]

    @pl.when(t == 0)
    def _():
        ext_ref[0:LRU_HALO, :] = buf_ref[...]
        hc_ref[...] = h0_ref[...]

    ext_ref[LRU_HALO:LRU_HALO + tl, :] = x_ref[...]
    off = LRU_HALO - (LRU_CONV_WIDTH - 1)
    xl = jnp.zeros((tl, D_LRU), F32)
    for k in range(LRU_CONV_WIDTH):
        xl = xl + ext_ref[off + k:off + k + tl, :] * cw_ref[k:k + 1, :]
    xl = xl + cb_ref[...]
    ext_ref[0:LRU_HALO, :] = ext_ref[tl:tl + LRU_HALO, :]

    xb = xl.astype(BF16)
    r_gate = jax.nn.sigmoid(_dot(xb, wr_ref[...]) + bg_ref[0:1, :])
    i_gate = jax.nn.sigmoid(_dot(xb, wi_ref[...]) + bg_ref[1:2, :])
    log_a = LRU_C * r_gate * jax.nn.log_sigmoid(lam_ref[...])
    a = jnp.exp(log_a)
    bv = jnp.sqrt(-jnp.tanh(log_a) * (a * a + 1.0)) * (i_gate * xl)

    row = lax.broadcasted_iota(I32, (tl, D_LRU), 0)
    s = 1
    while s < tl:
        keep = row >= s
        a_sh = jnp.where(keep, pltpu.roll(a, s, 0), 1.0)
        b_sh = jnp.where(keep, pltpu.roll(bv, s, 0), 0.0)
        bv = a * b_sh + bv
        a = a * a_sh
        s *= 2
    h = a * hc_ref[0:1, :] + bv
    hc_ref[...] = jnp.broadcast_to(h[last_row:last_row + 1, :], hc_ref.shape)
    hl_ref[...] = hc_ref[...]
    y_ref[...] = (h * jax.nn.gelu(gate_ref[...])).astype(BF16)


def _lru_group(x, gate, buf, h0, lw, tl, last_row):
    b, t, _ = x.shape
    assert last_row == tl - 1 or t == tl
    seq = pl.BlockSpec((None, tl, D_LRU), lambda i, j: (i, j, 0))
    per_b = pl.BlockSpec((None, LRU_HALO, D_LRU), lambda i, j: (i, 0, 0))
    return pl.pallas_call(
        functools.partial(_lru_kernel, last_row),
        grid=(b, t // tl),
        in_specs=[seq, seq, per_b, per_b,
                  _const_spec((LRU_HALO, D_LRU)), _const_spec((1, D_LRU)),
                  _const_spec((D_LRU, D_LRU)), _const_spec((D_LRU, D_LRU)),
                  _const_spec((2, D_LRU)), _const_spec((1, D_LRU))],
        out_specs=[seq, per_b],
        out_shape=[jax.ShapeDtypeStruct((b, t, D_LRU), BF16),
                   jax.ShapeDtypeStruct((b, LRU_HALO, D_LRU), F32)],
        scratch_shapes=[pltpu.VMEM((tl + LRU_HALO, D_LRU), F32),
                        pltpu.VMEM((LRU_HALO, D_LRU), F32)],
        compiler_params=_cparams(("parallel", "arbitrary")),
        name="lru_group",
    )(x, gate, buf, h0, lw["cw"], lw["cb"], lw["wr"], lw["wi"], lw["bg"], lw["lam"])


def _compress_rows(row_refs, pe_ref, w1_ref, b1_ref, w2_ref, b2_ref, carry_ref):
    xs = _chunk_vectors(row_refs, pe_ref)
    return _compress_mlp(lambda j, g: xs[j][g], w1_ref, b1_ref, w2_ref, b2_ref, carry_ref)


def _chunk_vectors(row_refs, pe_ref):
    lo = lax.broadcasted_iota(I32, (1, LANES), 1) < HEAD_DIM
    xs = [[] for _ in range(4)]
    for halves in row_refs:
        n = halves[0].shape[0] // CMP_STRIDE
        cols = [[] for _ in range(4)]
        for rp in range(CMP_STRIDE // 2):
            for half, ref in enumerate(halves):
                pa = ref[pl.ds(2 * rp, n, stride=CMP_STRIDE), :]
                pb = ref[pl.ds(2 * rp + 1, n, stride=CMP_STRIDE), :]
                cols[2 * half].append(jnp.where(lo, pa, pltpu.roll(pb, HEAD_DIM, 1)))
                cols[2 * half + 1].append(jnp.where(lo, pltpu.roll(pa, HEAD_DIM, 1), pb))
        for g in range(4):
            xs[g].append(jnp.concatenate(cols[g], axis=1))
    out = [[], []]
    for g in range(4):
        x = jnp.concatenate(xs[g], axis=0)
        for j in range(2):
            out[j].append((x + pe_ref[j, g:g + 1, :]).astype(BF16))
    return out


def _compress_mlp(get_x, w1_ref, b1_ref, w2_ref, b2_ref, carry_ref):
    out = None
    for g in range(4):
        sidx = g // 2
        p0 = _dot(get_x(0, g), w1_ref[sidx, 0])
        p1 = _dot(get_x(1, g), w1_ref[sidx, 1])
        n_tot = p0.shape[0]
        row = lax.broadcasted_iota(I32, p0.shape, 0)
        p0s = jnp.where(row == 0, carry_ref[g, 0:1, :], pltpu.roll(p0, 1, 0))
        carry_ref[g, 0:1, :] = p0[n_tot - 1:n_tot, :]
        h = (b1_ref[sidx:sidx + 1, :] + p0s) + p1
        part = _dot(jax.nn.gelu(h).astype(BF16), w2_ref[g])
        out = part if out is None else out + part
    return out + b2_ref[...]


def _cmp_weight_specs():
    return [
        _const_spec((2, 4, CMP_STRIDE * HEAD_DIM)),
        _const_spec((2, 2, CMP_STRIDE * HEAD_DIM, CMP_HIDDEN)),
        _const_spec((2, CMP_HIDDEN)),
        _const_spec((4, CMP_HIDDEN, KV_ROW)),
        _const_spec((1, KV_ROW)),
    ]


def _cmp_weight_args(cw):
    return (cw["pe"], cw["w1"], cw["b1"], cw["w2"], cw["b2"])


CMP_TILE_ROWS = 2048


def _compress_prompt_kernel(k_ref, v_ref, pe_ref, w1_ref, b1_ref, w2_ref, b2_ref, kc_ref, vct_ref, carry_ref):
    @pl.when(pl.program_id(1) == 0)
    def _():
        carry_ref[...] = jnp.zeros(carry_ref.shape, F32)

    out = _compress_rows([(k_ref, v_ref)], pe_ref, w1_ref, b1_ref, w2_ref, b2_ref, carry_ref)
    kc_ref[...] = out[:, 0:LANES].astype(BF16)
    vct_ref[...] = out[:, LANES:KV_ROW].T.astype(BF16)


def _compress_prompt(kvc, cw):
    b, t, _ = kvc.shape
    n_e = CMP_TILE_ROWS // CMP_STRIDE
    return pl.pallas_call(
        _compress_prompt_kernel,
        grid=(b, t // CMP_TILE_ROWS),
        in_specs=[pl.BlockSpec((None, CMP_TILE_ROWS, LANES), lambda i, j: (i, j, 0)),
                  pl.BlockSpec((None, CMP_TILE_ROWS, LANES), lambda i, j: (i, j, 1))]
        + _cmp_weight_specs(),
        out_specs=[pl.BlockSpec((None, n_e, LANES), lambda i, j: (i, j, 0)),
                   pl.BlockSpec((None, LANES, n_e), lambda i, j: (i, 0, j))],
        out_shape=[jax.ShapeDtypeStruct((b, t // CMP_STRIDE, LANES), BF16),
                   jax.ShapeDtypeStruct((b, LANES, t // CMP_STRIDE), BF16)],
        scratch_shapes=[pltpu.VMEM((4, 8, CMP_HIDDEN), F32)],
        compiler_params=_cparams(("parallel", "arbitrary")),
        name="compress_prompt",
    )(kvc, kvc, *_cmp_weight_args(cw))


def _selection_scores(pk_ref, n_sel):
    ratio = SEL_BLOCK // CMP_STRIDE
    slc = pk_ref[pl.ds(0, n_sel, stride=ratio), :]
    for o in range(1, ratio):
        slc = slc + 2.0 * pk_ref[pl.ds(o, n_sel, stride=ratio), :]
    return slc + pk_ref[pl.ds(ratio, n_sel, stride=ratio), :]


def _topk_rounds(score, n_rows, rounds=N_SEL):
    j = lax.broadcasted_iota(I32, score.shape, 0).astype(F32)
    sel = jnp.zeros(score.shape, F32)
    picks = []
    for _ in range(rounds):
        cm = jnp.max(score, axis=0, keepdims=True)
        mi = jnp.min(jnp.where(score == cm, j, float(n_rows)), axis=0, keepdims=True)
        hit = j == mi
        sel = jnp.where(hit, 1.0, sel)
        score = jnp.where(hit, -jnp.inf, score)
        picks.append(mi)
    return sel, picks


def _masked_softmax_rows(s_t, valid):
    s_t = jnp.where(valid, s_t, -jnp.inf)
    m = jnp.max(s_t, axis=0, keepdims=True)
    m = jnp.where(m > -jnp.inf, m, 0.0)
    e = jnp.exp(s_t - m)
    d = jnp.sum(e, axis=0, keepdims=True)
    return e / jnp.where(d > 0, d, 1.0)


CMP_ENTRY_CHUNK = 128
N_FORCED = 3


def _cmp_topk_prompt_kernel(qc_ref, kc_ref, vct_ref, ocmp_ref, mneg_ref, pk0_ref, pk1_ref):
    i = pl.program_id(1)
    n_e = kc_ref.shape[0]
    n_sel = n_e // (SEL_BLOCK // CMP_STRIDE)
    pk_refs = (pk0_ref, pk1_ref)

    def attend(n_use):
        kc = kc_ref[0:n_use, :]
        vct = vct_ref[:, 0:n_use]
        e_idx = lax.broadcasted_iota(I32, (n_use, Q_BLOCK), 0)
        qpos = i * Q_BLOCK + lax.broadcasted_iota(I32, (n_use, Q_BLOCK), 1)
        valid = (e_idx >= 1) & (CMP_STRIDE * e_idx + (CMP_STRIDE - 1) <= qpos)
        for kv in range(N_KV_HEADS):
            pkv = jnp.zeros((n_use, Q_BLOCK), F32)
            for h in range(kv * GROUP, (kv + 1) * GROUP):
                rows = slice(h * PAD_HEAD, (h + 1) * PAD_HEAD)
                p = _masked_softmax_rows(_dot(kc, qc_ref[rows, :]), valid)
                pkv = pkv + p
                ocmp_ref[rows, :] = _dot(vct, p.astype(BF16))
            pk_refs[kv][0:n_use, :] = pkv
            pk_refs[kv][n_use:, :] = jnp.zeros((n_e + 8 - n_use, Q_BLOCK), F32)

    n_var = n_e // CMP_ENTRY_CHUNK
    variant = jnp.minimum((8 * i + 7) // CMP_ENTRY_CHUNK, n_var - 1)
    for k in range(n_var):
        pl.when(variant == k)(functools.partial(attend, CMP_ENTRY_CHUNK * (k + 1)))

    j = lax.broadcasted_iota(I32, (n_sel, Q_BLOCK), 0)
    qp = i * Q_BLOCK + lax.broadcasted_iota(I32, (n_sel, Q_BLOCK), 1)
    qblk = jnp.right_shift(qp, 6)
    forced = (j == 0) | (j == qblk) | (j == qblk - 1)
    others = [jnp.where(forced | (j * SEL_BLOCK > qp), -jnp.inf, _selection_scores(r, n_sel)) for r in pk_refs]
    sel, _ = _topk_rounds(jnp.concatenate(others, axis=1), n_sel, N_SEL - N_FORCED)
    chosen = (sel > 0) | jnp.concatenate([forced] * N_KV_HEADS, axis=1)
    mneg = jnp.where(chosen, 0.0, MASK_NEG).astype(BF16)
    for kv in range(N_KV_HEADS):
        mneg_ref[kv] = mneg[:, kv * Q_BLOCK:(kv + 1) * Q_BLOCK]


def _cmp_topk_prompt(qc_t, kc, vc_t):
    b, _, t = qc_t.shape
    n_e = kc.shape[1]
    n_sel = t // SEL_BLOCK
    assert n_e % CMP_ENTRY_CHUNK == 0
    qblk = pl.BlockSpec((None, Q_PAD, Q_BLOCK), lambda bi, i: (bi, 0, i))
    return pl.pallas_call(
        _cmp_topk_prompt_kernel,
        grid=(b, t // Q_BLOCK),
        in_specs=[
            qblk,
            pl.BlockSpec((None, n_e, LANES), lambda bi, i: (bi, 0, 0)),
            pl.BlockSpec((None, LANES, n_e), lambda bi, i: (bi, 0, 0)),
        ],
        out_specs=[
            qblk,
            pl.BlockSpec((None, N_KV_HEADS, n_sel, Q_BLOCK), lambda bi, i: (bi, 0, 0, i)),
        ],
        out_shape=[jax.ShapeDtypeStruct((b, Q_PAD, t), F32),
                   jax.ShapeDtypeStruct((b, N_KV_HEADS, n_sel, t), BF16)],
        scratch_shapes=[pltpu.VMEM((n_e + 8, Q_BLOCK), F32), pltpu.VMEM((n_e + 8, Q_BLOCK), F32)],
        compiler_params=_cparams(("parallel", "arbitrary")),
        name="cmp_topk_prompt",
    )(qc_t, kc, vc_t)


def _selwin_prompt_kernel(qr_ref, mneg_ref, ks_ref, vs_ref, kw_ref, vw_ref, ocmp_ref, gt_ref, o_ref,
                          qa_ref, sa_ref, sb_ref, pa_ref, pb_ref):
    i = pl.program_id(1)
    n_sel = mneg_ref.shape[1]
    qb = qr_ref.shape[1]
    win_keys = WINDOW + qb
    cols = N_HEADS * qb
    kv_cols = GROUP * qb
    for h in range(N_HEADS):
        c = slice(h * qb, (h + 1) * qb)
        qa_ref[0:PAD_HEAD, c] = qr_ref[h * PAD_HEAD:(h + 1) * PAD_HEAD, :]
        qa_ref[PAD_HEAD:PAD_HEAD + n_sel, c] = mneg_ref[h // GROUP]
    qa = qa_ref[...]
    q0 = i * qb
    qpos = q0 + (lax.broadcasted_iota(I32, (1, cols), 1) & (qb - 1))

    tile = SEL_TILE
    key_col = lax.broadcasted_iota(I32, (tile, 1), 0)

    def scores(t):
        return _dot(ks_ref[pl.ds(pl.multiple_of(t * tile, tile), tile), :], qa)

    def weighted_values(v_ref, first_tile, n_tiles, p):
        outs = []
        for kv in range(N_KV_HEADS):
            vk = v_ref[kv, pl.ds(first_tile, n_tiles)]
            vt = jnp.concatenate([vk[c] for c in range(n_tiles)], axis=1)
            outs.append(_dot(vt, p[:, kv * kv_cols:(kv + 1) * kv_cols]))
        return jnp.concatenate(outs, axis=1)

    def normalise(acc):
        sums = [acc[SUM_ROWS[kv]:SUM_ROWS[kv] + 1, kv * kv_cols:(kv + 1) * kv_cols] for kv in range(N_KV_HEADS)]
        return acc / jnp.concatenate(sums, axis=1)

    start = pl.multiple_of(jnp.maximum(q0 - WINDOW, 0), LANES)
    s = _dot(kw_ref[pl.ds(start, win_keys), :], qa[0:PAD_HEAD, :])
    dpos = qpos - (start + lax.broadcasted_iota(I32, (win_keys, 1), 0))
    s = jnp.where(lax.bitcast_convert_type(dpos, jnp.uint32) <= jnp.uint32(WINDOW), s, -jnp.inf)
    p = jnp.exp2(s - jnp.max(s, axis=0, keepdims=True))
    o_win = normalise(weighted_values(vw_ref, start // LANES, win_keys // LANES, p.astype(BF16)))

    sub = tile // LANES

    def stage(t, cur, nxt, carry, masked, prefetch):
        m, acc, alpha_prev = carry
        if prefetch:
            s_refs[nxt][...] = scores(t + 1)
        acc = alpha_prev * acc + weighted_values(vs_ref, jnp.maximum(t - 1, 0) * sub, sub, p_refs[nxt][...])
        s = s_refs[cur][...]
        if masked:
            s = jnp.where(t * tile + key_col <= qpos, s, MASK_NEG)
        m_new = jnp.maximum(m, jnp.max(s, axis=0, keepdims=True))
        alpha = jnp.exp2(m - m_new)
        p_refs[cur][...] = jnp.exp2(s - m_new).astype(BF16)
        return m_new, acc, alpha

    s_refs = (sa_ref, sb_ref)
    p_refs = (pa_ref, pb_ref)
    sa_ref[...] = scores(0)
    pb_ref[...] = jnp.zeros(pb_ref.shape, BF16)
    init = (jnp.full((1, cols), MASK_NEG, F32), jnp.zeros((LANES, cols), F32), jnp.ones((1, cols), F32))

    def pair(u, carry):
        carry = stage(2 * u, 0, 1, carry, False, True)
        return stage(2 * u + 1, 1, 0, carry, False, True)

    u_diag = q0 // (2 * tile)
    carry = lax.fori_loop(0, u_diag, pair, init)
    carry = stage(2 * u_diag, 0, 1, carry, True, True)

    def finish_second(c):
        _, acc, alpha = stage(2 * u_diag + 1, 1, 0, c, True, False)
        return alpha * acc + weighted_values(vs_ref, (2 * u_diag + 1) * sub, sub, pb_ref[...])

    def finish_first(c):
        _, acc, alpha = c
        return alpha * acc + weighted_values(vs_ref, 2 * u_diag * sub, sub, pa_ref[...])

    o_sel = normalise(lax.cond(q0 - 2 * u_diag * tile >= tile, finish_second, finish_first, carry))

    for h in range(N_HEADS):
        c = slice(h * qb, (h + 1) * qb)
        rows = slice(h * PAD_HEAD, (h + 1) * PAD_HEAD)
        g0 = (h // GROUP) * LANES + 3 * (h % GROUP)
        o = (gt_ref[g0:g0 + 1, :] * ocmp_ref[rows, :] + gt_ref[g0 + 1:g0 + 2, :] * o_sel[:, c]
             + gt_ref[g0 + 2:g0 + 3, :] * o_win[:, c])
        o_ref[rows, :] = o.astype(BF16)


def _selwin_prompt(qr_t, mneg, ks, vs_t, kw, vw_t, ocmp_t, gates_t):
    b, _, t = qr_t.shape
    n_sel = mneg.shape[2]
    qb = SELWIN_QUERIES
    assert t % (2 * SEL_TILE) == 0 and SEL_TILE % qb == 0 and t >= WINDOW + qb
    cols = N_HEADS * qb
    per_q = lambda r: pl.BlockSpec((None, r, qb), lambda bi, i: (bi, 0, i))
    per_b = lambda *shape: pl.BlockSpec((None,) + shape, lambda bi, i: (bi,) + (0,) * len(shape),
                                        pipeline_mode=pl.Buffered(1))
    vals = per_b(N_KV_HEADS, t // LANES, LANES, LANES)
    return pl.pallas_call(
        _selwin_prompt_kernel,
        grid=(b, t // qb),
        in_specs=[
            per_q(Q_PAD),
            pl.BlockSpec((None, N_KV_HEADS, n_sel, qb), lambda bi, i: (bi, 0, 0, i)),
            per_b(t, LANES + n_sel), vals, per_b(t, LANES), vals,
            per_q(Q_PAD), per_q(N_KV_HEADS * LANES),
        ],
        out_specs=per_q(Q_PAD),
        out_shape=jax.ShapeDtypeStruct((b, Q_PAD, t), BF16),
        scratch_shapes=[pltpu.VMEM((PAD_HEAD + n_sel, cols), BF16),
                        pltpu.VMEM((SEL_TILE, cols), F32),
                        pltpu.VMEM((SEL_TILE, cols), F32),
                        pltpu.VMEM((SEL_TILE, cols), BF16),
                        pltpu.VMEM((SEL_TILE, cols), BF16)],
        compiler_params=_cparams(("parallel", "arbitrary")),
        name="selwin_prompt",
    )(qr_t, mneg, ks, vs_t, kw, vw_t, ocmp_t, gates_t)


PAGES_PER_STEP = 16
S_COLS = LANES


def _cmp_topk_sample_kernel(n_t, pt_ref, *refs):
    page_refs = refs[:PAGES_PER_STEP]
    (q_ref, pe_ref, w1_ref, b1_ref, w2_ref, b2_ref,
     ocmp_ref, idx_ref, kcv_ref, pk_ref, carry_ref, xk_ref, xv_ref) = refs[PAGES_PER_STEP:]
    s = pl.program_id(1)
    n_steps = pl.num_programs(1)
    n_e = kcv_ref.shape[0]
    step_e = PAGES_PER_STEP * PAGE_SIZE // CMP_STRIDE

    @pl.when(s == 0)
    def _():
        carry_ref[...] = jnp.zeros(carry_ref.shape, F32)

    for k, page in enumerate(page_refs):
        rows = slice(k * PAGE_SIZE, (k + 1) * PAGE_SIZE)
        xk_ref[rows, :] = page[0:LANES, :].T
        xv_ref[rows, :] = page[LANES:KV_ROW, :].T
    out = _compress_rows([(xk_ref, xv_ref)], pe_ref, w1_ref, b1_ref, w2_ref, b2_ref, carry_ref)
    kcv_ref[pl.ds(pl.multiple_of(s * step_e, step_e), step_e), :] = out.astype(BF16)

    @pl.when(s == n_steps - 1)
    def _():
        n_sel = n_e // (SEL_BLOCK // CMP_STRIDE) + 1
        n_sel_rows = pk_ref.shape[0] // (SEL_BLOCK // CMP_STRIDE) - 2
        kc = kcv_ref[:, 0:LANES]
        vc = kcv_ref[:, LANES:KV_ROW]
        e_idx = lax.broadcasted_iota(I32, (n_e, S_COLS), 0)
        p = _masked_softmax_rows(_dot_nt(kc, q_ref[...]), e_idx >= 1)
        ocmp_ref[...] = _dot_tn(p.astype(BF16), vc)
        pkv = p
        for hh in range(1, GROUP):
            pkv = pkv + pltpu.roll(p, S_COLS - n_t * hh, 1)
        pk_ref[0:n_e, :] = pkv
        pk_ref[n_e:, :] = jnp.zeros((pk_ref.shape[0] - n_e, S_COLS), F32)
        slc = _selection_scores(pk_ref, n_sel_rows)
        j = lax.broadcasted_iota(I32, (n_sel_rows, S_COLS), 0)
        qp = PAST_LEN + (lax.broadcasted_iota(I32, (n_sel_rows, S_COLS), 1) & (n_t - 1))
        qblk = jnp.right_shift(qp, 6)
        forced = (j == 0) | (j == qblk) | (j == qblk - 1)
        in_range = j < n_sel
        score = jnp.where(forced & in_range, jnp.inf,
                          jnp.where((j * SEL_BLOCK <= qp) & in_range, slc, -jnp.inf))
        _, picks = _topk_rounds(score, n_sel_rows)
        for r, mi in enumerate(picks):
            idx_ref[r:r + 1, :] = mi.astype(I32)


def _cmp_topk_sample(page_table, cache_cmp_t, layer, q_cols, cw, n_t):
    assert n_t & (n_t - 1) == 0 and N_HEADS * n_t <= S_COLS
    b, n_pages = page_table.shape
    n_e = n_pages * PAGE_SIZE // CMP_STRIDE
    n_sel_rows = ((n_e // 4 + 1) + 7) // 8 * 8
    pk_rows = 4 * (n_sel_rows + 2)
    n_steps = n_pages // PAGES_PER_STEP
    step_rows = PAGES_PER_STEP * PAGE_SIZE

    def page_spec(k):
        return pl.BlockSpec((None, None, KV_ROW, PAGE_SIZE),
                            lambda bi, s, pt: (layer, pt[bi, s * PAGES_PER_STEP + k], 0, 0))

    per_b = lambda rows, w: pl.BlockSpec((None, rows, w), lambda bi, s, pt: (bi, 0, 0))
    grid_spec = pltpu.PrefetchScalarGridSpec(
        num_scalar_prefetch=1,
        grid=(b, n_steps),
        in_specs=[page_spec(k) for k in range(PAGES_PER_STEP)]
        + [per_b(S_COLS, LANES)] + _cmp_weight_specs(),
        out_specs=[per_b(S_COLS, LANES), per_b(N_SEL, S_COLS)],
        scratch_shapes=[pltpu.VMEM((n_e, KV_ROW), BF16),
                        pltpu.VMEM((pk_rows, S_COLS), F32),
                        pltpu.VMEM((4, 8, CMP_HIDDEN), F32),
                        pltpu.VMEM((step_rows, LANES), F32),
                        pltpu.VMEM((step_rows, LANES), F32)],
    )
    return pl.pallas_call(
        functools.partial(_cmp_topk_sample_kernel, n_t),
        grid_spec=grid_spec,
        out_shape=[jax.ShapeDtypeStruct((b, S_COLS, LANES), F32),
                   jax.ShapeDtypeStruct((b, N_SEL, S_COLS), I32)],
        compiler_params=_cparams(("parallel", "arbitrary")),
        name="cmp_topk_sample",
    )(page_table, *([cache_cmp_t] * PAGES_PER_STEP), q_cols, *_cmp_weight_args(cw))


HROWS = 8


def _selwin_sample_kernel(idx_ref, pt_ref, *refs):
    page_refs = refs[:N_SEL]
    (q_ref, new_s_ref, win_ref, new_w_ref, ocmp_ref, gt_ref, o_ref, k_ref, v_ref) = refs[N_SEL:]
    bi = pl.program_id(0)
    tq = pl.program_id(1)
    kh = pl.program_id(2)
    n_t = pl.num_programs(1)
    qpos = PAST_LEN + tq
    q = q_ref[...]
    base = ((bi * n_t + tq) * N_KV_HEADS + kh) * N_SEL
    new_block = PAST_LEN // SEL_BLOCK

    valid_parts = []
    lane = lax.broadcasted_iota(I32, (1, PAGE_SIZE), 1)
    picked_new = False
    for r in range(N_SEL):
        j = idx_ref[base + r]
        cols = slice(r * PAGE_SIZE, (r + 1) * PAGE_SIZE)
        k_ref[:, cols] = page_refs[r][0:LANES, :].astype(BF16)
        v_ref[:, cols] = page_refs[r][LANES:KV_ROW, :].astype(BF16)
        kpos = jnp.right_shift(j, 1) * PAGE_SIZE + lane
        valid_parts.append((jnp.right_shift(kpos, 6) == j) & (j != new_block))
        picked_new = jnp.logical_or(picked_new, j == new_block)
    cols = slice(N_SEL * PAGE_SIZE, (N_SEL + 1) * PAGE_SIZE)
    k_ref[:, cols] = new_s_ref[0:LANES, :].astype(BF16)
    v_ref[:, cols] = new_s_ref[LANES:KV_ROW, :].astype(BF16)
    new_pos = PAST_LEN + lane
    last_new = jnp.where(picked_new, qpos, -1)
    valid_parts.append((jnp.right_shift(new_pos, 6) == new_block) & (new_pos <= last_new))
    valid = jnp.concatenate(valid_parts, axis=1)
    s = jnp.where(valid, _dot(q, k_ref[...]), -jnp.inf)
    m = jnp.max(s, axis=-1, keepdims=True)
    p = jnp.exp(s - m)
    o_sel = _dot_nt(p.astype(BF16), v_ref[...]) / jnp.sum(p, axis=-1, keepdims=True)

    wb = win_ref.shape[1]
    s_old = _dot(q, win_ref[0:LANES, :].astype(BF16))
    d_old = qpos - (PAST_LEN - wb + lax.broadcasted_iota(I32, (1, wb), 1))
    s_old = jnp.where((d_old >= 0) & (d_old <= WINDOW), s_old, -jnp.inf)
    s_new = _dot(q, new_w_ref[0:LANES, :].astype(BF16))
    d_new = tq - lax.broadcasted_iota(I32, (1, new_w_ref.shape[1]), 1)
    s_new = jnp.where((d_new >= 0) & (d_new <= WINDOW), s_new, -jnp.inf)
    m = jnp.maximum(jnp.max(s_old, axis=-1, keepdims=True), jnp.max(s_new, axis=-1, keepdims=True))
    p_old = jnp.exp(s_old - m)
    p_new = jnp.exp(s_new - m)
    den = jnp.sum(p_old, axis=-1, keepdims=True) + jnp.sum(p_new, axis=-1, keepdims=True)
    o_win = (_dot_nt(p_old.astype(BF16), win_ref[LANES:KV_ROW, :].astype(BF16))
             + _dot_nt(p_new.astype(BF16), new_w_ref[LANES:KV_ROW, :].astype(BF16))) / den

    g = gt_ref[...]
    o = g[:, 0:1] * ocmp_ref[...] + g[:, 1:2] * o_sel + g[:, 2:3] * o_win
    o_ref[...] = o.astype(BF16)


def _selwin_sample(idx_flat, page_table, cache_sel_t, win_t, layer, q_rows, new_s_t, new_w_t, ocmp_rows, gate_rows):
    b, n_t = q_rows.shape[:2]
    wb = win_t.shape[-1]
    n_t_static = n_t
    last_page = page_table.shape[1] - 1

    def page_spec(r):
        def imap(bi, tq, kh, idx, pt):
            j = idx[((bi * n_t_static + tq) * N_KV_HEADS + kh) * N_SEL + r]
            return (layer, pt[bi, jnp.minimum(jnp.right_shift(j, 1), last_page)], 0, 0)
        return pl.BlockSpec((None, None, KV_ROW, PAGE_SIZE), imap)

    row5 = pl.BlockSpec((None, None, None, HROWS, LANES), lambda bi, tq, kh, idx, pt: (bi, tq, kh, 0, 0))
    new_rows = pl.BlockSpec((None, KV_ROW, LANES), lambda bi, tq, kh, idx, pt: (bi, 0, 0))
    grid_spec = pltpu.PrefetchScalarGridSpec(
        num_scalar_prefetch=2,
        grid=(b, n_t, N_KV_HEADS),
        in_specs=[page_spec(r) for r in range(N_SEL)]
        + [row5, new_rows,
           pl.BlockSpec((None, None, KV_ROW, wb), lambda bi, tq, kh, idx, pt: (layer, bi, 0, 0)),
           new_rows, row5, row5],
        out_specs=row5,
        scratch_shapes=[pltpu.VMEM((LANES, (N_SEL + 1) * PAGE_SIZE), BF16),
                        pltpu.VMEM((LANES, (N_SEL + 1) * PAGE_SIZE), BF16)],
    )
    return pl.pallas_call(
        _selwin_sample_kernel,
        grid_spec=grid_spec,
        out_shape=jax.ShapeDtypeStruct((b, n_t, N_KV_HEADS, HROWS, LANES), BF16),
        compiler_params=_cparams(("arbitrary", "arbitrary", "arbitrary")),
        name="selwin_sample",
    )(idx_flat, page_table, *([cache_sel_t] * N_SEL), q_rows, new_s_t, win_t, new_w_t, ocmp_rows, gate_rows)


def _outproj_ln_kernel(attn_transposed, x_ref, yc_ref, yl_ref, ya_ref, w_ref, g_ref, b_ref,
                       wg_ref, wu_ref, wo_ref, g2_ref, b2_ref, o_ref):
    y = _dot(yc_ref[...], w_ref[0:D_CONV, :])
    y = y + _dot(yl_ref[...], w_ref[D_CONV:D_CONV + D_LRU, :])
    w_attn = w_ref[D_CONV + D_LRU:, :]
    y = y + (_dot_tn(ya_ref[...], w_attn) if attn_transposed else _dot(ya_ref[...], w_attn))
    x = _layernorm(ALPHA * x_ref[...] + y, g_ref[...], b_ref[...])
    o_ref[...] = _ffn_ln_apply(x, wg_ref, wu_ref, wo_ref, g2_ref, b2_ref)


def _outproj_ffn(x, yc, yl, ya, ow, fw, tm):
    n = x.shape[0]
    tok = lambda w: pl.BlockSpec((tm, w), lambda i: (i, 0))
    attn_transposed = ya.ndim == 3
    if attn_transposed:
        tps = ya.shape[2] // tm
        ya_spec = pl.BlockSpec((None, Q_PAD, tm), lambda i: (i // tps, 0, i % tps))
    else:
        ya_spec = tok(Q_PAD)
    return pl.pallas_call(
        functools.partial(_outproj_ln_kernel, attn_transposed),
        grid=(n // tm,),
        in_specs=[tok(D_MODEL), tok(D_CONV), tok(D_LRU), ya_spec,
                  _const_spec((D_CONV + D_LRU + Q_PAD, D_MODEL)),
                  _const_spec((1, D_MODEL)), _const_spec((1, D_MODEL))] + _ffn_weight_specs(),
        out_specs=tok(D_MODEL),
        out_shape=jax.ShapeDtypeStruct((n, D_MODEL), F32),
        compiler_params=_cparams(("parallel",)),
        name="outproj_ffn",
    )(x, yc, yl, ya, ow["w"], ow["g"], ow["b"], *_ffn_weight_args(fw))


def _rope_tables(pos):
    half = ROPE_DIM // 2
    inv = ROPE_THETA ** (-jnp.arange(half, dtype=F32) / half)
    ang = pos.astype(F32)[:, None] * inv[None, :]
    cos, sin = jnp.cos(ang), jnp.sin(ang)
    n = pos.shape[0]
    rest = HEAD_DIM - ROPE_DIM
    zeros8 = jnp.zeros((n, half), F32)
    c = jnp.concatenate([cos, cos, jnp.ones((n, rest), F32)], axis=1)
    s1 = jnp.concatenate([zeros8, sin, jnp.zeros((n, rest), F32)], axis=1)
    s2 = jnp.concatenate([-sin, zeros8, jnp.zeros((n, rest), F32)], axis=1)
    rep = LANES // HEAD_DIM
    return tuple(jnp.tile(a, (1, rep)) for a in (c, s1, s2))


def _head_pad_index():
    h = np.arange(D_ATTN) // HEAD_DIM
    d = np.arange(D_ATTN) % HEAD_DIM
    return h * PAD_HEAD + (h // GROUP) * HEAD_DIM + d


def _prep_layer(l, ln_g, ln_b, ffn_w_in, ffn_w_out, w_in, conv_w, conv_b, conv_ln_g, conv_ln_b,
                lru_conv_w, lru_conv_b, lru_w_gate, lru_b_gate, lru_lambda,
                cmp_pe, cmp_w1, cmp_b1, cmp_w2, cmp_b2, w_out):
    row = lambda v: v.reshape(1, -1).astype(F32)
    ffn = []
    for f, ln_i in ((0, 0), (1, 2)):
        wi = ffn_w_in[l, f]
        wg = wi[:, :D_FF].reshape(D_MODEL, N_FF_CHUNKS, FF_CHUNK).transpose(1, 0, 2).astype(BF16)
        wu = wi[:, D_FF:].reshape(D_MODEL, N_FF_CHUNKS, FF_CHUNK).transpose(1, 0, 2).astype(BF16)
        wo = ffn_w_out[l, f].reshape(N_FF_CHUNKS, FF_CHUNK, D_MODEL).astype(BF16)
        ffn.append({"wg": wg, "wu": wu, "wo": wo, "g": row(ln_g[l, ln_i]), "b": row(ln_b[l, ln_i])})

    wl = w_in[l]
    o_q = 2 * D_CONV + 2 * D_LRU
    o_kv = o_q + D_ATTN
    o_g = o_kv + 3 * KV_ROW
    pad_idx = _head_pad_index()
    wq = jnp.zeros((D_MODEL, Q_PAD), F32).at[:, pad_idx].set(wl[:, o_q:o_kv])
    hh = np.arange(3 * N_HEADS) // 3
    gate_idx = (hh // GROUP) * LANES + (hh % GROUP) * 3 + np.arange(3 * N_HEADS) % 3
    wgt = jnp.zeros((D_MODEL, N_KV_HEADS * LANES), F32).at[:, gate_idx].set(wl[:, o_g:])
    w_all = jnp.concatenate([wl[:, :o_q], wq, wl[:, o_kv:o_g], wgt], axis=1).astype(BF16)
    w_t = jnp.concatenate([wl[:, o_q:o_kv].astype(BF16), w_all[:, C_KVS + LANES:C_KVW],
                           w_all[:, C_KVW + LANES:C_GATE], w_all[:, C_GATE:]], axis=1).T

    conv = {"w": jnp.pad(conv_w[l], ((0, CONV_HALO - CONV_WIDTH), (0, 0))),
            "b": row(conv_b[l]), "ln_g": row(conv_ln_g[l]), "ln_b": row(conv_ln_b[l])}

    def blockdiag(w):
        out = jnp.zeros((D_LRU, D_LRU), F32)
        for n in range(LRU_BLOCKS):
            out = out.at[n * LRU_BW:(n + 1) * LRU_BW, n * LRU_BW:(n + 1) * LRU_BW].set(w[n])
        return out.astype(BF16)

    lru = {"cw": jnp.pad(lru_conv_w[l], ((0, LRU_HALO - LRU_CONV_WIDTH), (0, 0))),
           "cb": row(lru_conv_b[l]),
           "wr": blockdiag(lru_w_gate[l, 0]), "wi": blockdiag(lru_w_gate[l, 1]),
           "bg": lru_b_gate[l].astype(F32), "lam": row(lru_lambda[l])}

    pe = cmp_pe[l].reshape(2, 2, CMP_STRIDE * HEAD_DIM)
    pe_rows = jnp.stack([jnp.stack([pe[g // 2, j] for g in range(4)]) for j in range(2)])
    w2e = jnp.zeros((4, CMP_HIDDEN, KV_ROW), F32)
    for g in range(4):
        w2e = w2e.at[g, :, g * HEAD_DIM:(g + 1) * HEAD_DIM].set(cmp_w2[l, g // 2])
    cmp = {"pe": pe_rows.astype(F32),
           "w1": cmp_w1[l].reshape(2, 2, CMP_STRIDE * HEAD_DIM, CMP_HIDDEN).astype(BF16),
           "b1": cmp_b1[l].astype(F32),
           "w2": w2e.astype(BF16),
           "b2": jnp.concatenate([cmp_b2[l, 0], cmp_b2[l, 0], cmp_b2[l, 1], cmp_b2[l, 1]]).reshape(1, -1)}

    wo = w_out[l]
    wo_attn = jnp.zeros((Q_PAD, D_MODEL), F32).at[pad_idx, :].set(wo[D_CONV + D_LRU:])
    out = {"w": jnp.concatenate([wo[:D_CONV + D_LRU], wo_attn], axis=0).astype(BF16),
           "g": row(ln_g[l, 1]), "b": row(ln_b[l, 1])}
    return {"ffn": ffn, "w_all": w_all, "w_t": w_t, "conv": conv, "lru": lru, "cmp": cmp, "out": out}


def _pad_front(a, rows):
    return jnp.pad(a, ((0, 0), (rows - a.shape[1], 0), (0, 0)))


def _kv6(a, lead):
    return a.reshape(lead + (2, N_KV_HEADS, HEAD_DIM))


TM_PROMPT = 512
TC_PROMPT = 512
TL_PROMPT = 256


def _layer_prompt(x, bsz, t, lw, tabs):
    n = bsz * t
    x = _ffn_ln(x, lw["ffn"][0], TM_PROMPT)
    tabs_t = tuple(a.T for a in tabs)
    (u, lx, lg, kvc, ks, kw, kvc_t, kvs_t, kvw_t, qc_t, qr_t, vs_t, vw_t, gates_t) = _inproj_t(
        x, lw["w_all"], lw["w_t"], tabs, tabs_t, bsz, t, TM_PROMPT)
    s3 = lambda a: a.reshape(bsz, t, a.shape[-1])
    u3, lx3 = s3(u), s3(lx)
    yc = _conv_group(u3, jnp.zeros((bsz, CONV_HALO, D_CONV), F32), lw["conv"], TC_PROMPT)
    yl, h_last = _lru_group(lx3, s3(lg), jnp.zeros((bsz, LRU_HALO, D_LRU), F32),
                            jnp.zeros((bsz, LRU_HALO, D_LRU), F32), lw["lru"], TL_PROMPT, TL_PROMPT - 1)
    kc, vc_t = _compress_prompt(s3(kvc), lw["cmp"])
    ocmp_t, mneg = _cmp_topk_prompt(qc_t, kc, vc_t)
    ya_t = _selwin_prompt(qr_t, mneg, s3(ks), vs_t, s3(kw), vw_t, ocmp_t, gates_t)
    x = _outproj_ffn(x, yc.reshape(n, D_CONV), yl.reshape(n, D_LRU), ya_t, lw["out"], lw["ffn"][1], TM_PROMPT)
    leaf = lambda a: a.reshape(bsz, 2, N_KV_HEADS, HEAD_DIM, a.shape[-1]).transpose(0, 4, 1, 2, 3)
    state = (u3[:, t - (CONV_WIDTH - 1):], lx3[:, t - (LRU_CONV_WIDTH - 1):], h_last[:, 0],
             leaf(kvc_t), leaf(kvs_t), leaf(kvw_t[:, :, t - min(WINDOW, t):]))
    return x, state


T_PAD = 8


def _layer_sample(x, bsz, t, lw, tabs, conv_buf, lru_buf, lru_h, cache_cmp_t, cache_sel_t, win_t, layer, page_table):
    n = bsz * t
    x = _ffn_ln(x, lw["ffn"][0], n)
    (u, lx, lg, qc, qr, kvc, kvs, kvs_b, kvw, kvw_b, gates) = _inproj(x, lw["w_all"], tabs, n, 1)
    s3 = lambda a: a.reshape(bsz, t, a.shape[-1])
    padt = lambda a: jnp.pad(s3(a), ((0, 0), (0, T_PAD - t), (0, 0)))
    yc = _conv_group(padt(u), _pad_front(conv_buf, CONV_HALO), lw["conv"], T_PAD)[:, :t]
    h0 = jnp.broadcast_to(lru_h[:, None, :], (bsz, LRU_HALO, D_LRU))
    yl, h_all = _lru_group(padt(lx), padt(lg), _pad_front(lru_buf, LRU_HALO), h0, lw["lru"], T_PAD, t - 1)
    yl = yl[:, :t]

    def head_cols(a):
        a = a.reshape(bsz, t, N_HEADS, PAD_HEAD).transpose(0, 2, 1, 3).reshape(bsz, N_HEADS * t, PAD_HEAD)
        return jnp.pad(a, ((0, 0), (0, S_COLS - N_HEADS * t), (0, 0)))

    ocmp_cols, picks = _cmp_topk_sample(page_table, cache_cmp_t, layer, head_cols(qc), lw["cmp"], t)
    pk = picks[:, :, :N_HEADS * t].reshape(bsz, N_SEL, N_KV_HEADS, GROUP, t)[:, :, :, 0, :]
    idx_flat = pk.transpose(0, 3, 2, 1).reshape(-1).astype(I32)

    def head_rows(a, dt):
        a = a.reshape(bsz, N_KV_HEADS, GROUP, t, PAD_HEAD).transpose(0, 3, 1, 2, 4)
        return jnp.pad(a, ((0, 0), (0, 0), (0, 0), (0, HROWS - GROUP), (0, 0))).astype(dt)

    q_rows = head_rows(qr.reshape(bsz, t, N_HEADS, PAD_HEAD).transpose(0, 2, 1, 3).reshape(bsz, N_HEADS * t, PAD_HEAD), BF16)
    ocmp_rows = head_rows(ocmp_cols[:, :N_HEADS * t], F32)
    g3 = gates.reshape(bsz, t, N_KV_HEADS, LANES)[..., :3 * GROUP].reshape(bsz, t, N_KV_HEADS, GROUP, 3)
    gate_rows = jnp.pad(g3, ((0, 0), (0, 0), (0, 0), (0, HROWS - GROUP), (0, LANES - 3)))
    pos_last = lambda a: s3(a).transpose(0, 2, 1)
    pad_cols = lambda a: jnp.pad(a, ((0, 0), (0, 0), (0, LANES - t)))
    kvs_t, kvw_t = pos_last(kvs), pos_last(kvw)
    ya_rows = _selwin_sample(idx_flat, page_table, cache_sel_t, win_t, layer, q_rows, pad_cols(kvs_t),
                             pad_cols(kvw_t), ocmp_rows, gate_rows)
    ya = ya_rows[:, :, :, :GROUP].reshape(bsz, t, Q_PAD)

    x = _outproj_ffn(x, yc.reshape(n, D_CONV), yl.reshape(n, D_LRU), ya.reshape(n, Q_PAD), lw["out"], lw["ffn"][1], n)
    new_conv = jnp.concatenate([conv_buf, s3(u)], axis=1)[:, t:]
    new_lru = jnp.concatenate([lru_buf, s3(lx)], axis=1)[:, t:]
    win_all = jnp.concatenate([win_t[layer], kvw_t], axis=-1)
    n_win = min(WINDOW, win_all.shape[-1])
    new_win = win_all[..., win_all.shape[-1] - n_win:]
    new_win = new_win.reshape(bsz, 2, N_KV_HEADS, HEAD_DIM, n_win).transpose(0, 4, 1, 2, 3)
    state = (new_conv, new_lru, h_all[:, 0], _kv6(kvc, (bsz, t)), _kv6(kvs, (bsz, t)), new_win)
    return x, state


def kernel(x_prompt, x_sample, state_conv, state_lru_conv, state_lru_h, cache_cmp_kv, cache_sel_kv, cache_win_kv,
           page_table, ln_g, ln_b, ffn_w_in, ffn_w_out, w_in, conv_w, conv_b, conv_ln_g, conv_ln_b,
           lru_conv_w, lru_conv_b, lru_w_gate, lru_b_gate, lru_lambda, cmp_pe, cmp_w1, cmp_b1, cmp_w2, cmp_b2, w_out):
    bp, tp, _ = x_prompt.shape
    bs, ts, _ = x_sample.shape
    depth = ln_g.shape[0]
    past = page_table.shape[1] * PAGE_SIZE
    assert past == PAST_LEN and past % SEL_BLOCK == 0
    tabs_p = _rope_tables(jnp.arange(tp))
    tabs_s = _rope_tables(jnp.tile(past + jnp.arange(ts), bs))
    xp = x_prompt.reshape(bp * tp, D_MODEL)
    xs = x_sample.reshape(bs * ts, D_MODEL)
    n_pool = cache_cmp_kv.shape[1]
    pos_last = lambda c: jnp.transpose(c, (0, 1, 3, 4, 5, 2)).reshape(c.shape[:2] + (KV_ROW, c.shape[2]))
    cache_cmp_t, cache_sel_t, win_t = pos_last(cache_cmp_kv), pos_last(cache_sel_kv), pos_last(cache_win_kv)
    st_p, st_s = [], []
    for l in range(depth):
        lw = _prep_layer(l, ln_g, ln_b, ffn_w_in, ffn_w_out, w_in, conv_w, conv_b, conv_ln_g, conv_ln_b,
                         lru_conv_w, lru_conv_b, lru_w_gate, lru_b_gate, lru_lambda,
                         cmp_pe, cmp_w1, cmp_b1, cmp_w2, cmp_b2, w_out)
        xp, sp = _layer_prompt(xp, bp, tp, lw, tabs_p)
        xs, ss = _layer_sample(
            xs, bs, ts, lw, tabs_s, state_conv[l], state_lru_conv[l], state_lru_h[l],
            cache_cmp_t, cache_sel_t, win_t, l, page_table)
        st_p.append(sp)
        st_s.append(ss)
    outs = [xp.reshape(bp, tp, D_MODEL), xs.reshape(bs, ts, D_MODEL)]
    for k in range(6):
        outs.append(jnp.stack([s[k] for s in st_p]))
        outs.append(jnp.stack([s[k] for s in st_s]))
    return tuple(outs)
```

```python
import functools

import numpy as np
import jax
import jax.numpy as jnp
from jax import lax
from jax.experimental import pallas as pl
from jax.experimental.pallas import tpu as pltpu

F32 = jnp.float32
BF16 = jnp.bfloat16
I32 = jnp.int32

D_MODEL = 1024
DEPTH = 2
PAST_LEN = 16384
PAGE_SIZE = 128
D_CONV = 256
CONV_WIDTH = 31
D_LRU = 256
LRU_BLOCKS = 4
LRU_BW = D_LRU // LRU_BLOCKS
LRU_CONV_WIDTH = 4
LRU_C = 8.0
D_ATTN = 512
N_HEADS = 8
HEAD_DIM = 64
N_KV_HEADS = 2
GROUP = N_HEADS // N_KV_HEADS
KV_ROW = 2 * N_KV_HEADS * HEAD_DIM
ROPE_DIM = 16
ROPE_THETA = 500000.0
CMP_BLOCK = 32
CMP_STRIDE = 16
CMP_HIDDEN = 256
SEL_BLOCK = 64
N_SEL = 16
WINDOW = 512
Q_BLOCK = 128
D_FF = 2816
ALPHA = (2 * DEPTH) ** 0.25
LN_EPS = 1e-5
SCALE = HEAD_DIM ** -0.5

LANES = 128
SUBLANES = 8
VMEM_LIMIT = 56 * 1024 * 1024
FF_CHUNK = 256
N_FF_CHUNKS = D_FF // FF_CHUNK
PAD_HEAD = 128
Q_PAD = N_HEADS * PAD_HEAD
KV_GROUP_LANES = GROUP * PAD_HEAD
MASK_NEG = -(2.0 ** 60)
SEL_TILE = 512
SELWIN_QUERIES = 256

C_GLU = 0
C_LRUX = 512
C_LRUG = 768
C_Q = 1024
C_KVC = C_Q + Q_PAD
C_KVS = C_KVC + KV_ROW
C_KVW = C_KVS + KV_ROW
C_GATE = C_KVW + KV_ROW
N_PROJ = C_GATE + N_KV_HEADS * LANES


def _cparams(sem):
    return pltpu.CompilerParams(dimension_semantics=sem, vmem_limit_bytes=VMEM_LIMIT)


def _const_spec(shape):
    nd = len(shape)
    return pl.BlockSpec(shape, lambda *_: (0,) * nd, pipeline_mode=pl.Buffered(1))


def _layernorm(y, g, b):
    mu = jnp.mean(y, axis=-1, keepdims=True)
    d = y - mu
    var = jnp.mean(d * d, axis=-1, keepdims=True)
    return d * lax.rsqrt(var + LN_EPS) * g + b


def _dot(a, b):
    return jnp.dot(a, b, preferred_element_type=F32)


def _dot_nt(a, b):
    return lax.dot_general(a, b, (((1,), (1,)), ((), ())), preferred_element_type=F32)


def _dot_tn(a, b):
    return lax.dot_general(a, b, (((0,), (0,)), ((), ())), preferred_element_type=F32)


def _ffn_ln_apply(x, wg_ref, wu_ref, wo_ref, g_ref, b_ref):
    xb = x.astype(BF16)
    acc = jnp.zeros(x.shape, F32)
    for c in range(N_FF_CHUNKS):
        gate = _dot(xb, wg_ref[c])
        up = _dot(xb, wu_ref[c])
        h = (gate * jax.nn.sigmoid(gate)) * up
        acc = acc + _dot(h.astype(BF16), wo_ref[c])
    y = ALPHA * x + 0.5 * acc
    return _layernorm(y, g_ref[...], b_ref[...])


def _ffn_ln_kernel(x_ref, wg_ref, wu_ref, wo_ref, g_ref, b_ref, o_ref):
    o_ref[...] = _ffn_ln_apply(x_ref[...], wg_ref, wu_ref, wo_ref, g_ref, b_ref)


def _ffn_weight_specs():
    return [
        _const_spec((N_FF_CHUNKS, D_MODEL, FF_CHUNK)),
        _const_spec((N_FF_CHUNKS, D_MODEL, FF_CHUNK)),
        _const_spec((N_FF_CHUNKS, FF_CHUNK, D_MODEL)),
        _const_spec((1, D_MODEL)),
        _const_spec((1, D_MODEL)),
    ]


def _ffn_weight_args(fw):
    return (fw["wg"], fw["wu"], fw["wo"], fw["g"], fw["b"])


def _ffn_ln(x, fw, tm):
    n = x.shape[0]
    return pl.pallas_call(
        _ffn_ln_kernel,
        grid=(n // tm,),
        in_specs=[pl.BlockSpec((tm, D_MODEL), lambda i: (i, 0))] + _ffn_weight_specs(),
        out_specs=pl.BlockSpec((tm, D_MODEL), lambda i: (i, 0)),
        out_shape=jax.ShapeDtypeStruct((n, D_MODEL), F32),
        compiler_params=_cparams(("parallel",)),
        name="ffn_ln",
    )(x, *_ffn_weight_args(fw))


def _rope(v, cos, s1, s2):
    return v * cos + pltpu.roll(v, 8, 1) * s1 + pltpu.roll(v, LANES - 8, 1) * s2


def _inproj_kernel(x_ref, w_ref, cos_ref, s1_ref, s2_ref,
                   u_ref, lx_ref, lg_ref, qc_ref, qr_ref,
                   kvc_ref, kvs_ref, kvsb_ref, kvw_ref, kvwb_ref, gt_ref):
    xb = x_ref[...].astype(BF16)

    def mm(lo, hi):
        return _dot(xb, w_ref[:, lo:hi])

    glu = mm(C_GLU, C_GLU + 2 * D_CONV)
    u_ref[...] = glu[:, :D_CONV] * jax.nn.sigmoid(glu[:, D_CONV:])
    lx_ref[...] = mm(C_LRUX, C_LRUX + D_LRU)
    lg_ref[...] = mm(C_LRUG, C_LRUG + D_LRU)
    cos = cos_ref[...]
    s1 = s1_ref[...]
    s2 = s2_ref[...]
    for h in range(N_HEADS):
        qh = mm(C_Q + h * PAD_HEAD, C_Q + (h + 1) * PAD_HEAD) * SCALE
        qc_ref[:, h * PAD_HEAD:(h + 1) * PAD_HEAD] = qh.astype(BF16)
        qr_ref[:, h * PAD_HEAD:(h + 1) * PAD_HEAD] = _rope(qh, cos, s1, s2).astype(BF16)
    kvc_ref[...] = mm(C_KVC, C_KVC + KV_ROW)
    for c0, f_ref, b_ref in ((C_KVS, kvs_ref, kvsb_ref), (C_KVW, kvw_ref, kvwb_ref)):
        kv = mm(c0, c0 + KV_ROW)
        k = _rope(kv[:, :LANES], cos, s1, s2)
        v = kv[:, LANES:]
        f_ref[:, 0:LANES] = k
        f_ref[:, LANES:KV_ROW] = v
        b_ref[:, 0:LANES] = k.astype(BF16)
        b_ref[:, LANES:KV_ROW] = v.astype(BF16)
    gt_ref[...] = jax.nn.sigmoid(mm(C_GATE, N_PROJ))


def _rope_t(v, cos, s1, s2):
    return v * cos + pltpu.roll(v, 8, 0) * s1 + pltpu.roll(v, v.shape[0] - 8, 0) * s2


R_Q = 0
R_VS = D_ATTN
R_VW = R_VS + LANES
R_GATE = R_VW + LANES
N_PROJ_T = R_GATE + N_KV_HEADS * LANES


LOG2E = 1.4426950408889634
V_ROWS = HEAD_DIM + 16


def _inproj_t_kernel(x_ref, w_ref, wt_ref, cos_ref, s1_ref, s2_ref, cos_t_ref, s1_t_ref, s2_t_ref, blk_ref,
                     u_ref, lx_ref, lg_ref, kvc_ref, ks_ref, kw_ref,
                     kvct_ref, kvst_ref, kvwt_ref, qc_ref, qr_ref, vs_ref, vw_ref, gt_ref):
    xb = x_ref[...].astype(BF16)
    tm = xb.shape[0]

    def mm(lo, hi):
        return _dot(xb, w_ref[:, lo:hi])

    glu = mm(C_GLU, C_GLU + 2 * D_CONV)
    u_ref[...] = glu[:, :D_CONV] * jax.nn.sigmoid(glu[:, D_CONV:])
    lx_ref[...] = mm(C_LRUX, C_LRUX + D_LRU)
    lg_ref[...] = mm(C_LRUG, C_LRUG + D_LRU)
    kvc = mm(C_KVC, C_KVC + KV_ROW)
    kvc_ref[...] = kvc
    kvct_ref[...] = kvc.T
    cos = cos_ref[...]
    s1 = s1_ref[...]
    s2 = s2_ref[...]
    for c0, b_ref, leaf_ref in ((C_KVS, ks_ref, kvst_ref), (C_KVW, kw_ref, kvwt_ref)):
        k = _rope(mm(c0, c0 + LANES), cos, s1, s2)
        b_ref[:, 0:LANES] = k.astype(BF16)
        leaf_ref[0:LANES, :] = k.T
    ks_ref[:, LANES:] = blk_ref[...]

    all_t = _dot_nt(wt_ref[...], xb)
    cos_t = cos_t_ref[0:HEAD_DIM, :]
    s1_t = s1_t_ref[0:HEAD_DIM, :]
    s2_t = s2_t_ref[0:HEAD_DIM, :]
    no_rows = jnp.zeros((HEAD_DIM, tm), BF16)
    for h in range(N_HEADS):
        qh = all_t[R_Q + h * HEAD_DIM:R_Q + (h + 1) * HEAD_DIM, :]
        own = h * PAD_HEAD + (h // GROUP) * HEAD_DIM
        other = h * PAD_HEAD + (1 - h // GROUP) * HEAD_DIM
        qc_ref[own:own + HEAD_DIM, :] = (qh * SCALE).astype(BF16)
        qr_ref[own:own + HEAD_DIM, :] = (_rope_t(qh, cos_t, s1_t, s2_t) * (SCALE * LOG2E)).astype(BF16)
        qc_ref[other:other + HEAD_DIM, :] = no_rows
        qr_ref[other:other + HEAD_DIM, :] = no_rows
    sum_rows = jnp.where(lax.broadcasted_iota(I32, (V_ROWS - HEAD_DIM, tm), 0) == 0, 1.0, 0.0)
    for r0, leaf_ref, o_ref in ((R_VS, kvst_ref, vs_ref), (R_VW, kvwt_ref, vw_ref)):
        vt = all_t[r0:r0 + LANES, :]
        leaf_ref[LANES:KV_ROW, :] = vt
        for kv in range(N_KV_HEADS):
            vk = jnp.concatenate([vt[kv * HEAD_DIM:(kv + 1) * HEAD_DIM, :], sum_rows], axis=0).astype(BF16)
            for c in range(tm // LANES):
                o_ref[kv, c] = vk[:, c * LANES:(c + 1) * LANES]
    gt_ref[...] = jax.nn.sigmoid(all_t[R_GATE:N_PROJ_T, :])


def _inproj_t(x, w_all, w_t, rope_tabs, rope_tabs_t, bsz, t, tm):
    n = x.shape[0]
    tps = t // tm
    tok = lambda w: pl.BlockSpec((tm, w), lambda i: (i, 0))
    tab = pl.BlockSpec((tm, LANES), lambda i: (i % tps, 0))
    tab_t = pl.BlockSpec((PAD_HEAD, tm), lambda i: (0, i % tps))
    feat_t = lambda r: pl.BlockSpec((None, r, tm), lambda i: (i // tps, 0, i % tps))
    vt_spec = pl.BlockSpec((None, N_KV_HEADS, tm // LANES, V_ROWS, LANES), lambda i: (i // tps, 0, i % tps, 0, 0))
    n_sel = t // SEL_BLOCK
    row_outs = [(D_CONV, F32), (D_LRU, F32), (D_LRU, F32), (KV_ROW, F32), (LANES + n_sel, BF16), (LANES, BF16)]
    leaf_t = jax.ShapeDtypeStruct((bsz, KV_ROW, t), F32)
    vt_shape = jax.ShapeDtypeStruct((bsz, N_KV_HEADS, t // LANES, V_ROWS, LANES), BF16)
    block_id = (jnp.arange(t)[:, None] // SEL_BLOCK == jnp.arange(n_sel)[None, :]).astype(BF16)
    return pl.pallas_call(
        _inproj_t_kernel,
        grid=(n // tm,),
        in_specs=[tok(D_MODEL), _const_spec((D_MODEL, N_PROJ)), _const_spec((N_PROJ_T, D_MODEL)),
                  tab, tab, tab, tab_t, tab_t, tab_t,
                  pl.BlockSpec((tm, n_sel), lambda i: (i % tps, 0))],
        out_specs=[tok(w) for w, _ in row_outs]
        + [feat_t(KV_ROW)] * 3
        + [feat_t(Q_PAD), feat_t(Q_PAD), vt_spec, vt_spec, feat_t(N_KV_HEADS * LANES)],
        out_shape=[jax.ShapeDtypeStruct((n, w), dt) for w, dt in row_outs]
        + [leaf_t] * 3
        + [jax.ShapeDtypeStruct((bsz, Q_PAD, t), BF16), jax.ShapeDtypeStruct((bsz, Q_PAD, t), BF16),
           vt_shape, vt_shape, jax.ShapeDtypeStruct((bsz, N_KV_HEADS * LANES, t), F32)],
        compiler_params=_cparams(("parallel",)),
        name="inproj_t",
    )(x, w_all, w_t, *rope_tabs, *rope_tabs_t, block_id)


def _inproj(x, w_all, rope_tabs, tm, tiles_per_seq):
    n = x.shape[0]
    cos, s1, s2 = rope_tabs
    tok = lambda w: pl.BlockSpec((tm, w), lambda i: (i, 0))
    tab = pl.BlockSpec((tm, LANES), lambda i: (i % tiles_per_seq, 0))
    outs = [
        (D_CONV, F32), (D_LRU, F32), (D_LRU, F32), (Q_PAD, BF16), (Q_PAD, BF16),
        (KV_ROW, F32), (KV_ROW, F32), (KV_ROW, BF16), (KV_ROW, F32), (KV_ROW, BF16),
        (N_KV_HEADS * LANES, F32),
    ]
    return pl.pallas_call(
        _inproj_kernel,
        grid=(n // tm,),
        in_specs=[tok(D_MODEL), _const_spec((D_MODEL, N_PROJ)), tab, tab, tab],
        out_specs=[tok(w) for w, _ in outs],
        out_shape=[jax.ShapeDtypeStruct((n, w), dt) for w, dt in outs],
        compiler_params=_cparams(("parallel",)),
        name="inproj",
    )(x, w_all, cos, s1, s2)


CONV_HALO = 32
CONV_ROWS = 64


def _conv_kernel(u_ref, buf_ref, w_ref, b_ref, g_ref, bb_ref, o_ref, ext_ref, sh_ref):
    t = pl.program_id(1)
    tc = u_ref.shape[0]

    @pl.when(t == 0)
    def _():
        ext_ref[0:CONV_HALO, :] = buf_ref[...]

    ext_ref[CONV_HALO:CONV_HALO + tc, :] = u_ref[...]
    off = CONV_HALO - (CONV_WIDTH - 1)
    span = tc + CONV_HALO - SUBLANES
    for s in range(1, SUBLANES):
        sh_ref[s - 1, 0:span, :] = ext_ref[s:s + span, :]
    rows = min(CONV_ROWS, tc)
    for r0 in range(0, tc, rows):
        acc = jnp.zeros((rows, D_CONV), F32)
        for k in range(CONV_WIDTH):
            a, s = divmod(off + k, SUBLANES)
            src = ext_ref if s == 0 else sh_ref.at[s - 1]
            acc = acc + src[r0 + SUBLANES * a:r0 + SUBLANES * a + rows, :] * w_ref[k:k + 1, :]
        y = acc + b_ref[...]
        y = _layernorm(y, g_ref[...], bb_ref[...])
        o_ref[r0:r0 + rows, :] = (y * jax.nn.sigmoid(y)).astype(BF16)
    ext_ref[0:CONV_HALO, :] = ext_ref[tc:tc + CONV_HALO, :]


def _conv_group(u, buf, cw, tc):
    b, t, _ = u.shape
    return pl.pallas_call(
        _conv_kernel,
        grid=(b, t // tc),
        in_specs=[
            pl.BlockSpec((None, tc, D_CONV), lambda i, j: (i, j, 0)),
            pl.BlockSpec((None, CONV_HALO, D_CONV), lambda i, j: (i, 0, 0)),
            _const_spec((CONV_HALO, D_CONV)),
            _const_spec((1, D_CONV)), _const_spec((1, D_CONV)), _const_spec((1, D_CONV)),
        ],
        out_specs=pl.BlockSpec((None, tc, D_CONV), lambda i, j: (i, j, 0)),
        out_shape=jax.ShapeDtypeStruct((b, t, D_CONV), BF16),
        scratch_shapes=[pltpu.VMEM((tc + CONV_HALO, D_CONV), F32),
                        pltpu.VMEM((SUBLANES - 1, tc + CONV_HALO - SUBLANES, D_CONV), F32)],
        compiler_params=_cparams(("parallel", "arbitrary")),
        name="conv_group",
    )(u, buf, cw["w"], cw["b"], cw["ln_g"], cw["ln_b"])


LRU_HALO = 8


def _lru_kernel(last_row, x_ref, gate_ref, buf_ref, h0_ref, cw_ref, cb_ref, wr_ref, wi_ref,
                bg_ref, lam_ref, y_ref, hl_ref, ext_ref, hc_ref):
    t = pl.program_id(1)
    tl = x_ref.shape[0]

    @pl.when(t == 0)
    def _():
        ext_ref[0:LRU_HALO, :] = buf_ref[...]
        hc_ref[...] = h0_ref[...]

    ext_ref[LRU_HALO:LRU_HALO + tl, :] = x_ref[...]
    off = LRU_HALO - (LRU_CONV_WIDTH - 1)
    xl = jnp.zeros((tl, D_LRU), F32)
    for k in range(LRU_CONV_WIDTH):
        xl = xl + ext_ref[off + k:off + k + tl, :] * cw_ref[k:k + 1, :]
    xl = xl + cb_ref[...]
    ext_ref[0:LRU_HALO, :] = ext_ref[tl:tl + LRU_HALO, :]

    xb = xl.astype(BF16)
    r_gate = jax.nn.sigmoid(_dot(xb, wr_ref[...]) + bg_ref[0:1, :])
    i_gate = jax.nn.sigmoid(_dot(xb, wi_ref[...]) + bg_ref[1:2, :])
    log_a = LRU_C * r_gate * jax.nn.log_sigmoid(lam_ref[...])
    a = jnp.exp(log_a)
    bv = jnp.sqrt(-jnp.tanh(log_a) * (a * a + 1.0)) * (i_gate * xl)

    row = lax.broadcasted_iota(I32, (tl, D_LRU), 0)
    s = 1
    while s < tl:
        keep = row >= s
        a_sh = jnp.where(keep, pltpu.roll(a, s, 0), 1.0)
        b_sh = jnp.where(keep, pltpu.roll(bv, s, 0), 0.0)
        bv = a * b_sh + bv
        a = a * a_sh
        s *= 2
    h = a * hc_ref[0:1, :] + bv
    hc_ref[...] = jnp.broadcast_to(h[last_row:last_row + 1, :], hc_ref.shape)
    hl_ref[...] = hc_ref[...]
    y_ref[...] = (h * jax.nn.gelu(gate_ref[...])).astype(BF16)


def _lru_group(x, gate, buf, h0, lw, tl, last_row):
    b, t, _ = x.shape
    assert last_row == tl - 1 or t == tl
    seq = pl.BlockSpec((None, tl, D_LRU), lambda i, j: (i, j, 0))
    per_b = pl.BlockSpec((None, LRU_HALO, D_LRU), lambda i, j: (i, 0, 0))
    return pl.pallas_call(
        functools.partial(_lru_kernel, last_row),
        grid=(b, t // tl),
        in_specs=[seq, seq, per_b, per_b,
                  _const_spec((LRU_HALO, D_LRU)), _const_spec((1, D_LRU)),
                  _const_spec((D_LRU, D_LRU)), _const_spec((D_LRU, D_LRU)),
                  _const_spec((2, D_LRU)), _const_spec((1, D_LRU))],
        out_specs=[seq, per_b],
        out_shape=[jax.ShapeDtypeStruct((b, t, D_LRU), BF16),
                   jax.ShapeDtypeStruct((b, LRU_HALO, D_LRU), F32)],
        scratch_shapes=[pltpu.VMEM((tl + LRU_HALO, D_LRU), F32),
                        pltpu.VMEM((LRU_HALO, D_LRU), F32)],
        compiler_params=_cparams(("parallel", "arbitrary")),
        name="lru_group",
    )(x, gate, buf, h0, lw["cw"], lw["cb"], lw["wr"], lw["wi"], lw["bg"], lw["lam"])


def _compress_rows(row_refs, pe_ref, w1_ref, b1_ref, w2_ref, b2_ref, carry_ref):
    xs = _chunk_vectors(row_refs, pe_ref)
    return _compress_mlp(lambda j, g: xs[j][g], w1_ref, b1_ref, w2_ref, b2_ref, carry_ref)


def _chunk_vectors(row_refs, pe_ref):
    lo = lax.broadcasted_iota(I32, (1, LANES), 1) < HEAD_DIM
    xs = [[] for _ in range(4)]
    for halves in row_refs:
        n = halves[0].shape[0] // CMP_STRIDE
        cols = [[] for _ in range(4)]
        for rp in range(CMP_STRIDE // 2):
            for half, ref in enumerate(halves):
                pa = ref[pl.ds(2 * rp, n, stride=CMP_STRIDE), :]
                pb = ref[pl.ds(2 * rp + 1, n, stride=CMP_STRIDE), :]
                cols[2 * half].append(jnp.where(lo, pa, pltpu.roll(pb, HEAD_DIM, 1)))
                cols[2 * half + 1].append(jnp.where(lo, pltpu.roll(pa, HEAD_DIM, 1), pb))
        for g in range(4):
            xs[g].append(jnp.concatenate(cols[g], axis=1))
    out = [[], []]
    for g in range(4):
        x = jnp.concatenate(xs[g], axis=0)
        for j in range(2):
            out[j].append((x + pe_ref[j, g:g + 1, :]).astype(BF16))
    return out


def _compress_mlp(get_x, w1_ref, b1_ref, w2_ref, b2_ref, carry_ref):
    out = None
    for g in range(4):
        sidx = g // 2
        p0 = _dot(get_x(0, g), w1_ref[sidx, 0])
        p1 = _dot(get_x(1, g), w1_ref[sidx, 1])
        n_tot = p0.shape[0]
        row = lax.broadcasted_iota(I32, p0.shape, 0)
        p0s = jnp.where(row == 0, carry_ref[g, 0:1, :], pltpu.roll(p0, 1, 0))
        carry_ref[g, 0:1, :] = p0[n_tot - 1:n_tot, :]
        h = (b1_ref[sidx:sidx + 1, :] + p0s) + p1
        part = _dot(jax.nn.gelu(h).astype(BF16), w2_ref[g])
        out = part if out is None else out + part
    return out + b2_ref[...]


def _cmp_weight_specs():
    return [
        _const_spec((2, 4, CMP_STRIDE * HEAD_DIM)),
        _const_spec((2, 2, CMP_STRIDE * HEAD_DIM, CMP_HIDDEN)),
        _const_spec((2, CMP_HIDDEN)),
        _const_spec((4, CMP_HIDDEN, KV_ROW)),
        _const_spec((1, KV_ROW)),
    ]


def _cmp_weight_args(cw):
    return (cw["pe"], cw["w1"], cw["b1"], cw["w2"], cw["b2"])


CMP_TILE_ROWS = 2048


def _compress_prompt_kernel(k_ref, v_ref, pe_ref, w1_ref, b1_ref, w2_ref, b2_ref, kc_ref, vct_ref, carry_ref):
    @pl.when(pl.program_id(1) == 0)
    def _():
        carry_ref[...] = jnp.zeros(carry_ref.shape, F32)

    out = _compress_rows([(k_ref, v_ref)], pe_ref, w1_ref, b1_ref, w2_ref, b2_ref, carry_ref)
    kc_ref[...] = out[:, 0:LANES].astype(BF16)
    vct_ref[...] = out[:, LANES:KV_ROW].T.astype(BF16)


def _compress_prompt(kvc, cw):
    b, t, _ = kvc.shape
    n_e = CMP_TILE_ROWS // CMP_STRIDE
    return pl.pallas_call(
        _compress_prompt_kernel,
        grid=(b, t // CMP_TILE_ROWS),
        in_specs=[pl.BlockSpec((None, CMP_TILE_ROWS, LANES), lambda i, j: (i, j, 0)),
                  pl.BlockSpec((None, CMP_TILE_ROWS, LANES), lambda i, j: (i, j, 1))]
        + _cmp_weight_specs(),
        out_specs=[pl.BlockSpec((None, n_e, LANES), lambda i, j: (i, j, 0)),
                   pl.BlockSpec((None, LANES, n_e), lambda i, j: (i, 0, j))],
        out_shape=[jax.ShapeDtypeStruct((b, t // CMP_STRIDE, LANES), BF16),
                   jax.ShapeDtypeStruct((b, LANES, t // CMP_STRIDE), BF16)],
        scratch_shapes=[pltpu.VMEM((4, 8, CMP_HIDDEN), F32)],
        compiler_params=_cparams(("parallel", "arbitrary")),
        name="compress_prompt",
    )(kvc, kvc, *_cmp_weight_args(cw))


def _selection_scores(pk_ref, n_sel):
    ratio = SEL_BLOCK // CMP_STRIDE
    slc = pk_ref[pl.ds(0, n_sel, stride=ratio), :]
    for o in range(1, ratio):
        slc = slc + 2.0 * pk_ref[pl.ds(o, n_sel, stride=ratio), :]
    return slc + pk_ref[pl.ds(ratio, n_sel, stride=ratio), :]


def _topk_rounds(score, n_rows, rounds=N_SEL):
    j = lax.broadcasted_iota(I32, score.shape, 0).astype(F32)
    sel = jnp.zeros(score.shape, F32)
    picks = []
    for _ in range(rounds):
        cm = jnp.max(score, axis=0, keepdims=True)
        mi = jnp.min(jnp.where(score == cm, j, float(n_rows)), axis=0, keepdims=True)
        hit = j == mi
        sel = jnp.where(hit, 1.0, sel)
        score = jnp.where(hit, -jnp.inf, score)
        picks.append(mi)
    return sel, picks


def _masked_softmax_rows(s_t, valid):
    s_t = jnp.where(valid, s_t, -jnp.inf)
    m = jnp.max(s_t, axis=0, keepdims=True)
    m = jnp.where(m > -jnp.inf, m, 0.0)
    e = jnp.exp(s_t - m)
    d = jnp.sum(e, axis=0, keepdims=True)
    return e / jnp.where(d > 0, d, 1.0)


CMP_ENTRY_CHUNK = 128
N_FORCED = 3


def _cmp_topk_prompt_kernel(qc_ref, kc_ref, vct_ref, ocmp_ref, mneg_ref, pk0_ref, pk1_ref):
    i = pl.program_id(1)
    n_e = kc_ref.shape[0]
    n_sel = n_e // (SEL_BLOCK // CMP_STRIDE)
    pk_refs = (pk0_ref, pk1_ref)

    def attend(n_use):
        kc = kc_ref[0:n_use, :]
        vct = vct_ref[:, 0:n_use]
        e_idx = lax.broadcasted_iota(I32, (n_use, Q_BLOCK), 0)
        qpos = i * Q_BLOCK + lax.broadcasted_iota(I32, (n_use, Q_BLOCK), 1)
        valid = (e_idx >= 1) & (CMP_STRIDE * e_idx + (CMP_STRIDE - 1) <= qpos)
        for kv in range(N_KV_HEADS):
            pkv = jnp.zeros((n_use, Q_BLOCK), F32)
            for h in range(kv * GROUP, (kv + 1) * GROUP):
                rows = slice(h * PAD_HEAD, (h + 1) * PAD_HEAD)
                p = _masked_softmax_rows(_dot(kc, qc_ref[rows, :]), valid)
                pkv = pkv + p
                ocmp_ref[rows, :] = _dot(vct, p.astype(BF16))
            pk_refs[kv][0:n_use, :] = pkv
            pk_refs[kv][n_use:, :] = jnp.zeros((n_e + 8 - n_use, Q_BLOCK), F32)

    n_var = n_e // CMP_ENTRY_CHUNK
    variant = jnp.minimum((8 * i + 7) // CMP_ENTRY_CHUNK, n_var - 1)
    for k in range(n_var):
        pl.when(variant == k)(functools.partial(attend, CMP_ENTRY_CHUNK * (k + 1)))

    j = lax.broadcasted_iota(I32, (n_sel, Q_BLOCK), 0)
    qp = i * Q_BLOCK + lax.broadcasted_iota(I32, (n_sel, Q_BLOCK), 1)
    qblk = jnp.right_shift(qp, 6)
    forced = (j == 0) | (j == qblk) | (j == qblk - 1)
    others = [jnp.where(forced | (j * SEL_BLOCK > qp), -jnp.inf, _selection_scores(r, n_sel)) for r in pk_refs]
    sel, _ = _topk_rounds(jnp.concatenate(others, axis=1), n_sel, N_SEL - N_FORCED)
    chosen = (sel > 0) | jnp.concatenate([forced] * N_KV_HEADS, axis=1)
    mneg = jnp.where(chosen, 0.0, MASK_NEG).astype(BF16)
    for kv in range(N_KV_HEADS):
        mneg_ref[kv] = mneg[:, kv * Q_BLOCK:(kv + 1) * Q_BLOCK]


def _cmp_topk_prompt(qc_t, kc, vc_t):
    b, _, t = qc_t.shape
    n_e = kc.shape[1]
    n_sel = t // SEL_BLOCK
    assert n_e % CMP_ENTRY_CHUNK == 0
    qblk = pl.BlockSpec((None, Q_PAD, Q_BLOCK), lambda bi, i: (bi, 0, i))
    return pl.pallas_call(
        _cmp_topk_prompt_kernel,
        grid=(b, t // Q_BLOCK),
        in_specs=[
            qblk,
            pl.BlockSpec((None, n_e, LANES), lambda bi, i: (bi, 0, 0)),
            pl.BlockSpec((None, LANES, n_e), lambda bi, i: (bi, 0, 0)),
        ],
        out_specs=[
            qblk,
            pl.BlockSpec((None, N_KV_HEADS, n_sel, Q_BLOCK), lambda bi, i: (bi, 0, 0, i)),
        ],
        out_shape=[jax.ShapeDtypeStruct((b, Q_PAD, t), F32),
                   jax.ShapeDtypeStruct((b, N_KV_HEADS, n_sel, t), BF16)],
        scratch_shapes=[pltpu.VMEM((n_e + 8, Q_BLOCK), F32), pltpu.VMEM((n_e + 8, Q_BLOCK), F32)],
        compiler_params=_cparams(("parallel", "arbitrary")),
        name="cmp_topk_prompt",
    )(qc_t, kc, vc_t)


def _selwin_prompt_kernel(qr_ref, mneg_ref, ks_ref, vs_ref, kw_ref, vw_ref, ocmp_ref, gt_ref, o_ref,
                          qa_ref, sa_ref, sb_ref, pa_ref, pb_ref):
    i = pl.program_id(1)
    n_sel = mneg_ref.shape[1]
    qb = qr_ref.shape[1]
    win_keys = WINDOW + qb
    cols = N_HEADS * qb
    kv_cols = GROUP * qb
    for h in range(N_HEADS):
        c = slice(h * qb, (h + 1) * qb)
        qa_ref[0:PAD_HEAD, c] = qr_ref[h * PAD_HEAD:(h + 1) * PAD_HEAD, :]
        qa_ref[PAD_HEAD:PAD_HEAD + n_sel, c] = mneg_ref[h // GROUP]
    qa = qa_ref[...]
    q0 = i * qb
    qpos = q0 + (lax.broadcasted_iota(I32, (1, cols), 1) & (qb - 1))

    tile = SEL_TILE
    key_col = lax.broadcasted_iota(I32, (tile, 1), 0)

    def scores(t):
        return _dot(ks_ref[pl.ds(pl.multiple_of(t * tile, tile), tile), :], qa)

    def weighted_values(v_ref, first_tile, n_tiles, p):
        outs = []
        for kv in range(N_KV_HEADS):
            vk = v_ref[kv, pl.ds(first_tile, n_tiles)]
            vt = jnp.concatenate([vk[c] for c in range(n_tiles)], axis=1)
            outs.append(_dot(vt, p[:, kv * kv_cols:(kv + 1) * kv_cols]))
        return jnp.concatenate(outs, axis=1)

    def normalise(acc):
        return acc[0:HEAD_DIM, :] / acc[HEAD_DIM:HEAD_DIM + 1, :]

    start = pl.multiple_of(jnp.maximum(q0 - WINDOW, 0), LANES)
    s = _dot(kw_ref[pl.ds(start, win_keys), :], qa[0:PAD_HEAD, :])
    dpos = qpos - (start + lax.broadcasted_iota(I32, (win_keys, 1), 0))
    s = jnp.where(lax.bitcast_convert_type(dpos, jnp.uint32) <= jnp.uint32(WINDOW), s, -jnp.inf)
    p = jnp.exp2(s - jnp.max(s, axis=0, keepdims=True))
    o_win = normalise(weighted_values(vw_ref, start // LANES, win_keys // LANES, p.astype(BF16)))

    sub = tile // LANES

    def stage(t, cur, nxt, carry, masked, prefetch):
        m, acc, alpha_prev = carry
        if prefetch:
            s_refs[nxt][...] = scores(t + 1)
        acc = alpha_prev * acc + weighted_values(vs_ref, jnp.maximum(t - 1, 0) * sub, sub, p_refs[nxt][...])
        s = s_refs[cur][...]
        if masked:
            s = jnp.where(t * tile + key_col <= qpos, s, MASK_NEG)
        m_new = jnp.maximum(m, jnp.max(s, axis=0, keepdims=True))
        alpha = jnp.exp2(m - m_new)
        p_refs[cur][...] = jnp.exp2(s - m_new).astype(BF16)
        return m_new, acc, alpha

    s_refs = (sa_ref, sb_ref)
    p_refs = (pa_ref, pb_ref)
    sa_ref[...] = scores(0)
    pb_ref[...] = jnp.zeros(pb_ref.shape, BF16)
    init = (jnp.full((1, cols), MASK_NEG, F32), jnp.zeros((V_ROWS, cols), F32), jnp.ones((1, cols), F32))

    def pair(u, carry):
        carry = stage(2 * u, 0, 1, carry, False, True)
        return stage(2 * u + 1, 1, 0, carry, False, True)

    u_diag = q0 // (2 * tile)
    carry = lax.fori_loop(0, u_diag, pair, init)
    carry = stage(2 * u_diag, 0, 1, carry, True, True)

    def finish_second(c):
        _, acc, alpha = stage(2 * u_diag + 1, 1, 0, c, True, False)
        return alpha * acc + weighted_values(vs_ref, (2 * u_diag + 1) * sub, sub, pb_ref[...])

    def finish_first(c):
        _, acc, alpha = c
        return alpha * acc + weighted_values(vs_ref, 2 * u_diag * sub, sub, pa_ref[...])

    o_sel = normalise(lax.cond(q0 - 2 * u_diag * tile >= tile, finish_second, finish_first, carry))

    no_rows = jnp.zeros((HEAD_DIM, qb), BF16)
    for h in range(N_HEADS):
        c = slice(h * qb, (h + 1) * qb)
        own = h * PAD_HEAD + (h // GROUP) * HEAD_DIM
        other = h * PAD_HEAD + (1 - h // GROUP) * HEAD_DIM
        g0 = (h // GROUP) * LANES + 3 * (h % GROUP)
        o = (gt_ref[g0:g0 + 1, :] * ocmp_ref[own:own + HEAD_DIM, :] + gt_ref[g0 + 1:g0 + 2, :] * o_sel[:, c]
             + gt_ref[g0 + 2:g0 + 3, :] * o_win[:, c])
        o_ref[own:own + HEAD_DIM, :] = o.astype(BF16)
        o_ref[other:other + HEAD_DIM, :] = no_rows


def _selwin_prompt(qr_t, mneg, ks, vs_t, kw, vw_t, ocmp_t, gates_t):
    b, _, t = qr_t.shape
    n_sel = mneg.shape[2]
    qb = SELWIN_QUERIES
    assert t % (2 * SEL_TILE) == 0 and SEL_TILE % qb == 0 and t >= WINDOW + qb
    cols = N_HEADS * qb
    per_q = lambda r: pl.BlockSpec((None, r, qb), lambda bi, i: (bi, 0, i))
    per_b = lambda *shape: pl.BlockSpec((None,) + shape, lambda bi, i: (bi,) + (0,) * len(shape),
                                        pipeline_mode=pl.Buffered(1))
    vals = per_b(N_KV_HEADS, t // LANES, V_ROWS, LANES)
    return pl.pallas_call(
        _selwin_prompt_kernel,
        grid=(b, t // qb),
        in_specs=[
            per_q(Q_PAD),
            pl.BlockSpec((None, N_KV_HEADS, n_sel, qb), lambda bi, i: (bi, 0, 0, i)),
            per_b(t, LANES + n_sel), vals, per_b(t, LANES), vals,
            per_q(Q_PAD), per_q(N_KV_HEADS * LANES),
        ],
        out_specs=per_q(Q_PAD),
        out_shape=jax.ShapeDtypeStruct((b, Q_PAD, t), BF16),
        scratch_shapes=[pltpu.VMEM((PAD_HEAD + n_sel, cols), BF16),
                        pltpu.VMEM((SEL_TILE, cols), F32),
                        pltpu.VMEM((SEL_TILE, cols), F32),
                        pltpu.VMEM((SEL_TILE, cols), BF16),
                        pltpu.VMEM((SEL_TILE, cols), BF16)],
        compiler_params=_cparams(("parallel", "arbitrary")),
        name="selwin_prompt",
    )(qr_t, mneg, ks, vs_t, kw, vw_t, ocmp_t, gates_t)


PAGES_PER_STEP = 16
S_COLS = LANES


def _cmp_topk_sample_kernel(n_t, pt_ref, *refs):
    page_refs = refs[:PAGES_PER_STEP]
    (q_ref, pe_ref, w1_ref, b1_ref, w2_ref, b2_ref,
     ocmp_ref, idx_ref, kcv_ref, pk_ref, carry_ref, xk_ref, xv_ref) = refs[PAGES_PER_STEP:]
    s = pl.program_id(1)
    n_steps = pl.num_programs(1)
    n_e = kcv_ref.shape[0]
    step_e = PAGES_PER_STEP * PAGE_SIZE // CMP_STRIDE

    @pl.when(s == 0)
    def _():
        carry_ref[...] = jnp.zeros(carry_ref.shape, F32)

    for k, page in enumerate(page_refs):
        rows = slice(k * PAGE_SIZE, (k + 1) * PAGE_SIZE)
        xk_ref[rows, :] = page[0:LANES, :].T
        xv_ref[rows, :] = page[LANES:KV_ROW, :].T
    out = _compress_rows([(xk_ref, xv_ref)], pe_ref, w1_ref, b1_ref, w2_ref, b2_ref, carry_ref)
    kcv_ref[pl.ds(pl.multiple_of(s * step_e, step_e), step_e), :] = out.astype(BF16)

    @pl.when(s == n_steps - 1)
    def _():
        n_sel = n_e // (SEL_BLOCK // CMP_STRIDE) + 1
        n_sel_rows = pk_ref.shape[0] // (SEL_BLOCK // CMP_STRIDE) - 2
        kc = kcv_ref[:, 0:LANES]
        vc = kcv_ref[:, LANES:KV_ROW]
        e_idx = lax.broadcasted_iota(I32, (n_e, S_COLS), 0)
        p = _masked_softmax_rows(_dot_nt(kc, q_ref[...]), e_idx >= 1)
        ocmp_ref[...] = _dot_tn(p.astype(BF16), vc)
        pkv = p
        for hh in range(1, GROUP):
            pkv = pkv + pltpu.roll(p, S_COLS - n_t * hh, 1)
        pk_ref[0:n_e, :] = pkv
        pk_ref[n_e:, :] = jnp.zeros((pk_ref.shape[0] - n_e, S_COLS), F32)
        slc = _selection_scores(pk_ref, n_sel_rows)
        j = lax.broadcasted_iota(I32, (n_sel_rows, S_COLS), 0)
        qp = PAST_LEN + (lax.broadcasted_iota(I32, (n_sel_rows, S_COLS), 1) & (n_t - 1))
        qblk = jnp.right_shift(qp, 6)
        forced = (j == 0) | (j == qblk) | (j == qblk - 1)
        in_range = j < n_sel
        score = jnp.where(forced & in_range, jnp.inf,
                          jnp.where((j * SEL_BLOCK <= qp) & in_range, slc, -jnp.inf))
        _, picks = _topk_rounds(score, n_sel_rows)
        for r, mi in enumerate(picks):
            idx_ref[r:r + 1, :] = mi.astype(I32)


def _cmp_topk_sample(page_table, cache_cmp_t, layer, q_cols, cw, n_t):
    assert n_t & (n_t - 1) == 0 and N_HEADS * n_t <= S_COLS
    b, n_pages = page_table.shape
    n_e = n_pages * PAGE_SIZE // CMP_STRIDE
    n_sel_rows = ((n_e // 4 + 1) + 7) // 8 * 8
    pk_rows = 4 * (n_sel_rows + 2)
    n_steps = n_pages // PAGES_PER_STEP
    step_rows = PAGES_PER_STEP * PAGE_SIZE

    def page_spec(k):
        return pl.BlockSpec((None, None, KV_ROW, PAGE_SIZE),
                            lambda bi, s, pt: (layer, pt[bi, s * PAGES_PER_STEP + k], 0, 0))

    per_b = lambda rows, w: pl.BlockSpec((None, rows, w), lambda bi, s, pt: (bi, 0, 0))
    grid_spec = pltpu.PrefetchScalarGridSpec(
        num_scalar_prefetch=1,
        grid=(b, n_steps),
        in_specs=[page_spec(k) for k in range(PAGES_PER_STEP)]
        + [per_b(S_COLS, LANES)] + _cmp_weight_specs(),
        out_specs=[per_b(S_COLS, LANES), per_b(N_SEL, S_COLS)],
        scratch_shapes=[pltpu.VMEM((n_e, KV_ROW), BF16),
                        pltpu.VMEM((pk_rows, S_COLS), F32),
                        pltpu.VMEM((4, 8, CMP_HIDDEN), F32),
                        pltpu.VMEM((step_rows, LANES), F32),
                        pltpu.VMEM((step_rows, LANES), F32)],
    )
    return pl.pallas_call(
        functools.partial(_cmp_topk_sample_kernel, n_t),
        grid_spec=grid_spec,
        out_shape=[jax.ShapeDtypeStruct((b, S_COLS, LANES), F32),
                   jax.ShapeDtypeStruct((b, N_SEL, S_COLS), I32)],
        compiler_params=_cparams(("parallel", "arbitrary")),
        name="cmp_topk_sample",
    )(page_table, *([cache_cmp_t] * PAGES_PER_STEP), q_cols, *_cmp_weight_args(cw))


HROWS = 8


def _selwin_sample_kernel(idx_ref, pt_ref, *refs):
    page_refs = refs[:N_SEL]
    (q_ref, new_s_ref, win_ref, new_w_ref, ocmp_ref, gt_ref, o_ref, k_ref, v_ref) = refs[N_SEL:]
    bi = pl.program_id(0)
    tq = pl.program_id(1)
    kh = pl.program_id(2)
    n_t = pl.num_programs(1)
    qpos = PAST_LEN + tq
    q = q_ref[...]
    base = ((bi * n_t + tq) * N_KV_HEADS + kh) * N_SEL
    new_block = PAST_LEN // SEL_BLOCK

    valid_parts = []
    lane = lax.broadcasted_iota(I32, (1, PAGE_SIZE), 1)
    picked_new = False
    for r in range(N_SEL):
        j = idx_ref[base + r]
        cols = slice(r * PAGE_SIZE, (r + 1) * PAGE_SIZE)
        k_ref[:, cols] = page_refs[r][0:LANES, :].astype(BF16)
        v_ref[:, cols] = page_refs[r][LANES:KV_ROW, :].astype(BF16)
        kpos = jnp.right_shift(j, 1) * PAGE_SIZE + lane
        valid_parts.append((jnp.right_shift(kpos, 6) == j) & (j != new_block))
        picked_new = jnp.logical_or(picked_new, j == new_block)
    cols = slice(N_SEL * PAGE_SIZE, (N_SEL + 1) * PAGE_SIZE)
    k_ref[:, cols] = new_s_ref[0:LANES, :].astype(BF16)
    v_ref[:, cols] = new_s_ref[LANES:KV_ROW, :].astype(BF16)
    new_pos = PAST_LEN + lane
    last_new = jnp.where(picked_new, qpos, -1)
    valid_parts.append((jnp.right_shift(new_pos, 6) == new_block) & (new_pos <= last_new))
    valid = jnp.concatenate(valid_parts, axis=1)
    s = jnp.where(valid, _dot(q, k_ref[...]), -jnp.inf)
    m = jnp.max(s, axis=-1, keepdims=True)
    p = jnp.exp(s - m)
    o_sel = _dot_nt(p.astype(BF16), v_ref[...]) / jnp.sum(p, axis=-1, keepdims=True)

    wb = win_ref.shape[1]
    s_old = _dot(q, win_ref[0:LANES, :].astype(BF16))
    d_old = qpos - (PAST_LEN - wb + lax.broadcasted_iota(I32, (1, wb), 1))
    s_old = jnp.where((d_old >= 0) & (d_old <= WINDOW), s_old, -jnp.inf)
    s_new = _dot(q, new_w_ref[0:LANES, :].astype(BF16))
    d_new = tq - lax.broadcasted_iota(I32, (1, new_w_ref.shape[1]), 1)
    s_new = jnp.where((d_new >= 0) & (d_new <= WINDOW), s_new, -jnp.inf)
    m = jnp.maximum(jnp.max(s_old, axis=-1, keepdims=True), jnp.max(s_new, axis=-1, keepdims=True))
    p_old = jnp.exp(s_old - m)
    p_new = jnp.exp(s_new - m)
    den = jnp.sum(p_old, axis=-1, keepdims=True) + jnp.sum(p_new, axis=-1, keepdims=True)
    o_win = (_dot_nt(p_old.astype(BF16), win_ref[LANES:KV_ROW, :].astype(BF16))
             + _dot_nt(p_new.astype(BF16), new_w_ref[LANES:KV_ROW, :].astype(BF16))) / den

    g = gt_ref[...]
    o = g[:, 0:1] * ocmp_ref[...] + g[:, 1:2] * o_sel + g[:, 2:3] * o_win
    o_ref[...] = o.astype(BF16)


def _selwin_sample(idx_flat, page_table, cache_sel_t, win_t, layer, q_rows, new_s_t, new_w_t, ocmp_rows, gate_rows):
    b, n_t = q_rows.shape[:2]
    wb = win_t.shape[-1]
    n_t_static = n_t
    last_page = page_table.shape[1] - 1

    def page_spec(r):
        def imap(bi, tq, kh, idx, pt):
            j = idx[((bi * n_t_static + tq) * N_KV_HEADS + kh) * N_SEL + r]
            return (layer, pt[bi, jnp.minimum(jnp.right_shift(j, 1), last_page)], 0, 0)
        return pl.BlockSpec((None, None, KV_ROW, PAGE_SIZE), imap)

    row5 = pl.BlockSpec((None, None, None, HROWS, LANES), lambda bi, tq, kh, idx, pt: (bi, tq, kh, 0, 0))
    new_rows = pl.BlockSpec((None, KV_ROW, LANES), lambda bi, tq, kh, idx, pt: (bi, 0, 0))
    grid_spec = pltpu.PrefetchScalarGridSpec(
        num_scalar_prefetch=2,
        grid=(b, n_t, N_KV_HEADS),
        in_specs=[page_spec(r) for r in range(N_SEL)]
        + [row5, new_rows,
           pl.BlockSpec((None, None, KV_ROW, wb), lambda bi, tq, kh, idx, pt: (layer, bi, 0, 0)),
           new_rows, row5, row5],
        out_specs=row5,
        scratch_shapes=[pltpu.VMEM((LANES, (N_SEL + 1) * PAGE_SIZE), BF16),
                        pltpu.VMEM((LANES, (N_SEL + 1) * PAGE_SIZE), BF16)],
    )
    return pl.pallas_call(
        _selwin_sample_kernel,
        grid_spec=grid_spec,
        out_shape=jax.ShapeDtypeStruct((b, n_t, N_KV_HEADS, HROWS, LANES), BF16),
        compiler_params=_cparams(("arbitrary", "arbitrary", "arbitrary")),
        name="selwin_sample",
    )(idx_flat, page_table, *([cache_sel_t] * N_SEL), q_rows, new_s_t, win_t, new_w_t, ocmp_rows, gate_rows)


def _outproj_ln_kernel(attn_transposed, x_ref, yc_ref, yl_ref, ya_ref, w_ref, g_ref, b_ref,
                       wg_ref, wu_ref, wo_ref, g2_ref, b2_ref, o_ref):
    y = _dot(yc_ref[...], w_ref[0:D_CONV, :])
    y = y + _dot(yl_ref[...], w_ref[D_CONV:D_CONV + D_LRU, :])
    w_attn = w_ref[D_CONV + D_LRU:, :]
    y = y + (_dot_tn(ya_ref[...], w_attn) if attn_transposed else _dot(ya_ref[...], w_attn))
    x = _layernorm(ALPHA * x_ref[...] + y, g_ref[...], b_ref[...])
    o_ref[...] = _ffn_ln_apply(x, wg_ref, wu_ref, wo_ref, g2_ref, b2_ref)


def _outproj_ffn(x, yc, yl, ya, ow, fw, tm):
    n = x.shape[0]
    tok = lambda w: pl.BlockSpec((tm, w), lambda i: (i, 0))
    attn_transposed = ya.ndim == 3
    if attn_transposed:
        tps = ya.shape[2] // tm
        ya_spec = pl.BlockSpec((None, Q_PAD, tm), lambda i: (i // tps, 0, i % tps))
    else:
        ya_spec = tok(Q_PAD)
    return pl.pallas_call(
        functools.partial(_outproj_ln_kernel, attn_transposed),
        grid=(n // tm,),
        in_specs=[tok(D_MODEL), tok(D_CONV), tok(D_LRU), ya_spec,
                  _const_spec((D_CONV + D_LRU + Q_PAD, D_MODEL)),
                  _const_spec((1, D_MODEL)), _const_spec((1, D_MODEL))] + _ffn_weight_specs(),
        out_specs=tok(D_MODEL),
        out_shape=jax.ShapeDtypeStruct((n, D_MODEL), F32),
        compiler_params=_cparams(("parallel",)),
        name="outproj_ffn",
    )(x, yc, yl, ya, ow["w"], ow["g"], ow["b"], *_ffn_weight_args(fw))


def _rope_tables(pos):
    half = ROPE_DIM // 2
    inv = ROPE_THETA ** (-jnp.arange(half, dtype=F32) / half)
    ang = pos.astype(F32)[:, None] * inv[None, :]
    cos, sin = jnp.cos(ang), jnp.sin(ang)
    n = pos.shape[0]
    rest = HEAD_DIM - ROPE_DIM
    zeros8 = jnp.zeros((n, half), F32)
    c = jnp.concatenate([cos, cos, jnp.ones((n, rest), F32)], axis=1)
    s1 = jnp.concatenate([zeros8, sin, jnp.zeros((n, rest), F32)], axis=1)
    s2 = jnp.concatenate([-sin, zeros8, jnp.zeros((n, rest), F32)], axis=1)
    rep = LANES // HEAD_DIM
    return tuple(jnp.tile(a, (1, rep)) for a in (c, s1, s2))


def _head_pad_index():
    h = np.arange(D_ATTN) // HEAD_DIM
    d = np.arange(D_ATTN) % HEAD_DIM
    return h * PAD_HEAD + (h // GROUP) * HEAD_DIM + d


def _prep_layer(l, ln_g, ln_b, ffn_w_in, ffn_w_out, w_in, conv_w, conv_b, conv_ln_g, conv_ln_b,
                lru_conv_w, lru_conv_b, lru_w_gate, lru_b_gate, lru_lambda,
                cmp_pe, cmp_w1, cmp_b1, cmp_w2, cmp_b2, w_out):
    row = lambda v: v.reshape(1, -1).astype(F32)
    ffn = []
    for f, ln_i in ((0, 0), (1, 2)):
        wi = ffn_w_in[l, f]
        wg = wi[:, :D_FF].reshape(D_MODEL, N_FF_CHUNKS, FF_CHUNK).transpose(1, 0, 2).astype(BF16)
        wu = wi[:, D_FF:].reshape(D_MODEL, N_FF_CHUNKS, FF_CHUNK).transpose(1, 0, 2).astype(BF16)
        wo = ffn_w_out[l, f].reshape(N_FF_CHUNKS, FF_CHUNK, D_MODEL).astype(BF16)
        ffn.append({"wg": wg, "wu": wu, "wo": wo, "g": row(ln_g[l, ln_i]), "b": row(ln_b[l, ln_i])})

    wl = w_in[l]
    o_q = 2 * D_CONV + 2 * D_LRU
    o_kv = o_q + D_ATTN
    o_g = o_kv + 3 * KV_ROW
    pad_idx = _head_pad_index()
    wq = jnp.zeros((D_MODEL, Q_PAD), F32).at[:, pad_idx].set(wl[:, o_q:o_kv])
    hh = np.arange(3 * N_HEADS) // 3
    gate_idx = (hh // GROUP) * LANES + (hh % GROUP) * 3 + np.arange(3 * N_HEADS) % 3
    wgt = jnp.zeros((D_MODEL, N_KV_HEADS * LANES), F32).at[:, gate_idx].set(wl[:, o_g:])
    w_all = jnp.concatenate([wl[:, :o_q], wq, wl[:, o_kv:o_g], wgt], axis=1).astype(BF16)
    w_t = jnp.concatenate([wl[:, o_q:o_kv].astype(BF16), w_all[:, C_KVS + LANES:C_KVW],
                           w_all[:, C_KVW + LANES:C_GATE], w_all[:, C_GATE:]], axis=1).T

    conv = {"w": jnp.pad(conv_w[l], ((0, CONV_HALO - CONV_WIDTH), (0, 0))),
            "b": row(conv_b[l]), "ln_g": row(conv_ln_g[l]), "ln_b": row(conv_ln_b[l])}

    def blockdiag(w):
        out = jnp.zeros((D_LRU, D_LRU), F32)
        for n in range(LRU_BLOCKS):
            out = out.at[n * LRU_BW:(n + 1) * LRU_BW, n * LRU_BW:(n + 1) * LRU_BW].set(w[n])
        return out.astype(BF16)

    lru = {"cw": jnp.pad(lru_conv_w[l], ((0, LRU_HALO - LRU_CONV_WIDTH), (0, 0))),
           "cb": row(lru_conv_b[l]),
           "wr": blockdiag(lru_w_gate[l, 0]), "wi": blockdiag(lru_w_gate[l, 1]),
           "bg": lru_b_gate[l].astype(F32), "lam": row(lru_lambda[l])}

    pe = cmp_pe[l].reshape(2, 2, CMP_STRIDE * HEAD_DIM)
    pe_rows = jnp.stack([jnp.stack([pe[g // 2, j] for g in range(4)]) for j in range(2)])
    w2e = jnp.zeros((4, CMP_HIDDEN, KV_ROW), F32)
    for g in range(4):
        w2e = w2e.at[g, :, g * HEAD_DIM:(g + 1) * HEAD_DIM].set(cmp_w2[l, g // 2])
    cmp = {"pe": pe_rows.astype(F32),
           "w1": cmp_w1[l].reshape(2, 2, CMP_STRIDE * HEAD_DIM, CMP_HIDDEN).astype(BF16),
           "b1": cmp_b1[l].astype(F32),
           "w2": w2e.astype(BF16),
           "b2": jnp.concatenate([cmp_b2[l, 0], cmp_b2[l, 0], cmp_b2[l, 1], cmp_b2[l, 1]]).reshape(1, -1)}

    wo = w_out[l]
    wo_attn = jnp.zeros((Q_PAD, D_MODEL), F32).at[pad_idx, :].set(wo[D_CONV + D_LRU:])
    out = {"w": jnp.concatenate([wo[:D_CONV + D_LRU], wo_attn], axis=0).astype(BF16),
           "g": row(ln_g[l, 1]), "b": row(ln_b[l, 1])}
    return {"ffn": ffn, "w_all": w_all, "w_t": w_t, "conv": conv, "lru": lru, "cmp": cmp, "out": out}


def _pad_front(a, rows):
    return jnp.pad(a, ((0, 0), (rows - a.shape[1], 0), (0, 0)))


def _kv6(a, lead):
    return a.reshape(lead + (2, N_KV_HEADS, HEAD_DIM))


TM_PROMPT = 512
TC_PROMPT = 512
TL_PROMPT = 256


def _layer_prompt(x, bsz, t, lw, tabs):
    n = bsz * t
    x = _ffn_ln(x, lw["ffn"][0], TM_PROMPT)
    tabs_t = tuple(a.T for a in tabs)
    (u, lx, lg, kvc, ks, kw, kvc_t, kvs_t, kvw_t, qc_t, qr_t, vs_t, vw_t, gates_t) = _inproj_t(
        x, lw["w_all"], lw["w_t"], tabs, tabs_t, bsz, t, TM_PROMPT)
    s3 = lambda a: a.reshape(bsz, t, a.shape[-1])
    u3, lx3 = s3(u), s3(lx)
    yc = _conv_group(u3, jnp.zeros((bsz, CONV_HALO, D_CONV), F32), lw["conv"], TC_PROMPT)
    yl, h_last = _lru_group(lx3, s3(lg), jnp.zeros((bsz, LRU_HALO, D_LRU), F32),
                            jnp.zeros((bsz, LRU_HALO, D_LRU), F32), lw["lru"], TL_PROMPT, TL_PROMPT - 1)
    kc, vc_t = _compress_prompt(s3(kvc), lw["cmp"])
    ocmp_t, mneg = _cmp_topk_prompt(qc_t, kc, vc_t)
    ya_t = _selwin_prompt(qr_t, mneg, s3(ks), vs_t, s3(kw), vw_t, ocmp_t, gates_t)
    x = _outproj_ffn(x, yc.reshape(n, D_CONV), yl.reshape(n, D_LRU), ya_t, lw["out"], lw["ffn"][1], TM_PROMPT)
    leaf = lambda a: a.reshape(bsz, 2, N_KV_HEADS, HEAD_DIM, a.shape[-1]).transpose(0, 4, 1, 2, 3)
    state = (u3[:, t - (CONV_WIDTH - 1):], lx3[:, t - (LRU_CONV_WIDTH - 1):], h_last[:, 0],
             leaf(kvc_t), leaf(kvs_t), leaf(kvw_t[:, :, t - min(WINDOW, t):]))
    return x, state


T_PAD = 8


def _layer_sample(x, bsz, t, lw, tabs, conv_buf, lru_buf, lru_h, cache_cmp_t, cache_sel_t, win_t, layer, page_table):
    n = bsz * t
    x = _ffn_ln(x, lw["ffn"][0], n)
    (u, lx, lg, qc, qr, kvc, kvs, kvs_b, kvw, kvw_b, gates) = _inproj(x, lw["w_all"], tabs, n, 1)
    s3 = lambda a: a.reshape(bsz, t, a.shape[-1])
    padt = lambda a: jnp.pad(s3(a), ((0, 0), (0, T_PAD - t), (0, 0)))
    yc = _conv_group(padt(u), _pad_front(conv_buf, CONV_HALO), lw["conv"], T_PAD)[:, :t]
    h0 = jnp.broadcast_to(lru_h[:, None, :], (bsz, LRU_HALO, D_LRU))
    yl, h_all = _lru_group(padt(lx), padt(lg), _pad_front(lru_buf, LRU_HALO), h0, lw["lru"], T_PAD, t - 1)
    yl = yl[:, :t]

    def head_cols(a):
        a = a.reshape(bsz, t, N_HEADS, PAD_HEAD).transpose(0, 2, 1, 3).reshape(bsz, N_HEADS * t, PAD_HEAD)
        return jnp.pad(a, ((0, 0), (0, S_COLS - N_HEADS * t), (0, 0)))

    ocmp_cols, picks = _cmp_topk_sample(page_table, cache_cmp_t, layer, head_cols(qc), lw["cmp"], t)
    pk = picks[:, :, :N_HEADS * t].reshape(bsz, N_SEL, N_KV_HEADS, GROUP, t)[:, :, :, 0, :]
    idx_flat = pk.transpose(0, 3, 2, 1).reshape(-1).astype(I32)

    def head_rows(a, dt):
        a = a.reshape(bsz, N_KV_HEADS, GROUP, t, PAD_HEAD).transpose(0, 3, 1, 2, 4)
        return jnp.pad(a, ((0, 0), (0, 0), (0, 0), (0, HROWS - GROUP), (0, 0))).astype(dt)

    q_rows = head_rows(qr.reshape(bsz, t, N_HEADS, PAD_HEAD).transpose(0, 2, 1, 3).reshape(bsz, N_HEADS * t, PAD_HEAD), BF16)
    ocmp_rows = head_rows(ocmp_cols[:, :N_HEADS * t], F32)
    g3 = gates.reshape(bsz, t, N_KV_HEADS, LANES)[..., :3 * GROUP].reshape(bsz, t, N_KV_HEADS, GROUP, 3)
    gate_rows = jnp.pad(g3, ((0, 0), (0, 0), (0, 0), (0, HROWS - GROUP), (0, LANES - 3)))
    pos_last = lambda a: s3(a).transpose(0, 2, 1)
    pad_cols = lambda a: jnp.pad(a, ((0, 0), (0, 0), (0, LANES - t)))
    kvs_t, kvw_t = pos_last(kvs), pos_last(kvw)
    ya_rows = _selwin_sample(idx_flat, page_table, cache_sel_t, win_t, layer, q_rows, pad_cols(kvs_t),
                             pad_cols(kvw_t), ocmp_rows, gate_rows)
    ya = ya_rows[:, :, :, :GROUP].reshape(bsz, t, Q_PAD)

    x = _outproj_ffn(x, yc.reshape(n, D_CONV), yl.reshape(n, D_LRU), ya.reshape(n, Q_PAD), lw["out"], lw["ffn"][1], n)
    new_conv = jnp.concatenate([conv_buf, s3(u)], axis=1)[:, t:]
    new_lru = jnp.concatenate([lru_buf, s3(lx)], axis=1)[:, t:]
    win_all = jnp.concatenate([win_t[layer], kvw_t], axis=-1)
    n_win = min(WINDOW, win_all.shape[-1])
    new_win = win_all[..., win_all.shape[-1] - n_win:]
    new_win = new_win.reshape(bsz, 2, N_KV_HEADS, HEAD_DIM, n_win).transpose(0, 4, 1, 2, 3)
    state = (new_conv, new_lru, h_all[:, 0], _kv6(kvc, (bsz, t)), _kv6(kvs, (bsz, t)), new_win)
    return x, state


def kernel(x_prompt, x_sample, state_conv, state_lru_conv, state_lru_h, cache_cmp_kv, cache_sel_kv, cache_win_kv,
           page_table, ln_g, ln_b, ffn_w_in, ffn_w_out, w_in, conv_w, conv_b, conv_ln_g, conv_ln_b,
           lru_conv_w, lru_conv_b, lru_w_gate, lru_b_gate, lru_lambda, cmp_pe, cmp_w1, cmp_b1, cmp_w2, cmp_b2, w_out):
    bp, tp, _ = x_prompt.shape
    bs, ts, _ = x_sample.shape
    depth = ln_g.shape[0]
    past = page_table.shape[1] * PAGE_SIZE
    assert past == PAST_LEN and past % SEL_BLOCK == 0
    tabs_p = _rope_tables(jnp.arange(tp))
    tabs_s = _rope_tables(jnp.tile(past + jnp.arange(ts), bs))
    xp = x_prompt.reshape(bp * tp, D_MODEL)
    xs = x_sample.reshape(bs * ts, D_MODEL)
    n_pool = cache_cmp_kv.shape[1]
    pos_last = lambda c: jnp.transpose(c, (0, 1, 3, 4, 5, 2)).reshape(c.shape[:2] + (KV_ROW, c.shape[2]))
    cache_cmp_t, cache_sel_t, win_t = pos_last(cache_cmp_kv), pos_last(cache_sel_kv), pos_last(cache_win_kv)
    st_p, st_s = [], []
    for l in range(depth):
        lw = _prep_layer(l, ln_g, ln_b, ffn_w_in, ffn_w_out, w_in, conv_w, conv_b, conv_ln_g, conv_ln_b,
                         lru_conv_w, lru_conv_b, lru_w_gate, lru_b_gate, lru_lambda,
                         cmp_pe, cmp_w1, cmp_b1, cmp_w2, cmp_b2, w_out)
        xp, sp = _layer_prompt(xp, bp, tp, lw, tabs_p)
        xs, ss = _layer_sample(
            xs, bs, ts, lw, tabs_s, state_conv[l], state_lru_conv[l], state_lru_h[l],
            cache_cmp_t, cache_sel_t, win_t, l, page_table)
        st_p.append(sp)
        st_s.append(ss)
    outs = [xp.reshape(bp, tp, D_MODEL), xs.reshape(bs, ts, D_MODEL)]
    for k in range(6):
        outs.append(jnp.stack([s[k] for s in st_p]))
        outs.append(jnp.stack([s[k] for s in st_s]))
    return tuple(outs)
```

```python
import functools

import numpy as np
import jax
import jax.numpy as jnp
from jax import lax
from jax.experimental import pallas as pl
from jax.experimental.pallas import tpu as pltpu

F32 = jnp.float32
BF16 = jnp.bfloat16
I32 = jnp.int32

D_MODEL = 1024
DEPTH = 2
PAST_LEN = 16384
PAGE_SIZE = 128
D_CONV = 256
CONV_WIDTH = 31
D_LRU = 256
LRU_BLOCKS = 4
LRU_BW = D_LRU // LRU_BLOCKS
LRU_CONV_WIDTH = 4
LRU_C = 8.0
D_ATTN = 512
N_HEADS = 8
HEAD_DIM = 64
N_KV_HEADS = 2
GROUP = N_HEADS // N_KV_HEADS
KV_ROW = 2 * N_KV_HEADS * HEAD_DIM
ROPE_DIM = 16
ROPE_THETA = 500000.0
CMP_STRIDE = 16
CMP_HIDDEN = 256
SEL_BLOCK = 64
N_SEL = 16
WINDOW = 512
Q_BLOCK = 128
D_FF = 2816
ALPHA = (2 * DEPTH) ** 0.25
LN_EPS = 1e-5
SCALE = HEAD_DIM ** -0.5

LANES = 128
SUBLANES = 8
VMEM_LIMIT = 56 * 1024 * 1024
FF_CHUNK = 256
N_FF_CHUNKS = D_FF // FF_CHUNK
PAD_HEAD = 128
Q_PAD = N_HEADS * PAD_HEAD
MASK_NEG = -(2.0 ** 60)
SEL_TILE = 512
SELWIN_QUERIES = 256

C_GLU = 0
C_LRUX = 512
C_LRUG = 768
C_Q = 1024
C_KVC = C_Q + Q_PAD
C_KVS = C_KVC + KV_ROW
C_KVW = C_KVS + KV_ROW
C_GATE = C_KVW + KV_ROW
N_PROJ = C_GATE + N_KV_HEADS * LANES


def _cparams(sem):
    return pltpu.CompilerParams(dimension_semantics=sem, vmem_limit_bytes=VMEM_LIMIT)


def _const_spec(shape):
    nd = len(shape)
    return pl.BlockSpec(shape, lambda *_: (0,) * nd, pipeline_mode=pl.Buffered(1))


def _layernorm(y, g, b):
    mu = jnp.mean(y, axis=-1, keepdims=True)
    d = y - mu
    var = jnp.mean(d * d, axis=-1, keepdims=True)
    return d * lax.rsqrt(var + LN_EPS) * g + b


def _dot(a, b):
    return jnp.dot(a, b, preferred_element_type=F32)


def _dot_nt(a, b):
    return lax.dot_general(a, b, (((1,), (1,)), ((), ())), preferred_element_type=F32)


def _dot_tn(a, b):
    return lax.dot_general(a, b, (((0,), (0,)), ((), ())), preferred_element_type=F32)


def _ffn_ln_apply(x, wg_ref, wu_ref, wo_ref, g_ref, b_ref):
    xb = x.astype(BF16)
    acc = jnp.zeros(x.shape, F32)
    for c in range(N_FF_CHUNKS):
        gate = _dot(xb, wg_ref[c])
        up = _dot(xb, wu_ref[c])
        h = (gate * jax.nn.sigmoid(gate)) * up
        acc = acc + _dot(h.astype(BF16), wo_ref[c])
    y = ALPHA * x + 0.5 * acc
    return _layernorm(y, g_ref[...], b_ref[...])


def _ffn_ln_kernel(x_ref, wg_ref, wu_ref, wo_ref, g_ref, b_ref, o_ref):
    o_ref[...] = _ffn_ln_apply(x_ref[...], wg_ref, wu_ref, wo_ref, g_ref, b_ref)


def _ffn_weight_specs():
    return [
        _const_spec((N_FF_CHUNKS, D_MODEL, FF_CHUNK)),
        _const_spec((N_FF_CHUNKS, D_MODEL, FF_CHUNK)),
        _const_spec((N_FF_CHUNKS, FF_CHUNK, D_MODEL)),
        _const_spec((1, D_MODEL)),
        _const_spec((1, D_MODEL)),
    ]


def _ffn_weight_args(fw):
    return (fw["wg"], fw["wu"], fw["wo"], fw["g"], fw["b"])


def _ffn_ln(x, fw, tm):
    n = x.shape[0]
    return pl.pallas_call(
        _ffn_ln_kernel,
        grid=(n // tm,),
        in_specs=[pl.BlockSpec((tm, D_MODEL), lambda i: (i, 0))] + _ffn_weight_specs(),
        out_specs=pl.BlockSpec((tm, D_MODEL), lambda i: (i, 0)),
        out_shape=jax.ShapeDtypeStruct((n, D_MODEL), F32),
        compiler_params=_cparams(("parallel",)),
        name="ffn_ln",
    )(x, *_ffn_weight_args(fw))


def _rope(v, cos, s1, s2):
    return v * cos + pltpu.roll(v, 8, 1) * s1 + pltpu.roll(v, LANES - 8, 1) * s2


def _inproj_kernel(x_ref, w_ref, cos_ref, s1_ref, s2_ref,
                   u_ref, lx_ref, lg_ref, qc_ref, qr_ref, kvc_ref, kvs_ref, kvw_ref, gt_ref):
    xb = x_ref[...].astype(BF16)

    def mm(lo, hi):
        return _dot(xb, w_ref[:, lo:hi])

    glu = mm(C_GLU, C_GLU + 2 * D_CONV)
    u_ref[...] = glu[:, :D_CONV] * jax.nn.sigmoid(glu[:, D_CONV:])
    lx_ref[...] = mm(C_LRUX, C_LRUX + D_LRU)
    lg_ref[...] = mm(C_LRUG, C_LRUG + D_LRU)
    cos = cos_ref[...]
    s1 = s1_ref[...]
    s2 = s2_ref[...]
    for h in range(N_HEADS):
        qh = mm(C_Q + h * PAD_HEAD, C_Q + (h + 1) * PAD_HEAD) * SCALE
        qc_ref[:, h * PAD_HEAD:(h + 1) * PAD_HEAD] = qh.astype(BF16)
        qr_ref[:, h * PAD_HEAD:(h + 1) * PAD_HEAD] = _rope(qh, cos, s1, s2).astype(BF16)
    kvc_ref[...] = mm(C_KVC, C_KVC + KV_ROW)
    for c0, f_ref in ((C_KVS, kvs_ref), (C_KVW, kvw_ref)):
        kv = mm(c0, c0 + KV_ROW)
        f_ref[:, 0:LANES] = _rope(kv[:, :LANES], cos, s1, s2)
        f_ref[:, LANES:KV_ROW] = kv[:, LANES:]
    gt_ref[...] = jax.nn.sigmoid(mm(C_GATE, N_PROJ))


def _rope_t(v, cos, s1, s2):
    return v * cos + pltpu.roll(v, 8, 0) * s1 + pltpu.roll(v, v.shape[0] - 8, 0) * s2


R_Q = 0
R_VS = D_ATTN
R_VW = R_VS + LANES
R_GATE = R_VW + LANES
N_PROJ_T = R_GATE + N_KV_HEADS * LANES


LOG2E = 1.4426950408889634
V_ROWS = HEAD_DIM + 16


def _inproj_t_kernel(x_ref, w_ref, wt_ref, cos_ref, s1_ref, s2_ref, cos_t_ref, s1_t_ref, s2_t_ref, blk_ref,
                     u_ref, lx_ref, lg_ref, kvc_ref, ks_ref, kw_ref,
                     kvct_ref, kvst_ref, kvwt_ref, qc_ref, qr_ref, vs_ref, vw_ref, gt_ref):
    xb = x_ref[...].astype(BF16)
    tm = xb.shape[0]

    def mm(lo, hi):
        return _dot(xb, w_ref[:, lo:hi])

    glu = mm(C_GLU, C_GLU + 2 * D_CONV)
    u_ref[...] = glu[:, :D_CONV] * jax.nn.sigmoid(glu[:, D_CONV:])
    lx_ref[...] = mm(C_LRUX, C_LRUX + D_LRU)
    lg_ref[...] = mm(C_LRUG, C_LRUG + D_LRU)
    kvc = mm(C_KVC, C_KVC + KV_ROW)
    kvc_ref[...] = kvc
    kvct_ref[...] = kvc.T
    cos = cos_ref[...]
    s1 = s1_ref[...]
    s2 = s2_ref[...]
    for c0, b_ref, leaf_ref in ((C_KVS, ks_ref, kvst_ref), (C_KVW, kw_ref, kvwt_ref)):
        k = _rope(mm(c0, c0 + LANES), cos, s1, s2)
        b_ref[:, 0:LANES] = k.astype(BF16)
        leaf_ref[0:LANES, :] = k.T
    ks_ref[:, LANES:] = blk_ref[...]

    all_t = _dot_nt(wt_ref[...], xb)
    cos_t = cos_t_ref[0:HEAD_DIM, :]
    s1_t = s1_t_ref[0:HEAD_DIM, :]
    s2_t = s2_t_ref[0:HEAD_DIM, :]
    no_rows = jnp.zeros((HEAD_DIM, tm), BF16)
    for h in range(N_HEADS):
        qh = all_t[R_Q + h * HEAD_DIM:R_Q + (h + 1) * HEAD_DIM, :]
        own = h * PAD_HEAD + (h // GROUP) * HEAD_DIM
        other = h * PAD_HEAD + (1 - h // GROUP) * HEAD_DIM
        qc_ref[own:own + HEAD_DIM, :] = (qh * SCALE).astype(BF16)
        qr_ref[own:own + HEAD_DIM, :] = (_rope_t(qh, cos_t, s1_t, s2_t) * (SCALE * LOG2E)).astype(BF16)
        qc_ref[other:other + HEAD_DIM, :] = no_rows
        qr_ref[other:other + HEAD_DIM, :] = no_rows
    sum_rows = jnp.where(lax.broadcasted_iota(I32, (V_ROWS - HEAD_DIM, tm), 0) == 0, 1.0, 0.0)
    for r0, leaf_ref, o_ref in ((R_VS, kvst_ref, vs_ref), (R_VW, kvwt_ref, vw_ref)):
        vt = all_t[r0:r0 + LANES, :]
        leaf_ref[LANES:KV_ROW, :] = vt
        for kv in range(N_KV_HEADS):
            vk = jnp.concatenate([vt[kv * HEAD_DIM:(kv + 1) * HEAD_DIM, :], sum_rows], axis=0).astype(BF16)
            for c in range(tm // LANES):
                o_ref[kv, c] = vk[:, c * LANES:(c + 1) * LANES]
    gt_ref[...] = jax.nn.sigmoid(all_t[R_GATE:N_PROJ_T, :])


def _inproj_t(x, w_all, w_t, rope_tabs, rope_tabs_t, bsz, t, tm):
    n = x.shape[0]
    tps = t // tm
    tok = lambda w: pl.BlockSpec((tm, w), lambda i: (i, 0))
    tab = pl.BlockSpec((tm, LANES), lambda i: (i % tps, 0))
    tab_t = pl.BlockSpec((PAD_HEAD, tm), lambda i: (0, i % tps))
    feat_t = lambda r: pl.BlockSpec((None, r, tm), lambda i: (i // tps, 0, i % tps))
    vt_spec = pl.BlockSpec((None, N_KV_HEADS, tm // LANES, V_ROWS, LANES), lambda i: (i // tps, 0, i % tps, 0, 0))
    n_sel = t // SEL_BLOCK
    row_outs = [(D_CONV, F32), (D_LRU, F32), (D_LRU, F32), (KV_ROW, F32), (LANES + n_sel, BF16), (LANES, BF16)]
    leaf_t = jax.ShapeDtypeStruct((bsz, KV_ROW, t), F32)
    vt_shape = jax.ShapeDtypeStruct((bsz, N_KV_HEADS, t // LANES, V_ROWS, LANES), BF16)
    block_id = (jnp.arange(t)[:, None] // SEL_BLOCK == jnp.arange(n_sel)[None, :]).astype(BF16)
    return pl.pallas_call(
        _inproj_t_kernel,
        grid=(n // tm,),
        in_specs=[tok(D_MODEL), _const_spec((D_MODEL, N_PROJ)), _const_spec((N_PROJ_T, D_MODEL)),
                  tab, tab, tab, tab_t, tab_t, tab_t,
                  pl.BlockSpec((tm, n_sel), lambda i: (i % tps, 0))],
        out_specs=[tok(w) for w, _ in row_outs]
        + [feat_t(KV_ROW)] * 3
        + [feat_t(Q_PAD), feat_t(Q_PAD), vt_spec, vt_spec, feat_t(N_KV_HEADS * LANES)],
        out_shape=[jax.ShapeDtypeStruct((n, w), dt) for w, dt in row_outs]
        + [leaf_t] * 3
        + [jax.ShapeDtypeStruct((bsz, Q_PAD, t), BF16), jax.ShapeDtypeStruct((bsz, Q_PAD, t), BF16),
           vt_shape, vt_shape, jax.ShapeDtypeStruct((bsz, N_KV_HEADS * LANES, t), F32)],
        compiler_params=_cparams(("parallel",)),
        name="inproj_t",
    )(x, w_all, w_t, *rope_tabs, *rope_tabs_t, block_id)


def _inproj(x, w_all, rope_tabs, tm, tiles_per_seq):
    n = x.shape[0]
    cos, s1, s2 = rope_tabs
    tok = lambda w: pl.BlockSpec((tm, w), lambda i: (i, 0))
    tab = pl.BlockSpec((tm, LANES), lambda i: (i % tiles_per_seq, 0))
    outs = [
        (D_CONV, F32), (D_LRU, F32), (D_LRU, F32), (Q_PAD, BF16), (Q_PAD, BF16),
        (KV_ROW, F32), (KV_ROW, F32), (KV_ROW, F32), (N_KV_HEADS * LANES, F32),
    ]
    return pl.pallas_call(
        _inproj_kernel,
        grid=(n // tm,),
        in_specs=[tok(D_MODEL), _const_spec((D_MODEL, N_PROJ)), tab, tab, tab],
        out_specs=[tok(w) for w, _ in outs],
        out_shape=[jax.ShapeDtypeStruct((n, w), dt) for w, dt in outs],
        compiler_params=_cparams(("parallel",)),
        name="inproj",
    )(x, w_all, cos, s1, s2)


CONV_HALO = 32
CONV_ROWS = 64


def _conv_kernel(u_ref, buf_ref, w_ref, b_ref, g_ref, bb_ref, o_ref, ext_ref, sh_ref):
    t = pl.program_id(1)
    tc = u_ref.shape[0]

    @pl.when(t == 0)
    def _():
        ext_ref[0:CONV_HALO, :] = buf_ref[...]

    ext_ref[CONV_HALO:CONV_HALO + tc, :] = u_ref[...]
    off = CONV_HALO - (CONV_WIDTH - 1)
    span = tc + CONV_HALO - SUBLANES
    for s in range(1, SUBLANES):
        sh_ref[s - 1, 0:span, :] = ext_ref[s:s + span, :]
    rows = min(CONV_ROWS, tc)
    for r0 in range(0, tc, rows):
        acc = jnp.zeros((rows, D_CONV), F32)
        for k in range(CONV_WIDTH):
            a, s = divmod(off + k, SUBLANES)
            src = ext_ref if s == 0 else sh_ref.at[s - 1]
            acc = acc + src[r0 + SUBLANES * a:r0 + SUBLANES * a + rows, :] * w_ref[k:k + 1, :]
        y = acc + b_ref[...]
        y = _layernorm(y, g_ref[...], bb_ref[...])
        o_ref[r0:r0 + rows, :] = (y * jax.nn.sigmoid(y)).astype(BF16)
    ext_ref[0:CONV_HALO, :] = ext_ref[tc:tc + CONV_HALO, :]


def _conv_group(u, buf, cw, tc):
    b, t, _ = u.shape
    return pl.pallas_call(
        _conv_kernel,
        grid=(b, t // tc),
        in_specs=[
            pl.BlockSpec((None, tc, D_CONV), lambda i, j: (i, j, 0)),
            pl.BlockSpec((None, CONV_HALO, D_CONV), lambda i, j: (i, 0, 0)),
            _const_spec((CONV_HALO, D_CONV)),
            _const_spec((1, D_CONV)), _const_spec((1, D_CONV)), _const_spec((1, D_CONV)),
        ],
        out_specs=pl.BlockSpec((None, tc, D_CONV), lambda i, j: (i, j, 0)),
        out_shape=jax.ShapeDtypeStruct((b, t, D_CONV), BF16),
        scratch_shapes=[pltpu.VMEM((tc + CONV_HALO, D_CONV), F32),
                        pltpu.VMEM((SUBLANES - 1, tc + CONV_HALO - SUBLANES, D_CONV), F32)],
        compiler_params=_cparams(("parallel", "arbitrary")),
        name="conv_group",
    )(u, buf, cw["w"], cw["b"], cw["ln_g"], cw["ln_b"])


LRU_HALO = 8


def _lru_kernel(last_row, x_ref, gate_ref, buf_ref, h0_ref, cw_ref, cb_ref, wr_ref, wi_ref,
                bg_ref, lam_ref, y_ref, hl_ref, ext_ref, hc_ref):
    t = pl.program_id(1)
    tl = x_ref.shape[0]

    @pl.when(t == 0)
    def _():
        ext_ref[0:LRU_HALO, :] = buf_ref[...]
        hc_ref[...] = h0_ref[...]

    ext_ref[LRU_HALO:LRU_HALO + tl, :] = x_ref[...]
    off = LRU_HALO - (LRU_CONV_WIDTH - 1)
    xl = jnp.zeros((tl, D_LRU), F32)
    for k in range(LRU_CONV_WIDTH):
        xl = xl + ext_ref[off + k:off + k + tl, :] * cw_ref[k:k + 1, :]
    xl = xl + cb_ref[...]
    ext_ref[0:LRU_HALO, :] = ext_ref[tl:tl + LRU_HALO, :]

    xb = xl.astype(BF16)
    r_gate = jax.nn.sigmoid(_dot(xb, wr_ref[...]) + bg_ref[0:1, :])
    i_gate = jax.nn.sigmoid(_dot(xb, wi_ref[...]) + bg_ref[1:2, :])
    log_a = LRU_C * r_gate * jax.nn.log_sigmoid(lam_ref[...])
    a = jnp.exp(log_a)
    bv = jnp.sqrt(-jnp.tanh(log_a) * (a * a + 1.0)) * (i_gate * xl)

    row = lax.broadcasted_iota(I32, (tl, D_LRU), 0)
    s = 1
    while s < tl:
        keep = row >= s
        a_sh = jnp.where(keep, pltpu.roll(a, s, 0), 1.0)
        b_sh = jnp.where(keep, pltpu.roll(bv, s, 0), 0.0)
        bv = a * b_sh + bv
        a = a * a_sh
        s *= 2
    h = a * hc_ref[0:1, :] + bv
    hc_ref[...] = jnp.broadcast_to(h[last_row:last_row + 1, :], hc_ref.shape)
    hl_ref[...] = hc_ref[...]
    y_ref[...] = (h * jax.nn.gelu(gate_ref[...])).astype(BF16)


def _lru_group(x, gate, buf, h0, lw, tl, last_row):
    b, t, _ = x.shape
    assert last_row == tl - 1 or t == tl
    seq = pl.BlockSpec((None, tl, D_LRU), lambda i, j: (i, j, 0))
    per_b = pl.BlockSpec((None, LRU_HALO, D_LRU), lambda i, j: (i, 0, 0))
    return pl.pallas_call(
        functools.partial(_lru_kernel, last_row),
        grid=(b, t // tl),
        in_specs=[seq, seq, per_b, per_b,
                  _const_spec((LRU_HALO, D_LRU)), _const_spec((1, D_LRU)),
                  _const_spec((D_LRU, D_LRU)), _const_spec((D_LRU, D_LRU)),
                  _const_spec((2, D_LRU)), _const_spec((1, D_LRU))],
        out_specs=[seq, per_b],
        out_shape=[jax.ShapeDtypeStruct((b, t, D_LRU), BF16),
                   jax.ShapeDtypeStruct((b, LRU_HALO, D_LRU), F32)],
        scratch_shapes=[pltpu.VMEM((tl + LRU_HALO, D_LRU), F32),
                        pltpu.VMEM((LRU_HALO, D_LRU), F32)],
        compiler_params=_cparams(("parallel", "arbitrary")),
        name="lru_group",
    )(x, gate, buf, h0, lw["cw"], lw["cb"], lw["wr"], lw["wi"], lw["bg"], lw["lam"])


def _compress_rows(row_refs, pe_ref, w1_ref, b1_ref, w2_ref, b2_ref, carry_ref):
    xs = _chunk_vectors(row_refs, pe_ref)
    return _compress_mlp(lambda j, g: xs[j][g], w1_ref, b1_ref, w2_ref, b2_ref, carry_ref)


def _chunk_vectors(row_refs, pe_ref):
    lo = lax.broadcasted_iota(I32, (1, LANES), 1) < HEAD_DIM
    xs = [[] for _ in range(4)]
    for halves in row_refs:
        n = halves[0].shape[0] // CMP_STRIDE
        cols = [[] for _ in range(4)]
        for rp in range(CMP_STRIDE // 2):
            for half, ref in enumerate(halves):
                pa = ref[pl.ds(2 * rp, n, stride=CMP_STRIDE), :]
                pb = ref[pl.ds(2 * rp + 1, n, stride=CMP_STRIDE), :]
                cols[2 * half].append(jnp.where(lo, pa, pltpu.roll(pb, HEAD_DIM, 1)))
                cols[2 * half + 1].append(jnp.where(lo, pltpu.roll(pa, HEAD_DIM, 1), pb))
        for g in range(4):
            xs[g].append(jnp.concatenate(cols[g], axis=1))
    out = [[], []]
    for g in range(4):
        x = jnp.concatenate(xs[g], axis=0)
        for j in range(2):
            out[j].append((x + pe_ref[j, g:g + 1, :]).astype(BF16))
    return out


def _compress_mlp(get_x, w1_ref, b1_ref, w2_ref, b2_ref, carry_ref):
    out = None
    for g in range(4):
        sidx = g // 2
        p0 = _dot(get_x(0, g), w1_ref[sidx, 0])
        p1 = _dot(get_x(1, g), w1_ref[sidx, 1])
        n_tot = p0.shape[0]
        row = lax.broadcasted_iota(I32, p0.shape, 0)
        p0s = jnp.where(row == 0, carry_ref[g, 0:1, :], pltpu.roll(p0, 1, 0))
        carry_ref[g, 0:1, :] = p0[n_tot - 1:n_tot, :]
        h = (b1_ref[sidx:sidx + 1, :] + p0s) + p1
        part = _dot(jax.nn.gelu(h).astype(BF16), w2_ref[g])
        out = part if out is None else out + part
    return out + b2_ref[...]


def _cmp_weight_specs():
    return [
        _const_spec((2, 4, CMP_STRIDE * HEAD_DIM)),
        _const_spec((2, 2, CMP_STRIDE * HEAD_DIM, CMP_HIDDEN)),
        _const_spec((2, CMP_HIDDEN)),
        _const_spec((4, CMP_HIDDEN, KV_ROW)),
        _const_spec((1, KV_ROW)),
    ]


def _cmp_weight_args(cw):
    return (cw["pe"], cw["w1"], cw["b1"], cw["w2"], cw["b2"])


CMP_TILE_ROWS = 2048


def _compress_prompt_kernel(k_ref, v_ref, pe_ref, w1_ref, b1_ref, w2_ref, b2_ref, kc_ref, vct_ref, carry_ref):
    @pl.when(pl.program_id(1) == 0)
    def _():
        carry_ref[...] = jnp.zeros(carry_ref.shape, F32)

    out = _compress_rows([(k_ref, v_ref)], pe_ref, w1_ref, b1_ref, w2_ref, b2_ref, carry_ref)
    kc_ref[...] = out[:, 0:LANES].astype(BF16)
    vct_ref[...] = out[:, LANES:KV_ROW].T.astype(BF16)


def _compress_prompt(kvc, cw):
    b, t, _ = kvc.shape
    n_e = CMP_TILE_ROWS // CMP_STRIDE
    return pl.pallas_call(
        _compress_prompt_kernel,
        grid=(b, t // CMP_TILE_ROWS),
        in_specs=[pl.BlockSpec((None, CMP_TILE_ROWS, LANES), lambda i, j: (i, j, 0)),
                  pl.BlockSpec((None, CMP_TILE_ROWS, LANES), lambda i, j: (i, j, 1))]
        + _cmp_weight_specs(),
        out_specs=[pl.BlockSpec((None, n_e, LANES), lambda i, j: (i, j, 0)),
                   pl.BlockSpec((None, LANES, n_e), lambda i, j: (i, 0, j))],
        out_shape=[jax.ShapeDtypeStruct((b, t // CMP_STRIDE, LANES), BF16),
                   jax.ShapeDtypeStruct((b, LANES, t // CMP_STRIDE), BF16)],
        scratch_shapes=[pltpu.VMEM((4, 8, CMP_HIDDEN), F32)],
        compiler_params=_cparams(("parallel", "arbitrary")),
        name="compress_prompt",
    )(kvc, kvc, *_cmp_weight_args(cw))


def _selection_scores(pk_ref, n_sel):
    ratio = SEL_BLOCK // CMP_STRIDE
    slc = pk_ref[pl.ds(0, n_sel, stride=ratio), :]
    for o in range(1, ratio):
        slc = slc + 2.0 * pk_ref[pl.ds(o, n_sel, stride=ratio), :]
    return slc + pk_ref[pl.ds(ratio, n_sel, stride=ratio), :]


def _topk_rounds(score, n_rows, rounds=N_SEL):
    j = lax.broadcasted_iota(I32, score.shape, 0).astype(F32)
    sel = jnp.zeros(score.shape, F32)
    picks = []
    for _ in range(rounds):
        cm = jnp.max(score, axis=0, keepdims=True)
        mi = jnp.min(jnp.where(score == cm, j, float(n_rows)), axis=0, keepdims=True)
        hit = j == mi
        sel = jnp.where(hit, 1.0, sel)
        score = jnp.where(hit, -jnp.inf, score)
        picks.append(mi)
    return sel, picks


def _masked_softmax_rows(s_t, valid):
    s_t = jnp.where(valid, s_t, -jnp.inf)
    m = jnp.max(s_t, axis=0, keepdims=True)
    m = jnp.where(m > -jnp.inf, m, 0.0)
    e = jnp.exp(s_t - m)
    d = jnp.sum(e, axis=0, keepdims=True)
    return e / jnp.where(d > 0, d, 1.0)


CMP_ENTRY_CHUNK = 128
N_FORCED = 3


def _cmp_topk_prompt_kernel(qc_ref, kc_ref, vct_ref, ocmp_ref, mneg_ref, pk0_ref, pk1_ref):
    i = pl.program_id(1)
    n_e = kc_ref.shape[0]
    n_sel = n_e // (SEL_BLOCK // CMP_STRIDE)
    pk_refs = (pk0_ref, pk1_ref)

    def attend(n_use):
        kc = kc_ref[0:n_use, :]
        vct = vct_ref[:, 0:n_use]
        e_idx = lax.broadcasted_iota(I32, (n_use, Q_BLOCK), 0)
        qpos = i * Q_BLOCK + lax.broadcasted_iota(I32, (n_use, Q_BLOCK), 1)
        valid = (e_idx >= 1) & (CMP_STRIDE * e_idx + (CMP_STRIDE - 1) <= qpos)
        for kv in range(N_KV_HEADS):
            pkv = jnp.zeros((n_use, Q_BLOCK), F32)
            for h in range(kv * GROUP, (kv + 1) * GROUP):
                rows = slice(h * PAD_HEAD, (h + 1) * PAD_HEAD)
                p = _masked_softmax_rows(_dot(kc, qc_ref[rows, :]), valid)
                pkv = pkv + p
                ocmp_ref[rows, :] = _dot(vct, p.astype(BF16))
            pk_refs[kv][0:n_use, :] = pkv
            pk_refs[kv][n_use:, :] = jnp.zeros((n_e + 8 - n_use, Q_BLOCK), F32)

    n_var = n_e // CMP_ENTRY_CHUNK
    variant = jnp.minimum((8 * i + 7) // CMP_ENTRY_CHUNK, n_var - 1)
    for k in range(n_var):
        pl.when(variant == k)(functools.partial(attend, CMP_ENTRY_CHUNK * (k + 1)))

    j = lax.broadcasted_iota(I32, (n_sel, Q_BLOCK), 0)
    qp = i * Q_BLOCK + lax.broadcasted_iota(I32, (n_sel, Q_BLOCK), 1)
    qblk = jnp.right_shift(qp, 6)
    forced = (j == 0) | (j == qblk) | (j == qblk - 1)
    others = [jnp.where(forced | (j * SEL_BLOCK > qp), -jnp.inf, _selection_scores(r, n_sel)) for r in pk_refs]
    sel, _ = _topk_rounds(jnp.concatenate(others, axis=1), n_sel, N_SEL - N_FORCED)
    chosen = (sel > 0) | jnp.concatenate([forced] * N_KV_HEADS, axis=1)
    mneg = jnp.where(chosen, 0.0, MASK_NEG).astype(BF16)
    for kv in range(N_KV_HEADS):
        mneg_ref[kv] = mneg[:, kv * Q_BLOCK:(kv + 1) * Q_BLOCK]


def _cmp_topk_prompt(qc_t, kc, vc_t):
    b, _, t = qc_t.shape
    n_e = kc.shape[1]
    n_sel = t // SEL_BLOCK
    assert n_e % CMP_ENTRY_CHUNK == 0
    qblk = pl.BlockSpec((None, Q_PAD, Q_BLOCK), lambda bi, i: (bi, 0, i))
    return pl.pallas_call(
        _cmp_topk_prompt_kernel,
        grid=(b, t // Q_BLOCK),
        in_specs=[
            qblk,
            pl.BlockSpec((None, n_e, LANES), lambda bi, i: (bi, 0, 0)),
            pl.BlockSpec((None, LANES, n_e), lambda bi, i: (bi, 0, 0)),
        ],
        out_specs=[
            qblk,
            pl.BlockSpec((None, N_KV_HEADS, n_sel, Q_BLOCK), lambda bi, i: (bi, 0, 0, i)),
        ],
        out_shape=[jax.ShapeDtypeStruct((b, Q_PAD, t), F32),
                   jax.ShapeDtypeStruct((b, N_KV_HEADS, n_sel, t), BF16)],
        scratch_shapes=[pltpu.VMEM((n_e + 8, Q_BLOCK), F32), pltpu.VMEM((n_e + 8, Q_BLOCK), F32)],
        compiler_params=_cparams(("parallel", "arbitrary")),
        name="cmp_topk_prompt",
    )(qc_t, kc, vc_t)


def _selwin_prompt_kernel(qr_ref, mneg_ref, ks_ref, vs_ref, kw_ref, vw_ref, ocmp_ref, gt_ref, o_ref,
                          qa_ref, sa_ref, sb_ref, pa_ref, pb_ref):
    i = pl.program_id(1)
    n_sel = mneg_ref.shape[1]
    qb = qr_ref.shape[1]
    win_keys = WINDOW + qb
    cols = N_HEADS * qb
    kv_cols = GROUP * qb
    for h in range(N_HEADS):
        c = slice(h * qb, (h + 1) * qb)
        qa_ref[0:PAD_HEAD, c] = qr_ref[h * PAD_HEAD:(h + 1) * PAD_HEAD, :]
        qa_ref[PAD_HEAD:PAD_HEAD + n_sel, c] = mneg_ref[h // GROUP]
    qa = qa_ref[...]
    q0 = i * qb
    qpos = q0 + (lax.broadcasted_iota(I32, (1, cols), 1) & (qb - 1))

    tile = SEL_TILE
    key_col = lax.broadcasted_iota(I32, (tile, 1), 0)

    def scores(t):
        return _dot(ks_ref[pl.ds(pl.multiple_of(t * tile, tile), tile), :], qa)

    def weighted_values(v_ref, first_tile, n_tiles, p):
        outs = []
        for kv in range(N_KV_HEADS):
            vk = v_ref[kv, pl.ds(first_tile, n_tiles)]
            vt = jnp.concatenate([vk[c] for c in range(n_tiles)], axis=1)
            outs.append(_dot(vt, p[:, kv * kv_cols:(kv + 1) * kv_cols]))
        return jnp.concatenate(outs, axis=1)

    def normalise(acc):
        return acc[0:HEAD_DIM, :] / acc[HEAD_DIM:HEAD_DIM + 1, :]

    start = pl.multiple_of(jnp.maximum(q0 - WINDOW, 0), LANES)
    s = _dot(kw_ref[pl.ds(start, win_keys), :], qa[0:PAD_HEAD, :])
    dpos = qpos - (start + lax.broadcasted_iota(I32, (win_keys, 1), 0))
    s = jnp.where(lax.bitcast_convert_type(dpos, jnp.uint32) <= jnp.uint32(WINDOW), s, -jnp.inf)
    p = jnp.exp2(s - jnp.max(s, axis=0, keepdims=True))
    o_win = normalise(weighted_values(vw_ref, start // LANES, win_keys // LANES, p.astype(BF16)))

    sub = tile // LANES

    def stage(t, cur, nxt, carry, masked, prefetch):
        m, acc, alpha_prev = carry
        if prefetch:
            s_refs[nxt][...] = scores(t + 1)
        acc = alpha_prev * acc + weighted_values(vs_ref, jnp.maximum(t - 1, 0) * sub, sub, p_refs[nxt][...])
        s = s_refs[cur][...]
        if masked:
            s = jnp.where(t * tile + key_col <= qpos, s, MASK_NEG)
        m_new = jnp.maximum(m, jnp.max(s, axis=0, keepdims=True))
        alpha = jnp.exp2(m - m_new)
        p_refs[cur][...] = jnp.exp2(s - m_new).astype(BF16)
        return m_new, acc, alpha

    s_refs = (sa_ref, sb_ref)
    p_refs = (pa_ref, pb_ref)
    sa_ref[...] = scores(0)
    pb_ref[...] = jnp.zeros(pb_ref.shape, BF16)
    init = (jnp.full((1, cols), MASK_NEG, F32), jnp.zeros((V_ROWS, cols), F32), jnp.ones((1, cols), F32))

    def pair(u, carry):
        carry = stage(2 * u, 0, 1, carry, False, True)
        return stage(2 * u + 1, 1, 0, carry, False, True)

    u_diag = q0 // (2 * tile)
    carry = lax.fori_loop(0, u_diag, pair, init)
    carry = stage(2 * u_diag, 0, 1, carry, True, True)

    def finish_second(c):
        _, acc, alpha = stage(2 * u_diag + 1, 1, 0, c, True, False)
        return alpha * acc + weighted_values(vs_ref, (2 * u_diag + 1) * sub, sub, pb_ref[...])

    def finish_first(c):
        _, acc, alpha = c
        return alpha * acc + weighted_values(vs_ref, 2 * u_diag * sub, sub, pa_ref[...])

    o_sel = normalise(lax.cond(q0 - 2 * u_diag * tile >= tile, finish_second, finish_first, carry))

    no_rows = jnp.zeros((HEAD_DIM, qb), BF16)
    for h in range(N_HEADS):
        c = slice(h * qb, (h + 1) * qb)
        own = h * PAD_HEAD + (h // GROUP) * HEAD_DIM
        other = h * PAD_HEAD + (1 - h // GROUP) * HEAD_DIM
        g0 = (h // GROUP) * LANES + 3 * (h % GROUP)
        o = (gt_ref[g0:g0 + 1, :] * ocmp_ref[own:own + HEAD_DIM, :] + gt_ref[g0 + 1:g0 + 2, :] * o_sel[:, c]
             + gt_ref[g0 + 2:g0 + 3, :] * o_win[:, c])
        o_ref[own:own + HEAD_DIM, :] = o.astype(BF16)
        o_ref[other:other + HEAD_DIM, :] = no_rows


def _selwin_prompt(qr_t, mneg, ks, vs_t, kw, vw_t, ocmp_t, gates_t):
    b, _, t = qr_t.shape
    n_sel = mneg.shape[2]
    qb = SELWIN_QUERIES
    assert t % (2 * SEL_TILE) == 0 and SEL_TILE % qb == 0 and t >= WINDOW + qb
    cols = N_HEADS * qb
    per_q = lambda r: pl.BlockSpec((None, r, qb), lambda bi, i: (bi, 0, i))
    per_b = lambda *shape: pl.BlockSpec((None,) + shape, lambda bi, i: (bi,) + (0,) * len(shape),
                                        pipeline_mode=pl.Buffered(1))
    vals = per_b(N_KV_HEADS, t // LANES, V_ROWS, LANES)
    return pl.pallas_call(
        _selwin_prompt_kernel,
        grid=(b, t // qb),
        in_specs=[
            per_q(Q_PAD),
            pl.BlockSpec((None, N_KV_HEADS, n_sel, qb), lambda bi, i: (bi, 0, 0, i)),
            per_b(t, LANES + n_sel), vals, per_b(t, LANES), vals,
            per_q(Q_PAD), per_q(N_KV_HEADS * LANES),
        ],
        out_specs=per_q(Q_PAD),
        out_shape=jax.ShapeDtypeStruct((b, Q_PAD, t), BF16),
        scratch_shapes=[pltpu.VMEM((PAD_HEAD + n_sel, cols), BF16),
                        pltpu.VMEM((SEL_TILE, cols), F32),
                        pltpu.VMEM((SEL_TILE, cols), F32),
                        pltpu.VMEM((SEL_TILE, cols), BF16),
                        pltpu.VMEM((SEL_TILE, cols), BF16)],
        compiler_params=_cparams(("parallel", "arbitrary")),
        name="selwin_prompt",
    )(qr_t, mneg, ks, vs_t, kw, vw_t, ocmp_t, gates_t)


PAGES_PER_STEP = 16
S_COLS = LANES


def _cmp_topk_sample_kernel(n_t, pt_ref, *refs):
    page_refs = refs[:PAGES_PER_STEP]
    (q_ref, pe_ref, w1_ref, b1_ref, w2_ref, b2_ref,
     ocmp_ref, idx_ref, kcv_ref, pk_ref, carry_ref, xk_ref, xv_ref) = refs[PAGES_PER_STEP:]
    s = pl.program_id(1)
    n_steps = pl.num_programs(1)
    n_e = kcv_ref.shape[0]
    step_e = PAGES_PER_STEP * PAGE_SIZE // CMP_STRIDE

    @pl.when(s == 0)
    def _():
        carry_ref[...] = jnp.zeros(carry_ref.shape, F32)

    for k, page in enumerate(page_refs):
        rows = slice(k * PAGE_SIZE, (k + 1) * PAGE_SIZE)
        xk_ref[rows, :] = page[0:LANES, :].T
        xv_ref[rows, :] = page[LANES:KV_ROW, :].T
    out = _compress_rows([(xk_ref, xv_ref)], pe_ref, w1_ref, b1_ref, w2_ref, b2_ref, carry_ref)
    kcv_ref[pl.ds(pl.multiple_of(s * step_e, step_e), step_e), :] = out.astype(BF16)

    @pl.when(s == n_steps - 1)
    def _():
        n_sel = n_e // (SEL_BLOCK // CMP_STRIDE) + 1
        n_sel_rows = pk_ref.shape[0] // (SEL_BLOCK // CMP_STRIDE) - 2
        kc = kcv_ref[:, 0:LANES]
        vc = kcv_ref[:, LANES:KV_ROW]
        e_idx = lax.broadcasted_iota(I32, (n_e, S_COLS), 0)
        p = _masked_softmax_rows(_dot_nt(kc, q_ref[...]), e_idx >= 1)
        ocmp_ref[...] = _dot_tn(p.astype(BF16), vc)
        pkv = p
        for hh in range(1, GROUP):
            pkv = pkv + pltpu.roll(p, S_COLS - n_t * hh, 1)
        pk_ref[0:n_e, :] = pkv
        pk_ref[n_e:, :] = jnp.zeros((pk_ref.shape[0] - n_e, S_COLS), F32)
        slc = _selection_scores(pk_ref, n_sel_rows)
        j = lax.broadcasted_iota(I32, (n_sel_rows, S_COLS), 0)
        qp = PAST_LEN + (lax.broadcasted_iota(I32, (n_sel_rows, S_COLS), 1) & (n_t - 1))
        qblk = jnp.right_shift(qp, 6)
        forced = (j == 0) | (j == qblk) | (j == qblk - 1)
        in_range = j < n_sel
        score = jnp.where(forced & in_range, jnp.inf,
                          jnp.where((j * SEL_BLOCK <= qp) & in_range, slc, -jnp.inf))
        _, picks = _topk_rounds(score, n_sel_rows)
        for r, mi in enumerate(picks):
            idx_ref[r:r + 1, :] = mi.astype(I32)


def _cmp_topk_sample(page_table, cache_cmp_t, layer, q_cols, cw, n_t):
    assert n_t & (n_t - 1) == 0 and N_HEADS * n_t <= S_COLS
    b, n_pages = page_table.shape
    n_e = n_pages * PAGE_SIZE // CMP_STRIDE
    n_sel_rows = ((n_e // 4 + 1) + 7) // 8 * 8
    pk_rows = 4 * (n_sel_rows + 2)
    n_steps = n_pages // PAGES_PER_STEP
    step_rows = PAGES_PER_STEP * PAGE_SIZE

    def page_spec(k):
        return pl.BlockSpec((None, None, KV_ROW, PAGE_SIZE),
                            lambda bi, s, pt: (layer, pt[bi, s * PAGES_PER_STEP + k], 0, 0))

    per_b = lambda rows, w: pl.BlockSpec((None, rows, w), lambda bi, s, pt: (bi, 0, 0))
    grid_spec = pltpu.PrefetchScalarGridSpec(
        num_scalar_prefetch=1,
        grid=(b, n_steps),
        in_specs=[page_spec(k) for k in range(PAGES_PER_STEP)]
        + [per_b(S_COLS, LANES)] + _cmp_weight_specs(),
        out_specs=[per_b(S_COLS, LANES), per_b(N_SEL, S_COLS)],
        scratch_shapes=[pltpu.VMEM((n_e, KV_ROW), BF16),
                        pltpu.VMEM((pk_rows, S_COLS), F32),
                        pltpu.VMEM((4, 8, CMP_HIDDEN), F32),
                        pltpu.VMEM((step_rows, LANES), F32),
                        pltpu.VMEM((step_rows, LANES), F32)],
    )
    return pl.pallas_call(
        functools.partial(_cmp_topk_sample_kernel, n_t),
        grid_spec=grid_spec,
        out_shape=[jax.ShapeDtypeStruct((b, S_COLS, LANES), F32),
                   jax.ShapeDtypeStruct((b, N_SEL, S_COLS), I32)],
        compiler_params=_cparams(("parallel", "arbitrary")),
        name="cmp_topk_sample",
    )(page_table, *([cache_cmp_t] * PAGES_PER_STEP), q_cols, *_cmp_weight_args(cw))


HROWS = 8


def _selwin_sample_kernel(idx_ref, pt_ref, *refs):
    page_refs = refs[:N_SEL]
    (q_ref, new_s_ref, win_ref, new_w_ref, ocmp_ref, gt_ref, o_ref, k_ref, v_ref) = refs[N_SEL:]
    bi = pl.program_id(0)
    tq = pl.program_id(1)
    kh = pl.program_id(2)
    n_t = pl.num_programs(1)
    qpos = PAST_LEN + tq
    q = q_ref[...]
    base = ((bi * n_t + tq) * N_KV_HEADS + kh) * N_SEL
    new_block = PAST_LEN // SEL_BLOCK

    valid_parts = []
    lane = lax.broadcasted_iota(I32, (1, PAGE_SIZE), 1)
    picked_new = False
    for r in range(N_SEL):
        j = idx_ref[base + r]
        cols = slice(r * PAGE_SIZE, (r + 1) * PAGE_SIZE)
        k_ref[:, cols] = page_refs[r][0:LANES, :].astype(BF16)
        v_ref[:, cols] = page_refs[r][LANES:KV_ROW, :].astype(BF16)
        kpos = jnp.right_shift(j, 1) * PAGE_SIZE + lane
        valid_parts.append((jnp.right_shift(kpos, 6) == j) & (j != new_block))
        picked_new = jnp.logical_or(picked_new, j == new_block)
    cols = slice(N_SEL * PAGE_SIZE, (N_SEL + 1) * PAGE_SIZE)
    k_ref[:, cols] = new_s_ref[0:LANES, :].astype(BF16)
    v_ref[:, cols] = new_s_ref[LANES:KV_ROW, :].astype(BF16)
    new_pos = PAST_LEN + lane
    last_new = jnp.where(picked_new, qpos, -1)
    valid_parts.append((jnp.right_shift(new_pos, 6) == new_block) & (new_pos <= last_new))
    valid = jnp.concatenate(valid_parts, axis=1)
    s = jnp.where(valid, _dot(q, k_ref[...]), -jnp.inf)
    m = jnp.max(s, axis=-1, keepdims=True)
    p = jnp.exp(s - m)
    o_sel = _dot_nt(p.astype(BF16), v_ref[...]) / jnp.sum(p, axis=-1, keepdims=True)

    wb = win_ref.shape[1]
    s_old = _dot(q, win_ref[0:LANES, :].astype(BF16))
    d_old = qpos - (PAST_LEN - wb + lax.broadcasted_iota(I32, (1, wb), 1))
    s_old = jnp.where((d_old >= 0) & (d_old <= WINDOW), s_old, -jnp.inf)
    s_new = _dot(q, new_w_ref[0:LANES, :].astype(BF16))
    d_new = tq - lax.broadcasted_iota(I32, (1, new_w_ref.shape[1]), 1)
    s_new = jnp.where((d_new >= 0) & (d_new <= WINDOW), s_new, -jnp.inf)
    m = jnp.maximum(jnp.max(s_old, axis=-1, keepdims=True), jnp.max(s_new, axis=-1, keepdims=True))
    p_old = jnp.exp(s_old - m)
    p_new = jnp.exp(s_new - m)
    den = jnp.sum(p_old, axis=-1, keepdims=True) + jnp.sum(p_new, axis=-1, keepdims=True)
    o_win = (_dot_nt(p_old.astype(BF16), win_ref[LANES:KV_ROW, :].astype(BF16))
             + _dot_nt(p_new.astype(BF16), new_w_ref[LANES:KV_ROW, :].astype(BF16))) / den

    g = gt_ref[...]
    o = g[:, 0:1] * ocmp_ref[...] + g[:, 1:2] * o_sel + g[:, 2:3] * o_win
    o_ref[...] = o.astype(BF16)


def _selwin_sample(idx_flat, page_table, cache_sel_t, win_t, layer, q_rows, new_s_t, new_w_t, ocmp_rows, gate_rows):
    b, n_t = q_rows.shape[:2]
    wb = win_t.shape[-1]
    n_t_static = n_t
    last_page = page_table.shape[1] - 1

    def page_spec(r):
        def imap(bi, tq, kh, idx, pt):
            j = idx[((bi * n_t_static + tq) * N_KV_HEADS + kh) * N_SEL + r]
            return (layer, pt[bi, jnp.minimum(jnp.right_shift(j, 1), last_page)], 0, 0)
        return pl.BlockSpec((None, None, KV_ROW, PAGE_SIZE), imap)

    row5 = pl.BlockSpec((None, None, None, HROWS, LANES), lambda bi, tq, kh, idx, pt: (bi, tq, kh, 0, 0))
    new_rows = pl.BlockSpec((None, KV_ROW, LANES), lambda bi, tq, kh, idx, pt: (bi, 0, 0))
    grid_spec = pltpu.PrefetchScalarGridSpec(
        num_scalar_prefetch=2,
        grid=(b, n_t, N_KV_HEADS),
        in_specs=[page_spec(r) for r in range(N_SEL)]
        + [row5, new_rows,
           pl.BlockSpec((None, None, KV_ROW, wb), lambda bi, tq, kh, idx, pt: (layer, bi, 0, 0)),
           new_rows, row5, row5],
        out_specs=row5,
        scratch_shapes=[pltpu.VMEM((LANES, (N_SEL + 1) * PAGE_SIZE), BF16),
                        pltpu.VMEM((LANES, (N_SEL + 1) * PAGE_SIZE), BF16)],
    )
    return pl.pallas_call(
        _selwin_sample_kernel,
        grid_spec=grid_spec,
        out_shape=jax.ShapeDtypeStruct((b, n_t, N_KV_HEADS, HROWS, LANES), BF16),
        compiler_params=_cparams(("arbitrary", "arbitrary", "arbitrary")),
        name="selwin_sample",
    )(idx_flat, page_table, *([cache_sel_t] * N_SEL), q_rows, new_s_t, win_t, new_w_t, ocmp_rows, gate_rows)


def _outproj_ln_kernel(attn_transposed, x_ref, yc_ref, yl_ref, ya_ref, w_ref, g_ref, b_ref,
                       wg_ref, wu_ref, wo_ref, g2_ref, b2_ref, o_ref):
    y = _dot(yc_ref[...], w_ref[0:D_CONV, :])
    y = y + _dot(yl_ref[...], w_ref[D_CONV:D_CONV + D_LRU, :])
    w_attn = w_ref[D_CONV + D_LRU:, :]
    y = y + (_dot_tn(ya_ref[...], w_attn) if attn_transposed else _dot(ya_ref[...], w_attn))
    x = _layernorm(ALPHA * x_ref[...] + y, g_ref[...], b_ref[...])
    o_ref[...] = _ffn_ln_apply(x, wg_ref, wu_ref, wo_ref, g2_ref, b2_ref)


def _outproj_ffn(x, yc, yl, ya, ow, fw, tm):
    n = x.shape[0]
    tok = lambda w: pl.BlockSpec((tm, w), lambda i: (i, 0))
    attn_transposed = ya.ndim == 3
    if attn_transposed:
        tps = ya.shape[2] // tm
        ya_spec = pl.BlockSpec((None, Q_PAD, tm), lambda i: (i // tps, 0, i % tps))
    else:
        ya_spec = tok(Q_PAD)
    return pl.pallas_call(
        functools.partial(_outproj_ln_kernel, attn_transposed),
        grid=(n // tm,),
        in_specs=[tok(D_MODEL), tok(D_CONV), tok(D_LRU), ya_spec,
                  _const_spec((D_CONV + D_LRU + Q_PAD, D_MODEL)),
                  _const_spec((1, D_MODEL)), _const_spec((1, D_MODEL))] + _ffn_weight_specs(),
        out_specs=tok(D_MODEL),
        out_shape=jax.ShapeDtypeStruct((n, D_MODEL), F32),
        compiler_params=_cparams(("parallel",)),
        name="outproj_ffn",
    )(x, yc, yl, ya, ow["w"], ow["g"], ow["b"], *_ffn_weight_args(fw))


def _rope_tables(pos):
    half = ROPE_DIM // 2
    inv = ROPE_THETA ** (-jnp.arange(half, dtype=F32) / half)
    ang = pos.astype(F32)[:, None] * inv[None, :]
    cos, sin = jnp.cos(ang), jnp.sin(ang)
    n = pos.shape[0]
    rest = HEAD_DIM - ROPE_DIM
    zeros8 = jnp.zeros((n, half), F32)
    c = jnp.concatenate([cos, cos, jnp.ones((n, rest), F32)], axis=1)
    s1 = jnp.concatenate([zeros8, sin, jnp.zeros((n, rest), F32)], axis=1)
    s2 = jnp.concatenate([-sin, zeros8, jnp.zeros((n, rest), F32)], axis=1)
    rep = LANES // HEAD_DIM
    return tuple(jnp.tile(a, (1, rep)) for a in (c, s1, s2))


def _head_pad_index():
    h = np.arange(D_ATTN) // HEAD_DIM
    d = np.arange(D_ATTN) % HEAD_DIM
    return h * PAD_HEAD + (h // GROUP) * HEAD_DIM + d


def _prep_layer(l, ln_g, ln_b, ffn_w_in, ffn_w_out, w_in, conv_w, conv_b, conv_ln_g, conv_ln_b,
                lru_conv_w, lru_conv_b, lru_w_gate, lru_b_gate, lru_lambda,
                cmp_pe, cmp_w1, cmp_b1, cmp_w2, cmp_b2, w_out):
    row = lambda v: v.reshape(1, -1).astype(F32)
    ffn = []
    for f, ln_i in ((0, 0), (1, 2)):
        wi = ffn_w_in[l, f]
        wg = wi[:, :D_FF].reshape(D_MODEL, N_FF_CHUNKS, FF_CHUNK).transpose(1, 0, 2).astype(BF16)
        wu = wi[:, D_FF:].reshape(D_MODEL, N_FF_CHUNKS, FF_CHUNK).transpose(1, 0, 2).astype(BF16)
        wo = ffn_w_out[l, f].reshape(N_FF_CHUNKS, FF_CHUNK, D_MODEL).astype(BF16)
        ffn.append({"wg": wg, "wu": wu, "wo": wo, "g": row(ln_g[l, ln_i]), "b": row(ln_b[l, ln_i])})

    wl = w_in[l]
    o_q = 2 * D_CONV + 2 * D_LRU
    o_kv = o_q + D_ATTN
    o_g = o_kv + 3 * KV_ROW
    pad_idx = _head_pad_index()
    wq = jnp.zeros((D_MODEL, Q_PAD), F32).at[:, pad_idx].set(wl[:, o_q:o_kv])
    hh = np.arange(3 * N_HEADS) // 3
    gate_idx = (hh // GROUP) * LANES + (hh % GROUP) * 3 + np.arange(3 * N_HEADS) % 3
    wgt = jnp.zeros((D_MODEL, N_KV_HEADS * LANES), F32).at[:, gate_idx].set(wl[:, o_g:])
    w_all = jnp.concatenate([wl[:, :o_q], wq, wl[:, o_kv:o_g], wgt], axis=1).astype(BF16)
    w_t = jnp.concatenate([wl[:, o_q:o_kv].astype(BF16), w_all[:, C_KVS + LANES:C_KVW],
                           w_all[:, C_KVW + LANES:C_GATE], w_all[:, C_GATE:]], axis=1).T

    conv = {"w": jnp.pad(conv_w[l], ((0, CONV_HALO - CONV_WIDTH), (0, 0))),
            "b": row(conv_b[l]), "ln_g": row(conv_ln_g[l]), "ln_b": row(conv_ln_b[l])}

    def blockdiag(w):
        out = jnp.zeros((D_LRU, D_LRU), F32)
        for n in range(LRU_BLOCKS):
            out = out.at[n * LRU_BW:(n + 1) * LRU_BW, n * LRU_BW:(n + 1) * LRU_BW].set(w[n])
        return out.astype(BF16)

    lru = {"cw": jnp.pad(lru_conv_w[l], ((0, LRU_HALO - LRU_CONV_WIDTH), (0, 0))),
           "cb": row(lru_conv_b[l]),
           "wr": blockdiag(lru_w_gate[l, 0]), "wi": blockdiag(lru_w_gate[l, 1]),
           "bg": lru_b_gate[l].astype(F32), "lam": row(lru_lambda[l])}

    pe = cmp_pe[l].reshape(2, 2, CMP_STRIDE * HEAD_DIM)
    pe_rows = jnp.stack([jnp.stack([pe[g // 2, j] for g in range(4)]) for j in range(2)])
    w2e = jnp.zeros((4, CMP_HIDDEN, KV_ROW), F32)
    for g in range(4):
        w2e = w2e.at[g, :, g * HEAD_DIM:(g + 1) * HEAD_DIM].set(cmp_w2[l, g // 2])
    cmp = {"pe": pe_rows.astype(F32),
           "w1": cmp_w1[l].reshape(2, 2, CMP_STRIDE * HEAD_DIM, CMP_HIDDEN).astype(BF16),
           "b1": cmp_b1[l].astype(F32),
           "w2": w2e.astype(BF16),
           "b2": jnp.concatenate([cmp_b2[l, 0], cmp_b2[l, 0], cmp_b2[l, 1], cmp_b2[l, 1]]).reshape(1, -1)}

    wo = w_out[l]
    wo_attn = jnp.zeros((Q_PAD, D_MODEL), F32).at[pad_idx, :].set(wo[D_CONV + D_LRU:])
    out = {"w": jnp.concatenate([wo[:D_CONV + D_LRU], wo_attn], axis=0).astype(BF16),
           "g": row(ln_g[l, 1]), "b": row(ln_b[l, 1])}
    return {"ffn": ffn, "w_all": w_all, "w_t": w_t, "conv": conv, "lru": lru, "cmp": cmp, "out": out}


def _pad_front(a, rows):
    return jnp.pad(a, ((0, 0), (rows - a.shape[1], 0), (0, 0)))


def _kv6(a, lead):
    return a.reshape(lead + (2, N_KV_HEADS, HEAD_DIM))


TM_PROMPT = 512
TC_PROMPT = 512
TL_PROMPT = 256


def _layer_prompt(x, bsz, t, lw, tabs):
    n = bsz * t
    x = _ffn_ln(x, lw["ffn"][0], TM_PROMPT)
    tabs_t = tuple(a.T for a in tabs)
    (u, lx, lg, kvc, ks, kw, kvc_t, kvs_t, kvw_t, qc_t, qr_t, vs_t, vw_t, gates_t) = _inproj_t(
        x, lw["w_all"], lw["w_t"], tabs, tabs_t, bsz, t, TM_PROMPT)
    s3 = lambda a: a.reshape(bsz, t, a.shape[-1])
    u3, lx3 = s3(u), s3(lx)
    yc = _conv_group(u3, jnp.zeros((bsz, CONV_HALO, D_CONV), F32), lw["conv"], TC_PROMPT)
    yl, h_last = _lru_group(lx3, s3(lg), jnp.zeros((bsz, LRU_HALO, D_LRU), F32),
                            jnp.zeros((bsz, LRU_HALO, D_LRU), F32), lw["lru"], TL_PROMPT, TL_PROMPT - 1)
    kc, vc_t = _compress_prompt(s3(kvc), lw["cmp"])
    ocmp_t, mneg = _cmp_topk_prompt(qc_t, kc, vc_t)
    ya_t = _selwin_prompt(qr_t, mneg, s3(ks), vs_t, s3(kw), vw_t, ocmp_t, gates_t)
    x = _outproj_ffn(x, yc.reshape(n, D_CONV), yl.reshape(n, D_LRU), ya_t, lw["out"], lw["ffn"][1], TM_PROMPT)
    leaf = lambda a: a.reshape(bsz, 2, N_KV_HEADS, HEAD_DIM, a.shape[-1]).transpose(0, 4, 1, 2, 3)
    state = (u3[:, t - (CONV_WIDTH - 1):], lx3[:, t - (LRU_CONV_WIDTH - 1):], h_last[:, 0],
             leaf(kvc_t), leaf(kvs_t), leaf(kvw_t[:, :, t - min(WINDOW, t):]))
    return x, state


T_PAD = 8


def _layer_sample(x, bsz, t, lw, tabs, conv_buf, lru_buf, lru_h, cache_cmp_t, cache_sel_t, win_t, layer, page_table):
    n = bsz * t
    x = _ffn_ln(x, lw["ffn"][0], n)
    (u, lx, lg, qc, qr, kvc, kvs, kvw, gates) = _inproj(x, lw["w_all"], tabs, n, 1)
    s3 = lambda a: a.reshape(bsz, t, a.shape[-1])
    padt = lambda a: jnp.pad(s3(a), ((0, 0), (0, T_PAD - t), (0, 0)))
    yc = _conv_group(padt(u), _pad_front(conv_buf, CONV_HALO), lw["conv"], T_PAD)[:, :t]
    h0 = jnp.broadcast_to(lru_h[:, None, :], (bsz, LRU_HALO, D_LRU))
    yl, h_all = _lru_group(padt(lx), padt(lg), _pad_front(lru_buf, LRU_HALO), h0, lw["lru"], T_PAD, t - 1)
    yl = yl[:, :t]

    def head_cols(a):
        a = a.reshape(bsz, t, N_HEADS, PAD_HEAD).transpose(0, 2, 1, 3).reshape(bsz, N_HEADS * t, PAD_HEAD)
        return jnp.pad(a, ((0, 0), (0, S_COLS - N_HEADS * t), (0, 0)))

    ocmp_cols, picks = _cmp_topk_sample(page_table, cache_cmp_t, layer, head_cols(qc), lw["cmp"], t)
    pk = picks[:, :, :N_HEADS * t].reshape(bsz, N_SEL, N_KV_HEADS, GROUP, t)[:, :, :, 0, :]
    idx_flat = pk.transpose(0, 3, 2, 1).reshape(-1).astype(I32)

    def head_rows(a, dt):
        a = a.reshape(bsz, N_KV_HEADS, GROUP, t, PAD_HEAD).transpose(0, 3, 1, 2, 4)
        return jnp.pad(a, ((0, 0), (0, 0), (0, 0), (0, HROWS - GROUP), (0, 0))).astype(dt)

    q_rows = head_rows(qr.reshape(bsz, t, N_HEADS, PAD_HEAD).transpose(0, 2, 1, 3).reshape(bsz, N_HEADS * t, PAD_HEAD), BF16)
    ocmp_rows = head_rows(ocmp_cols[:, :N_HEADS * t], F32)
    g3 = gates.reshape(bsz, t, N_KV_HEADS, LANES)[..., :3 * GROUP].reshape(bsz, t, N_KV_HEADS, GROUP, 3)
    gate_rows = jnp.pad(g3, ((0, 0), (0, 0), (0, 0), (0, HROWS - GROUP), (0, LANES - 3)))
    pos_last = lambda a: s3(a).transpose(0, 2, 1)
    pad_cols = lambda a: jnp.pad(a, ((0, 0), (0, 0), (0, LANES - t)))
    kvs_t, kvw_t = pos_last(kvs), pos_last(kvw)
    ya_rows = _selwin_sample(idx_flat, page_table, cache_sel_t, win_t, layer, q_rows, pad_cols(kvs_t),
                             pad_cols(kvw_t), ocmp_rows, gate_rows)
    ya = ya_rows[:, :, :, :GROUP].reshape(bsz, t, Q_PAD)

    x = _outproj_ffn(x, yc.reshape(n, D_CONV), yl.reshape(n, D_LRU), ya.reshape(n, Q_PAD), lw["out"], lw["ffn"][1], n)
    new_conv = jnp.concatenate([conv_buf, s3(u)], axis=1)[:, t:]
    new_lru = jnp.concatenate([lru_buf, s3(lx)], axis=1)[:, t:]
    win_all = jnp.concatenate([win_t[layer], kvw_t], axis=-1)
    n_win = min(WINDOW, win_all.shape[-1])
    new_win = win_all[..., win_all.shape[-1] - n_win:]
    new_win = new_win.reshape(bsz, 2, N_KV_HEADS, HEAD_DIM, n_win).transpose(0, 4, 1, 2, 3)
    state = (new_conv, new_lru, h_all[:, 0], _kv6(kvc, (bsz, t)), _kv6(kvs, (bsz, t)), new_win)
    return x, state


def kernel(x_prompt, x_sample, state_conv, state_lru_conv, state_lru_h, cache_cmp_kv, cache_sel_kv, cache_win_kv,
           page_table, ln_g, ln_b, ffn_w_in, ffn_w_out, w_in, conv_w, conv_b, conv_ln_g, conv_ln_b,
           lru_conv_w, lru_conv_b, lru_w_gate, lru_b_gate, lru_lambda, cmp_pe, cmp_w1, cmp_b1, cmp_w2, cmp_b2, w_out):
    bp, tp, _ = x_prompt.shape
    bs, ts, _ = x_sample.shape
    depth = ln_g.shape[0]
    past = page_table.shape[1] * PAGE_SIZE
    assert past == PAST_LEN and past % SEL_BLOCK == 0
    tabs_p = _rope_tables(jnp.arange(tp))
    tabs_s = _rope_tables(jnp.tile(past + jnp.arange(ts), bs))
    xp = x_prompt.reshape(bp * tp, D_MODEL)
    xs = x_sample.reshape(bs * ts, D_MODEL)
    pos_last = lambda c: jnp.transpose(c, (0, 1, 3, 4, 5, 2)).reshape(c.shape[:2] + (KV_ROW, c.shape[2]))
    cache_cmp_t, cache_sel_t, win_t = pos_last(cache_cmp_kv), pos_last(cache_sel_kv), pos_last(cache_win_kv)
    st_p, st_s = [], []
    for l in range(depth):
        lw = _prep_layer(l, ln_g, ln_b, ffn_w_in, ffn_w_out, w_in, conv_w, conv_b, conv_ln_g, conv_ln_b,
                         lru_conv_w, lru_conv_b, lru_w_gate, lru_b_gate, lru_lambda,
                         cmp_pe, cmp_w1, cmp_b1, cmp_w2, cmp_b2, w_out)
        xp, sp = _layer_prompt(xp, bp, tp, lw, tabs_p)
        xs, ss = _layer_sample(
            xs, bs, ts, lw, tabs_s, state_conv[l], state_lru_conv[l], state_lru_h[l],
            cache_cmp_t, cache_sel_t, win_t, l, page_table)
        st_p.append(sp)
        st_s.append(ss)
    outs = [xp.reshape(bp, tp, D_MODEL), xs.reshape(bs, ts, D_MODEL)]
    for k in range(6):
        outs.append(jnp.stack([s[k] for s in st_p]))
        outs.append(jnp.stack([s[k] for s in st_s]))
    return tuple(outs)
```
